```python
import math
import jax, jax.numpy as jnp
from jax import lax
import numpy as np

D_MODEL = 4096
BATCH = 4
SEQ = 2048
DEPTH = 2
DEC_BATCH = 32
DEC_SEQ = 4
PAST_LEN = 16384
PAGE_SIZE = 128

N_HEADS = 32
HEAD_DIM = D_MODEL // N_HEADS
SWA_KV_HEADS = 8
SWA_WINDOW = 128
NSA_KV_HEADS = 4
CMP_STRIDE = 16
CMP_LEN = 2 * CMP_STRIDE
CMP_HID = HEAD_DIM
SEL_BLOCK = 64
SEL_TOPN = 16
NSA_WINDOW = 512
D_FF = ((8 * D_MODEL // 3 + 255) // 256) * 256
BAND_BLOCK = 128
SEL_Q_CHUNK = 64
N_MIXERS = 2
N_SWA_LAYERS = (DEPTH + 1) // 2
N_NSA_LAYERS = DEPTH // 2
RMS_EPS = 1e-6
NEG_FILL = -1e30

kernel_name = "hybrid_swa_sink_nsa_macaron_alibi_step"


def rms_norm(x, g):
    xf = x.astype(jnp.float32)
    y = xf * lax.rsqrt(jnp.mean(xf * xf, axis=-1, keepdims=True) + RMS_EPS)
    return y.astype(x.dtype) * g


def swiglu(h, w_in, w_out):
    gate, up = jnp.split(h @ w_in, 2, axis=-1)
    return (jax.nn.silu(gate) * up) @ w_out


def alibi_slopes(kv_heads):
    m = jnp.exp2(-8.0 * jnp.arange(1, N_HEADS + 1, dtype=jnp.float32) / N_HEADS)
    return m.reshape(kv_heads, N_HEADS // kv_heads)


def masked_softmax(s, mask, sink=None):
    s = jnp.where(mask, s, NEG_FILL)
    m = jnp.max(s, axis=-1, keepdims=True)
    if sink is not None:
        m = jnp.maximum(m, sink)
    p = jnp.where(mask, jnp.exp(s - m), 0.0)
    denom = jnp.sum(p, axis=-1, keepdims=True)
    if sink is not None:
        denom = denom + jnp.exp(sink - m)
    return p / jnp.maximum(denom, 1e-30)


def local_attn(q, k, v, q_pos, k_pos, window, slopes, sink=None):
    s = jnp.einsum('...qkgd,...skd->...qkgs', q, k).astype(jnp.float32) * (HEAD_DIM ** -0.5)
    dist = q_pos[..., :, None] - k_pos[..., None, :]
    valid = (dist >= 0) & (dist <= window) & (k_pos[..., None, :] >= 0)
    dist = dist[..., :, None, None, :].astype(jnp.float32)
    valid = valid[..., :, None, None, :]
    s = s - slopes[:, :, None] * dist
    p = masked_softmax(s, valid, sink)
    return jnp.einsum('...qkgs,...skd->...qkgd', p.astype(v.dtype), v)


def banded_attn(q, k, v, window, slopes, sink=None):
    b, t = q.shape[:2]
    nb = t // BAND_BLOCK
    nprev = -(-window // BAND_BLOCK)
    pad = nprev * BAND_BLOCK
    kp = jnp.pad(k, ((0, 0), (pad, 0), (0, 0), (0, 0)))
    vp = jnp.pad(v, ((0, 0), (pad, 0), (0, 0), (0, 0)))

    def band(a):
        return jnp.concatenate(
            [a[:, j * BAND_BLOCK: j * BAND_BLOCK + t].reshape(b, nb, BAND_BLOCK, *a.shape[2:])
             for j in range(nprev + 1)], axis=2)

    qb = q.reshape(b, nb, BAND_BLOCK, *q.shape[2:])
    q_pos = jnp.arange(t, dtype=jnp.int32).reshape(nb, BAND_BLOCK)
    k_pos = (jnp.arange(nb, dtype=jnp.int32)[:, None] * BAND_BLOCK
             + jnp.arange((nprev + 1) * BAND_BLOCK, dtype=jnp.int32)[None, :] - pad)
    o = local_attn(qb, band(kp), band(vp), q_pos, k_pos, window, slopes, sink)
    return o.reshape(q.shape)


def compress(k, pe, w1, w2):
    b, lk = k.shape[:2]
    nch = lk // CMP_STRIDE
    ch = k[:, :nch * CMP_STRIDE].reshape(b, nch, CMP_STRIDE, *k.shape[2:])
    blk = jnp.concatenate([ch[:, :-1], ch[:, 1:]], axis=2) + pe[:, None, :]
    h = jax.nn.gelu(jnp.einsum('bclkd,ldf->bckf', blk, w1))
    return jnp.einsum('bckf,fd->bckd', h, w2)


def nsa_attend(q, q_pos, kc, vc, ks, vs, slopes):
    scale = HEAD_DIM ** -0.5
    b, tq, kvh, g, _ = q.shape
    lk = ks.shape[1]
    nc = kc.shape[1]
    c_idx = jnp.arange(nc, dtype=jnp.int32)
    c_start = c_idx * CMP_STRIDE
    c_end = c_start + CMP_LEN - 1
    c_mid = c_start.astype(jnp.float32) + (CMP_LEN - 1) / 2
    tf = q_pos.astype(jnp.float32)
    s_c = jnp.einsum('bqkgd,bckd->bqkgc', q, kc).astype(jnp.float32) * scale
    s_c = s_c - slopes[:, :, None] * (tf[:, None] - c_mid[None, :])[:, None, None, :]
    mask_c = (c_end[None, :] <= q_pos[:, None])[:, None, None, :]
    p_c = masked_softmax(s_c, mask_c)
    o_c = jnp.einsum('bqkgc,bckd->bqkgd', p_c.astype(vc.dtype), vc)
    ns = -(-lk // SEL_BLOCK)
    j = jnp.arange(ns, dtype=jnp.int32)
    overlap = jnp.clip(jnp.minimum(c_start[:, None] + CMP_LEN, (j[None, :] + 1) * SEL_BLOCK)
                       - jnp.maximum(c_start[:, None], j[None, :] * SEL_BLOCK), 0, None)
    overlap = overlap.astype(jnp.float32) / CMP_LEN
    imp = jnp.einsum('bqkc,cj->bqkj', jnp.sum(p_c, axis=3), overlap)
    cur = q_pos // SEL_BLOCK
    forced = (j[None, :] == 0) | (j[None, :] == cur[:, None]) | (j[None, :] == cur[:, None] - 1)
    visible = j[None, :] * SEL_BLOCK <= q_pos[:, None]
    rank = jnp.where(forced[None, :, None, :], 1e9, jnp.where(visible[None, :, None, :], imp, -1.0))
    _, idx = lax.top_k(rank, min(SEL_TOPN, ns))
    pad_len = ns * SEL_BLOCK - lk

    def blocks(a):
        a = jnp.pad(a, ((0, 0), (0, pad_len), (0, 0), (0, 0)))
        return a.reshape(b, ns, SEL_BLOCK, kvh, a.shape[-1]).transpose(0, 3, 1, 2, 4)

    bi = jnp.arange(b)[:, None, None, None]
    hi = jnp.arange(kvh)[None, None, :, None]
    kg = blocks(ks)[bi, hi, idx]
    vg = blocks(vs)[bi, hi, idx]
    pos = idx[..., None] * SEL_BLOCK + jnp.arange(SEL_BLOCK, dtype=jnp.int32)
    dist = q_pos[None, :, None, None, None] - pos
    s_s = jnp.einsum('bqkgd,bqknrd->bqkgnr', q, kg).astype(jnp.float32) * scale
    s_s = s_s - slopes[None, None, :, :, None, None] * dist[:, :, :, None].astype(jnp.float32)
    mask_s = (dist >= 0)[:, :, :, None]
    p_s = masked_softmax(s_s.reshape(b, tq, kvh, g, -1),
                         mask_s.reshape(b, tq, kvh, 1, -1)).reshape(s_s.shape)
    o_s = jnp.einsum('bqkgnr,bqknrd->bqkgd', p_s.astype(vs.dtype), vg)
    return o_c, o_s


def swa_mixer(h, w_in, w_out, sinks, buf=None):
    b, t, _ = h.shape
    nq = N_HEADS * HEAD_DIM
    proj = h @ w_in
    q = proj[..., :nq].reshape(b, t, SWA_KV_HEADS, -1, HEAD_DIM)
    kv = proj[..., nq:].reshape(b, t, 2, SWA_KV_HEADS, HEAD_DIM)
    slopes = alibi_slopes(SWA_KV_HEADS)
    sink = sinks.astype(jnp.float32).reshape(SWA_KV_HEADS, -1)[:, :, None]
    if buf is None:
        o = banded_attn(q, kv[:, :, 0], kv[:, :, 1], SWA_WINDOW, slopes, sink)
        new_buf = kv[:, t - min(SWA_WINDOW, t):]
    else:
        nbuf = buf.shape[1]
        kv_all = jnp.concatenate([buf, kv.astype(buf.dtype)], axis=1)
        q_pos = PAST_LEN + jnp.arange(t, dtype=jnp.int32)
        k_pos = jnp.concatenate([PAST_LEN - nbuf + jnp.arange(nbuf, dtype=jnp.int32), q_pos])
        o = local_attn(q, kv_all[:, :, 0], kv_all[:, :, 1], q_pos, k_pos, SWA_WINDOW, slopes, sink)
        new_buf = kv_all[:, t:]
    y = o.reshape(b, t, -1) @ w_out
    return y, new_buf


def nsa_mixer(h, w_in, gate_b, cmp_pe, cmp_w1, cmp_w2, w_out,
              win_buf=None, pool=None, layer=0, page_table=None):
    b, t, _ = h.shape
    nq = N_HEADS * HEAD_DIM
    nkv = 6 * NSA_KV_HEADS * HEAD_DIM
    proj = h @ w_in
    q = proj[..., :nq].reshape(b, t, NSA_KV_HEADS, -1, HEAD_DIM)
    kv = proj[..., nq:nq + nkv].reshape(b, t, 6, NSA_KV_HEADS, HEAD_DIM)
    gate = jax.nn.sigmoid((proj[..., nq + nkv:] + gate_b).astype(jnp.float32))
    gate = gate.reshape(b, t, 3, NSA_KV_HEADS, -1, 1)
    slopes = alibi_slopes(NSA_KV_HEADS)
    rows = kv[:, :, :4]

    def cmp_kv(a):
        return (compress(a[:, :, 0], cmp_pe[0], cmp_w1[0], cmp_w2[0]),
                compress(a[:, :, 1], cmp_pe[1], cmp_w1[1], cmp_w2[1]))

    if win_buf is None:
        o_w = banded_attn(q, kv[:, :, 4], kv[:, :, 5], NSA_WINDOW, slopes)
        kc, vc = cmp_kv(rows)
        n_chunk = t // SEL_Q_CHUNK
        qc = q.reshape(b, n_chunk, SEL_Q_CHUNK, *q.shape[2:]).swapaxes(0, 1)
        pc = jnp.arange(t, dtype=jnp.int32).reshape(n_chunk, SEL_Q_CHUNK)

        def one_chunk(args):
            qq, pp = args
            return nsa_attend(qq, pp, kc, vc, kv[:, :, 2], kv[:, :, 3], slopes)

        o_c, o_s = lax.map(one_chunk, (qc, pc))
        o_c = o_c.swapaxes(0, 1).reshape(q.shape)
        o_s = o_s.swapaxes(0, 1).reshape(q.shape)
        new_win = kv[:, t - min(NSA_WINDOW, t):, 4:]
    else:
        nbuf = win_buf.shape[1]
        win_all = jnp.concatenate([win_buf, kv[:, :, 4:].astype(win_buf.dtype)], axis=1)
        q_pos = PAST_LEN + jnp.arange(t, dtype=jnp.int32)
        k_pos = jnp.concatenate([PAST_LEN - nbuf + jnp.arange(nbuf, dtype=jnp.int32), q_pos])
        o_w = local_attn(q, win_all[:, :, 0], win_all[:, :, 1], q_pos, k_pos, NSA_WINDOW, slopes)

        def one_seq(args):
            pt, qq, new_rows = args
            past = pool[layer, pt].reshape(-1, 4, NSA_KV_HEADS, HEAD_DIM)
            full = jnp.concatenate([past, new_rows.astype(past.dtype)], axis=0)[None]
            kc, vc = cmp_kv(full)
            oc, os_ = nsa_attend(qq[None], q_pos, kc, vc, full[:, :, 2], full[:, :, 3], slopes)
            return oc[0], os_[0]

        o_c, o_s = lax.map(one_seq, (page_table, q, rows))
        new_win = win_all[:, t:]
    o = gate[:, :, 0] * o_c + gate[:, :, 1] * o_s + gate[:, :, 2] * o_w
    y = o.astype(h.dtype).reshape(b, t, -1) @ w_out
    return y, rows, new_win


def setup_inputs(seed: int = 0) -> dict:
    key = jax.random.key(seed)
    ks = jax.random.split(key, 20)
    n_pages = PAST_LEN // PAGE_SIZE
    in_use = DEC_BATCH * n_pages
    n_pool = in_use + max(1, in_use // 4)
    f32 = jnp.float32
    cache_nsa_kv = jax.random.normal(ks[0], (N_NSA_LAYERS, n_pool, PAGE_SIZE, 4, NSA_KV_HEADS, HEAD_DIM), f32)
    page_table = jax.random.permutation(ks[1], n_pool)[:in_use].reshape(DEC_BATCH, n_pages).astype(jnp.int32)

    def nrm(k, shape, scale):
        return jax.random.normal(k, shape, f32) * scale

    nq = N_HEADS * HEAD_DIM
    return {
        "x_prompt": jax.random.normal(ks[2], (BATCH, SEQ, D_MODEL), f32),
        "x_sample": jax.random.normal(ks[3], (DEC_BATCH, DEC_SEQ, D_MODEL), f32),
        "cache_swa_kv": jax.random.normal(ks[4], (N_SWA_LAYERS, DEC_BATCH, min(SWA_WINDOW, PAST_LEN), 2, SWA_KV_HEADS, HEAD_DIM), f32),
        "cache_nsa_win_kv": jax.random.normal(ks[5], (N_NSA_LAYERS, DEC_BATCH, min(NSA_WINDOW, PAST_LEN), 2, NSA_KV_HEADS, HEAD_DIM), f32),
        "cache_nsa_kv": cache_nsa_kv,
        "page_table": page_table,
        "norm_g": 1.0 + nrm(ks[6], (DEPTH, 3, D_MODEL), 0.01),
        "final_norm_g": 1.0 + nrm(ks[7], (D_MODEL,), 0.01),
        "ffn_w_in": nrm(ks[8], (DEPTH, 2, D_MODEL, 2 * D_FF), D_MODEL ** -0.5),
        "ffn_w_out": nrm(ks[9], (DEPTH, 2, D_FF, D_MODEL), D_FF ** -0.5),
        "swa_w_in": nrm(ks[10], (N_SWA_LAYERS, D_MODEL, nq + 2 * SWA_KV_HEADS * HEAD_DIM), D_MODEL ** -0.5),
        "swa_w_out": nrm(ks[11], (N_SWA_LAYERS, nq, D_MODEL), nq ** -0.5),
        "swa_sinks": nrm(ks[12], (N_SWA_LAYERS, N_HEADS), 1.0),
        "nsa_w_in": nrm(ks[13], (N_NSA_LAYERS, D_MODEL, nq + 6 * NSA_KV_HEADS * HEAD_DIM + 3 * N_HEADS), D_MODEL ** -0.5),
        "nsa_gate_b": nrm(ks[14], (N_NSA_LAYERS, 3 * N_HEADS), 0.1),
        "nsa_cmp_pe": nrm(ks[15], (N_NSA_LAYERS, 2, CMP_LEN, HEAD_DIM), 0.1),
        "nsa_cmp_w1": nrm(ks[16], (N_NSA_LAYERS, 2, CMP_LEN, HEAD_DIM, CMP_HID), (CMP_LEN * HEAD_DIM) ** -0.5),
        "nsa_cmp_w2": nrm(ks[17], (N_NSA_LAYERS, 2, CMP_HID, HEAD_DIM), CMP_HID ** -0.5),
        "nsa_w_out": nrm(ks[18], (N_NSA_LAYERS, nq, D_MODEL), nq ** -0.5),
    }


def reference(x_prompt, x_sample, cache_swa_kv, cache_nsa_win_kv, cache_nsa_kv, page_table,
              norm_g, final_norm_g, ffn_w_in, ffn_w_out, swa_w_in, swa_w_out, swa_sinks,
              nsa_w_in, nsa_gate_b, nsa_cmp_pe, nsa_cmp_w1, nsa_cmp_w2, nsa_w_out):
    xp, xs = x_prompt, x_sample
    swa_p, swa_s, win_p, win_s, kv_p, kv_s = [], [], [], [], [], []
    for i in range(DEPTH):
        xp = xp + 0.5 * swiglu(rms_norm(xp, norm_g[i, 0]), ffn_w_in[i, 0], ffn_w_out[i, 0])
        xs = xs + 0.5 * swiglu(rms_norm(xs, norm_g[i, 0]), ffn_w_in[i, 0], ffn_w_out[i, 0])
        hp = rms_norm(xp, norm_g[i, 1])
        hs = rms_norm(xs, norm_g[i, 1])
        li = i // N_MIXERS
        if i % N_MIXERS == 0:
            yp, bp = swa_mixer(hp, swa_w_in[li], swa_w_out[li], swa_sinks[li])
            ys, bs = swa_mixer(hs, swa_w_in[li], swa_w_out[li], swa_sinks[li], buf=cache_swa_kv[li])
            swa_p.append(bp)
            swa_s.append(bs)
        else:
            yp, rp, wp = nsa_mixer(hp, nsa_w_in[li], nsa_gate_b[li], nsa_cmp_pe[li], nsa_cmp_w1[li],
                                   nsa_cmp_w2[li], nsa_w_out[li])
            ys, rs, ws = nsa_mixer(hs, nsa_w_in[li], nsa_gate_b[li], nsa_cmp_pe[li], nsa_cmp_w1[li],
                                   nsa_cmp_w2[li], nsa_w_out[li], win_buf=cache_nsa_win_kv[li],
                                   pool=cache_nsa_kv, layer=li, page_table=page_table)
            kv_p.append(rp)
            kv_s.append(rs)
            win_p.append(wp)
            win_s.append(ws)
        xp = xp + yp
        xs = xs + ys
        xp = xp + 0.5 * swiglu(rms_norm(xp, norm_g[i, 2]), ffn_w_in[i, 1], ffn_w_out[i, 1])
        xs = xs + 0.5 * swiglu(rms_norm(xs, norm_g[i, 2]), ffn_w_in[i, 1], ffn_w_out[i, 1])
    y_prompt = rms_norm(xp, final_norm_g)
    y_sample = rms_norm(xs, final_norm_g)
    return (y_prompt, y_sample, jnp.stack(swa_p), jnp.stack(swa_s), jnp.stack(win_p), jnp.stack(win_s),
            jnp.stack(kv_p), jnp.stack(kv_s))
```

```python
import functools
import math

import jax
import jax.numpy as jnp
from jax import lax
from jax.experimental import pallas as pl
from jax.experimental.pallas import tpu as pltpu

D_MODEL = 4096
BATCH = 4
SEQ = 2048
DEPTH = 2
DEC_BATCH = 32
DEC_SEQ = 4
PAST_LEN = 16384
PAGE_SIZE = 128
N_HEADS = 32
HEAD_DIM = 128
SWA_KV_HEADS = 8
SWA_WINDOW = 128
NSA_KV_HEADS = 4
CMP_STRIDE = 16
CMP_LEN = 32
SEL_BLOCK = 64
SEL_TOPN = 16
NSA_WINDOW = 512
D_FF = 11008
BAND_BLOCK = 128
N_MIXERS = 2
RMS_EPS = 1e-6
NEG_FILL = -1e30
SCALE = HEAD_DIM ** -0.5

M_PROMPT = BATCH * SEQ
M_SAMPLE = DEC_BATCH * DEC_SEQ
M_ALL = M_PROMPT + M_SAMPLE
DEC_PAD = 8
LANE = 128
F32 = jnp.float32
BF16 = jnp.bfloat16
VMEM_LIMIT = 60 * 1024 * 1024


def _params(sem):
    return pltpu.CompilerParams(dimension_semantics=sem, vmem_limit_bytes=VMEM_LIMIT)


def _rms_kernel(x_ref, g_ref, o_ref):
    x = x_ref[...]
    y = x * lax.rsqrt(jnp.mean(x * x, axis=-1, keepdims=True) + RMS_EPS)
    o_ref[...] = (y * g_ref[...]).astype(o_ref.dtype)


def rms_norm(x, g, out_dtype, tm, row_block0=0, rows=None):
    m, d = x.shape
    rows = m if rows is None else rows
    return pl.pallas_call(
        _rms_kernel,
        grid=(rows // tm,),
        in_specs=[pl.BlockSpec((tm, d), lambda i: (row_block0 + i, 0)),
                  pl.BlockSpec((1, d), lambda i: (0, 0))],
        out_specs=pl.BlockSpec((tm, d), lambda i: (i, 0)),
        out_shape=jax.ShapeDtypeStruct((rows, d), out_dtype),
        compiler_params=_params(("arbitrary",)),
        name="rms_norm",
    )(x, g.reshape(1, d))


def _mm_kernel(x_ref, w_ref, *rest, scale, has_res):
    if has_res:
        res_ref, o_ref, wbf_ref = rest
    else:
        o_ref, wbf_ref = rest

    @pl.when(pl.program_id(1) == 0)
    def _():
        wbf_ref[...] = w_ref[...].astype(BF16)

    acc = jnp.dot(x_ref[...], wbf_ref[...], preferred_element_type=F32)
    if has_res:
        o_ref[...] = res_ref[...] + (acc if scale == 1.0 else scale * acc)
    else:
        o_ref[...] = acc.astype(o_ref.dtype)


def matmul_wres(x, w, w_prefix, n_cols, tm, tn, res=None, scale=1.0, col_block0=0):
    m, k = x.shape
    npre = len(w_prefix)
    w_block = (None,) * npre + (k, tn)
    in_specs = [pl.BlockSpec((tm, k), lambda j, i: (i, 0)),
                pl.BlockSpec(w_block, lambda j, i: tuple(w_prefix) + (0, col_block0 + j))]
    args = [x, w]
    if res is not None:
        in_specs.append(pl.BlockSpec((tm, tn), lambda j, i: (i, j)))
        args.append(res)
    return pl.pallas_call(
        functools.partial(_mm_kernel, scale=scale, has_res=res is not None),
        grid=(n_cols // tn, m // tm),
        in_specs=in_specs,
        out_specs=pl.BlockSpec((tm, tn), lambda j, i: (i, j)),
        out_shape=jax.ShapeDtypeStruct((m, n_cols), F32),
        scratch_shapes=[pltpu.VMEM((k, tn), BF16)],
        compiler_params=_params(("arbitrary", "arbitrary")),
        name="matmul_wres",
    )(*args)


def _ffn_in_kernel(x_ref, wg_ref, wu_ref, o_ref, wg_bf, wu_bf):
    @pl.when(pl.program_id(1) == 0)
    def _():
        wg_bf[...] = wg_ref[...].astype(BF16)
        wu_bf[...] = wu_ref[...].astype(BF16)

    x = x_ref[...]
    g = jnp.dot(x, wg_bf[...], preferred_element_type=F32)
    u = jnp.dot(x, wu_bf[...], preferred_element_type=F32)
    o_ref[...] = (jax.nn.silu(g) * u).astype(o_ref.dtype)


def ffn_in(xn, w_in, layer, sub, tm, tf):
    m, k = xn.shape
    nf = D_FF // tf
    w_block = (None, None, k, tf)
    return pl.pallas_call(
        _ffn_in_kernel,
        grid=(nf, m // tm),
        in_specs=[pl.BlockSpec((tm, k), lambda j, i: (i, 0)),
                  pl.BlockSpec(w_block, lambda j, i: (layer, sub, 0, j)),
                  pl.BlockSpec(w_block, lambda j, i: (layer, sub, 0, nf + j))],
        out_specs=pl.BlockSpec((tm, tf), lambda j, i: (i, j)),
        out_shape=jax.ShapeDtypeStruct((m, D_FF), BF16),
        scratch_shapes=[pltpu.VMEM((k, tf), BF16), pltpu.VMEM((k, tf), BF16)],
        compiler_params=_params(("arbitrary", "arbitrary")),
        name="ffn_in",
    )(xn, w_in, w_in)


def _ffn_out_kernel(h_ref, w_ref, res_ref, o_ref):
    acc = jnp.dot(h_ref[...], w_ref[...], preferred_element_type=F32)
    o_ref[...] = res_ref[...] + 0.5 * acc


def ffn_out(h, w_out_bf, layer, sub, res, tm, tn):
    m, k = h.shape
    n = res.shape[1]
    return pl.pallas_call(
        _ffn_out_kernel,
        grid=(m // tm, n // tn),
        in_specs=[pl.BlockSpec((tm, k), lambda i, j: (i, 0)),
                  pl.BlockSpec((None, None, k, tn), lambda i, j: (layer, sub, 0, j)),
                  pl.BlockSpec((tm, tn), lambda i, j: (i, j))],
        out_specs=pl.BlockSpec((tm, tn), lambda i, j: (i, j)),
        out_shape=jax.ShapeDtypeStruct((m, n), F32),
        compiler_params=_params(("arbitrary", "arbitrary")),
        name="ffn_out",
    )(h, w_out_bf, res)


def _stack_heads(q, g_count):
    return jnp.concatenate([q[:, g * HEAD_DIM:(g + 1) * HEAD_DIM] for g in range(g_count)], axis=0)


def _softmax_pv(s, valid, v_bf, sink=None):
    s = jnp.where(valid, s, NEG_FILL)
    m = jnp.max(s, axis=-1, keepdims=True)
    if sink is not None:
        m = jnp.maximum(m, sink)
    p = jnp.where(valid, jnp.exp(s - m), 0.0)
    den = jnp.sum(p, axis=-1, keepdims=True)
    if sink is not None:
        den = den + jnp.exp(sink - m)
    o = jnp.dot(p.astype(BF16), v_bf, preferred_element_type=F32)
    return o / jnp.maximum(den, 1e-30)


def _band_kernel(slopes_ref, sinks_ref, q_ref, k_ref, v_ref, o_ref, *, g_count, nprev, window, use_sink):
    kvh = pl.program_id(1)
    i = pl.program_id(2)
    width = (nprev + 1) * BAND_BLOCK
    start = pl.multiple_of(jnp.maximum(i - nprev, 0) * BAND_BLOCK, BAND_BLOCK)
    k = k_ref[pl.ds(start, width), :].astype(BF16)
    v = v_ref[pl.ds(start, width), :].astype(BF16)
    qs = _stack_heads(q_ref[...], g_count).astype(BF16)
    s = lax.dot_general(qs, k, (((1,), (1,)), ((), ())), preferred_element_type=F32) * SCALE
    tq = i * BAND_BLOCK + lax.broadcasted_iota(jnp.int32, (BAND_BLOCK, width), 0)
    kp = start + lax.broadcasted_iota(jnp.int32, (BAND_BLOCK, width), 1)
    dist = tq - kp
    valid = (dist >= 0) & (dist <= window)
    distf = dist.astype(F32)
    outs = []
    for g in range(g_count):
        h = kvh * g_count + g
        sg = s[g * BAND_BLOCK:(g + 1) * BAND_BLOCK] - slopes_ref[h] * distf
        outs.append(_softmax_pv(sg, valid, v, sinks_ref[h] if use_sink else None))
    o_ref[...] = jnp.concatenate(outs, axis=1).astype(o_ref.dtype)


def band_attention(proj, slopes, sinks, *, batch, seq, kv_heads, k_col0, v_col0, window, use_sink, out_dtype):
    g_count = N_HEADS // kv_heads
    nb = seq // BAND_BLOCK
    nprev = -(-window // BAND_BLOCK)
    kb0, vb0 = k_col0 // HEAD_DIM, v_col0 // HEAD_DIM
    smem = pl.BlockSpec(memory_space=pltpu.SMEM)
    return pl.pallas_call(
        functools.partial(_band_kernel, g_count=g_count, nprev=nprev, window=window, use_sink=use_sink),
        grid=(batch, kv_heads, nb),
        in_specs=[smem, smem,
                  pl.BlockSpec((BAND_BLOCK, g_count * HEAD_DIM), lambda b, h, i: (b * nb + i, h)),
                  pl.BlockSpec((seq, HEAD_DIM), lambda b, h, i: (b, kb0 + h)),
                  pl.BlockSpec((seq, HEAD_DIM), lambda b, h, i: (b, vb0 + h))],
        out_specs=pl.BlockSpec((BAND_BLOCK, g_count * HEAD_DIM), lambda b, h, i: (b * nb + i, h)),
        out_shape=jax.ShapeDtypeStruct((batch * seq, N_HEADS * HEAD_DIM), out_dtype),
        compiler_params=_params(("arbitrary", "arbitrary", "arbitrary")),
        name="band_attention",
    )(slopes, sinks, proj, proj, proj)


def _dec_kernel(slopes_ref, sinks_ref, qkv_ref, c_ref, o_ref, *, kv_heads, g_count, lc, k_col0, v_col0,
                window, use_sink):
    rows = g_count * DEC_PAD
    width = lc + LANE
    t = lax.broadcasted_iota(jnp.int32, (DEC_PAD, width), 0)
    col = lax.broadcasted_iota(jnp.int32, (DEC_PAD, width), 1)
    dist = jnp.where(col < lc, t + (lc - col), t - (col - lc))
    valid = (dist >= 0) & (dist <= window) & (col < lc + DEC_SEQ)
    distf = dist.astype(F32)
    pad = jnp.zeros((LANE - DEC_PAD, HEAD_DIM), F32)
    for kvh in range(kv_heads):
        q0 = kvh * g_count * HEAD_DIM
        qs = _stack_heads(qkv_ref[:, q0:q0 + g_count * HEAD_DIM], g_count).astype(BF16)
        kn = qkv_ref[:, k_col0 + kvh * HEAD_DIM:k_col0 + (kvh + 1) * HEAD_DIM]
        vn = qkv_ref[:, v_col0 + kvh * HEAD_DIM:v_col0 + (kvh + 1) * HEAD_DIM]
        kc = c_ref[:, kvh * HEAD_DIM:(kvh + 1) * HEAD_DIM]
        vc = c_ref[:, (kv_heads + kvh) * HEAD_DIM:(kv_heads + kvh + 1) * HEAD_DIM]
        kall = jnp.concatenate([kc, kn, pad], axis=0).astype(BF16)
        vall = jnp.concatenate([vc, vn, pad], axis=0).astype(BF16)
        s = lax.dot_general(qs, kall, (((1,), (1,)), ((), ())), preferred_element_type=F32) * SCALE
        assert s.shape == (rows, width)
        for g in range(g_count):
            h = kvh * g_count + g
            sg = s[g * DEC_PAD:(g + 1) * DEC_PAD] - slopes_ref[h] * distf
            o = _softmax_pv(sg, valid, vall, sinks_ref[h] if use_sink else None)
            o_ref[:, h * HEAD_DIM:(h + 1) * HEAD_DIM] = o.astype(o_ref.dtype)


def decode_attention(qkv8, cache, slopes, sinks, *, kv_heads, k_col0, v_col0, window, use_sink, out_dtype):
    nb, _, ncols = qkv8.shape
    lc = cache.shape[1]
    g_count = N_HEADS // kv_heads
    smem = pl.BlockSpec(memory_space=pltpu.SMEM)
    return pl.pallas_call(
        functools.partial(_dec_kernel, kv_heads=kv_heads, g_count=g_count, lc=lc, k_col0=k_col0, v_col0=v_col0,
                          window=window, use_sink=use_sink),
        grid=(nb,),
        in_specs=[smem, smem,
                  pl.BlockSpec((None, DEC_PAD, ncols), lambda b: (b, 0, 0)),
                  pl.BlockSpec((None, lc, cache.shape[2]), lambda b: (b, 0, 0))],
        out_specs=pl.BlockSpec((None, DEC_PAD, N_HEADS * HEAD_DIM), lambda b: (b, 0, 0)),
        out_shape=jax.ShapeDtypeStruct((nb, DEC_PAD, N_HEADS * HEAD_DIM), out_dtype),
        compiler_params=_params(("arbitrary",)),
        name="decode_attention",
    )(slopes, sinks, qkv8, cache)


def _pe_term_kernel(pe_ref, w1_ref, o_ref):
    acc = jnp.zeros((8, HEAD_DIM), F32)
    for l in range(CMP_LEN):
        row = jnp.broadcast_to(pe_ref[l:l + 1, :], (8, HEAD_DIM)).astype(BF16)
        acc = acc + jnp.dot(row, w1_ref[l].astype(BF16), preferred_element_type=F32)
    o_ref[...] = acc


def pe_term(pe, w1, layer):
    return pl.pallas_call(
        _pe_term_kernel,
        grid=(2,),
        in_specs=[pl.BlockSpec((None, None, CMP_LEN, HEAD_DIM), lambda s: (layer, s, 0, 0)),
                  pl.BlockSpec((None, None, CMP_LEN, HEAD_DIM, HEAD_DIM), lambda s: (layer, s, 0, 0, 0))],
        out_specs=pl.BlockSpec((None, 8, HEAD_DIM), lambda s: (s, 0, 0)),
        out_shape=jax.ShapeDtypeStruct((2, 8, HEAD_DIM), F32),
        compiler_params=_params(("arbitrary",)),
        name="pe_term",
    )(pe, w1)


def _compress_finish(acc_a, acc_b, pet_row, w2_bf):
    n = acc_a.shape[0]
    pre = acc_a + pltpu.roll(acc_b, n - 1, 0) + pet_row
    return jnp.dot(jax.nn.gelu(pre).astype(BF16), w2_bf, preferred_element_type=F32)


def _cmp_prompt_kernel(x_ref, w1_ref, w2_ref, pet_ref, o_ref, *, nch):
    acc_a = jnp.zeros((nch, HEAD_DIM), F32)
    acc_b = jnp.zeros((nch, HEAD_DIM), F32)
    for l in range(CMP_STRIDE):
        xl = x_ref[pl.ds(l, nch, stride=CMP_STRIDE), :].astype(BF16)
        acc_a = acc_a + jnp.dot(xl, w1_ref[l].astype(BF16), preferred_element_type=F32)
        acc_b = acc_b + jnp.dot(xl, w1_ref[CMP_STRIDE + l].astype(BF16), preferred_element_type=F32)
    o_ref[...] = _compress_finish(acc_a, acc_b, pet_ref[0:1, :], w2_ref[...].astype(BF16))


def compress_prompt(proj, w1, w2, pet, layer, *, batch, seq, col0):
    nch = seq // CMP_STRIDE
    cb0 = col0 // HEAD_DIM
    return pl.pallas_call(
        functools.partial(_cmp_prompt_kernel, nch=nch),
        grid=(batch, 2, NSA_KV_HEADS),
        in_specs=[pl.BlockSpec((seq, HEAD_DIM), lambda b, s, h: (b, cb0 + s * NSA_KV_HEADS + h)),
                  pl.BlockSpec((None, None, CMP_LEN, HEAD_DIM, HEAD_DIM), lambda b, s, h: (layer, s, 0, 0, 0)),
                  pl.BlockSpec((None, None, HEAD_DIM, HEAD_DIM), lambda b, s, h: (layer, s, 0, 0)),
                  pl.BlockSpec((None, 8, HEAD_DIM), lambda b, s, h: (s, 0, 0))],
        out_specs=pl.BlockSpec((None, None, None, nch, HEAD_DIM), lambda b, s, h: (b, s, h, 0, 0)),
        out_shape=jax.ShapeDtypeStruct((batch, 2, NSA_KV_HEADS, nch, HEAD_DIM), F32),
        compiler_params=_params(("arbitrary", "arbitrary", "arbitrary")),
        name="compress_prompt",
    )(proj, w1, w2, pet)


CMP_PAGES = 16
CHUNKS_PER_PAGE = PAGE_SIZE // CMP_STRIDE
CMP_HEADS = 2 * NSA_KV_HEADS
CMP_ROWS = CMP_PAGES * CHUNKS_PER_PAGE * CMP_HEADS


def _slot_split(x, slot1):
    return jnp.concatenate([jnp.where(slot1, 0.0, x), jnp.where(slot1, x, 0.0)], axis=1).astype(BF16)


def _slot1_rows(rows):
    return (lax.broadcasted_iota(jnp.int32, (rows, HEAD_DIM), 0) % CMP_HEADS) >= NSA_KV_HEADS


def _cmp_pages_kernel(pt_ref, *refs):
    pages = refs[:CMP_PAGES]
    w_ref, a_ref, b_ref = refs[CMP_PAGES:]
    slot1 = _slot1_rows(CMP_ROWS)
    acc = jnp.zeros((CMP_ROWS, 2 * HEAD_DIM), F32)
    for l in range(CMP_STRIDE):
        x = jnp.concatenate([pages[r][l + CMP_STRIDE * c] for r in range(CMP_PAGES) for c in range(CHUNKS_PER_PAGE)],
                            axis=0)
        acc = acc + jnp.dot(_slot_split(x, slot1), w_ref[l], preferred_element_type=F32)
    a_ref[...] = acc[:, :HEAD_DIM]
    b_ref[...] = acc[:, HEAD_DIM:]


def compress_pages_partial(pool4, page_table, w1_slots, layer_page0):
    nb, n_pages = page_table.shape
    n_groups = n_pages // CMP_PAGES

    def page_spec(r):
        return pl.BlockSpec((None, PAGE_SIZE, CMP_HEADS, HEAD_DIM),
                            lambda b, j, pt: (layer_page0 + pt[b, j * CMP_PAGES + r], 0, 0, 0))

    out_spec = pl.BlockSpec((None, CMP_ROWS, HEAD_DIM), lambda b, j, pt: (b, j, 0))
    out_shape = jax.ShapeDtypeStruct((nb, n_groups * CMP_ROWS, HEAD_DIM), F32)
    grid_spec = pltpu.PrefetchScalarGridSpec(
        num_scalar_prefetch=1,
        grid=(nb, n_groups),
        in_specs=[page_spec(r) for r in range(CMP_PAGES)] + [pl.BlockSpec(w1_slots.shape, lambda b, j, pt: (0, 0, 0))],
        out_specs=[out_spec, out_spec],
    )
    return pl.pallas_call(
        _cmp_pages_kernel,
        grid_spec=grid_spec,
        out_shape=[out_shape, out_shape],
        compiler_params=_params(("arbitrary", "arbitrary")),
        name="compress_pages_partial",
    )(page_table, *([pool4] * CMP_PAGES), w1_slots)


def _cmp_finish_kernel(a_ref, b_ref, bnext_ref, w2_ref, pet_ref, o_ref, scr_ref):
    rows = a_ref.shape[0]
    last_part = pl.program_id(1) == pl.num_programs(1) - 1
    b_shift = jnp.concatenate([b_ref[CMP_HEADS:, :], bnext_ref[:CMP_HEADS, :]], axis=0)
    pet = jnp.concatenate([pet_ref[...]] * (rows // CMP_HEADS), axis=0)
    h = jax.nn.gelu(a_ref[...] + b_shift + pet)
    res = jnp.dot(_slot_split(h, _slot1_rows(rows)), w2_ref[...], preferred_element_type=F32)
    row = lax.broadcasted_iota(jnp.int32, (rows, HEAD_DIM), 0)
    scr_ref[...] = jnp.where(row >= jnp.where(last_part, rows - CMP_HEADS, rows), 0.0, res)
    for sk in range(CMP_HEADS):
        o_ref[sk] = scr_ref[pl.ds(sk, rows // CMP_HEADS, stride=CMP_HEADS), :]


def compress_pages_finish(a, b, w2_slots, pet_slots):
    nb, total_rows, _ = a.shape
    n_parts = total_rows // CMP_ROWS
    per_part = CMP_ROWS // CMP_HEADS
    blk = (None, CMP_ROWS, HEAD_DIM)
    out = pl.pallas_call(
        _cmp_finish_kernel,
        grid=(nb, n_parts),
        in_specs=[pl.BlockSpec(blk, lambda b_, p: (b_, p, 0)),
                  pl.BlockSpec(blk, lambda b_, p: (b_, p, 0)),
                  pl.BlockSpec(blk, lambda b_, p: (b_, jnp.minimum(p + 1, n_parts - 1), 0)),
                  pl.BlockSpec(w2_slots.shape, lambda b_, p: (0, 0)),
                  pl.BlockSpec(pet_slots.shape, lambda b_, p: (0, 0))],
        out_specs=pl.BlockSpec((None, CMP_HEADS, per_part, HEAD_DIM), lambda b_, p: (b_, 0, p, 0)),
        out_shape=jax.ShapeDtypeStruct((nb, CMP_HEADS, n_parts * per_part, HEAD_DIM), F32),
        scratch_shapes=[pltpu.VMEM((CMP_ROWS, HEAD_DIM), F32)],
        compiler_params=_params(("arbitrary", "arbitrary")),
        name="compress_pages_finish",
    )(a, b, b, w2_slots, pet_slots)
    return out.reshape(nb, 2, NSA_KV_HEADS, n_parts * per_part, HEAD_DIM)


def _split3(x):
    hi = x.astype(BF16)
    r1 = x - hi.astype(F32)
    mid = r1.astype(BF16)
    lo = (r1 - mid.astype(F32)).astype(BF16)
    return hi, mid, lo


def _cmp_attn_kernel(slopes_ref, q_ref, kc_ref, vc_ref, ov_ref, o_ref, sel_ref, *, tb, nc, n_sel, pos0, emit_idx):
    kvh = pl.program_id(1)
    i = pl.program_id(2)
    g_count = N_HEADS // NSA_KV_HEADS
    ncp = kc_ref.shape[0]
    nsp = ov_ref.shape[1]
    qs = _stack_heads(q_ref[...], g_count).astype(BF16)
    kc = kc_ref[...].astype(BF16)
    vc = vc_ref[...].astype(BF16)
    s = lax.dot_general(qs, kc, (((1,), (1,)), ((), ())), preferred_element_type=F32) * SCALE
    t = pos0 + i * tb + lax.broadcasted_iota(jnp.int32, (tb, ncp), 0)
    c = lax.broadcasted_iota(jnp.int32, (tb, ncp), 1)
    valid = (c * CMP_STRIDE + (CMP_LEN - 1) <= t) & (c < nc)
    rel = t.astype(F32) - ((c * CMP_STRIDE).astype(F32) + (CMP_LEN - 1) / 2)
    psum = jnp.zeros((tb, ncp), F32)
    outs = []
    for g in range(g_count):
        sg = s[g * tb:(g + 1) * tb] - slopes_ref[kvh * g_count + g] * rel
        sg = jnp.where(valid, sg, NEG_FILL)
        m = jnp.max(sg, axis=-1, keepdims=True)
        p = jnp.where(valid, jnp.exp(sg - m), 0.0)
        p = p / jnp.maximum(jnp.sum(p, axis=-1, keepdims=True), 1e-30)
        psum = psum + p
        outs.append(jnp.dot(p.astype(BF16), vc, preferred_element_type=F32))
    o_ref[...] = jnp.concatenate(outs, axis=1)

    ov = ov_ref[...]
    imp = sum(jnp.dot(part, ov, preferred_element_type=F32) for part in _split3(psum))
    tq = pos0 + i * tb + lax.broadcasted_iota(jnp.int32, (tb, nsp), 0)
    j = lax.broadcasted_iota(jnp.int32, (tb, nsp), 1)
    cur = tq // SEL_BLOCK
    forced = (j == 0) | (j == cur) | (j == cur - 1)
    visible = j * SEL_BLOCK <= tq
    rank = jnp.where(forced, 1e9, jnp.where(visible, imp, -1.0))
    rank = jnp.where(j < n_sel, rank, -2.0)
    cnt = jnp.zeros((tb, nsp), jnp.int32)
    for jp in range(n_sel):
        col = rank[:, jp:jp + 1]
        beats = (col > rank) | ((col == rank) & (j > jp))
        cnt = cnt + beats.astype(jnp.int32)
    if emit_idx:
        lane = lax.broadcasted_iota(jnp.int32, (tb, LANE), 1)
        idx = jnp.zeros((tb, LANE), jnp.int32)
        jf = j.astype(F32)
        for r in range(SEL_TOPN):
            val = jnp.sum(jnp.where(cnt == r, jf, 0.0), axis=1, keepdims=True).astype(jnp.int32)
            idx = jnp.where(lane == r, val, idx)
        sel_ref[...] = idx
    else:
        sel_ref[...] = (cnt < SEL_TOPN).astype(F32)


def _overlap_matrix(ncp, n_sel, nsp):
    c_start = jnp.arange(ncp, dtype=jnp.int32)[:, None] * CMP_STRIDE
    j = jnp.arange(nsp, dtype=jnp.int32)[None, :]
    ov = jnp.clip(jnp.minimum(c_start + CMP_LEN, (j + 1) * SEL_BLOCK) - jnp.maximum(c_start, j * SEL_BLOCK), 0, None)
    ov = jnp.where(j < n_sel, ov, 0)
    return (ov.astype(F32) / CMP_LEN).astype(BF16)


def cmp_attention_prompt(proj, cmp_kv, slopes, *, batch, seq):
    g_count = N_HEADS // NSA_KV_HEADS
    tb = BAND_BLOCK
    nb = seq // tb
    ncp = cmp_kv.shape[3]
    n_sel = -(-seq // SEL_BLOCK)
    nsp = LANE
    ov = _overlap_matrix(ncp, n_sel, nsp)
    smem = pl.BlockSpec(memory_space=pltpu.SMEM)
    kv_block = (None, None, None, ncp, HEAD_DIM)
    return pl.pallas_call(
        functools.partial(_cmp_attn_kernel, tb=tb, nc=ncp - 1, n_sel=n_sel, pos0=0, emit_idx=False),
        grid=(batch, NSA_KV_HEADS, nb),
        in_specs=[smem,
                  pl.BlockSpec((tb, g_count * HEAD_DIM), lambda b, h, i: (b * nb + i, h)),
                  pl.BlockSpec(kv_block, lambda b, h, i: (b, 0, h, 0, 0)),
                  pl.BlockSpec(kv_block, lambda b, h, i: (b, 1, h, 0, 0)),
                  pl.BlockSpec((ncp, nsp), lambda b, h, i: (0, 0))],
        out_specs=[pl.BlockSpec((tb, g_count * HEAD_DIM), lambda b, h, i: (b * nb + i, h)),
                   pl.BlockSpec((None, None, tb, nsp), lambda b, h, i: (b, h, i, 0))],
        out_shape=[jax.ShapeDtypeStruct((batch * seq, N_HEADS * HEAD_DIM), F32),
                   jax.ShapeDtypeStruct((batch, NSA_KV_HEADS, seq, nsp), F32)],
        compiler_params=_params(("arbitrary", "arbitrary", "arbitrary")),
        name="cmp_attention_prompt",
    )(slopes, proj, cmp_kv, cmp_kv, ov)


def cmp_attention_sample(qkv8, cmp_kv, slopes, *, lk):
    g_count = N_HEADS // NSA_KV_HEADS
    nb = qkv8.shape[0]
    ncp = cmp_kv.shape[3]
    n_sel = -(-lk // SEL_BLOCK)
    nsp = -(-n_sel // LANE) * LANE
    ov = _overlap_matrix(ncp, n_sel, nsp)
    smem = pl.BlockSpec(memory_space=pltpu.SMEM)
    kv_block = (None, None, None, ncp, HEAD_DIM)
    return pl.pallas_call(
        functools.partial(_cmp_attn_kernel, tb=DEC_PAD, nc=ncp - 1, n_sel=n_sel, pos0=PAST_LEN, emit_idx=True),
        grid=(nb, NSA_KV_HEADS, 1),
        in_specs=[smem,
                  pl.BlockSpec((None, DEC_PAD, g_count * HEAD_DIM), lambda b, h, i: (b, 0, h)),
                  pl.BlockSpec(kv_block, lambda b, h, i: (b, 0, h, 0, 0)),
                  pl.BlockSpec(kv_block, lambda b, h, i: (b, 1, h, 0, 0)),
                  pl.BlockSpec((ncp, nsp), lambda b, h, i: (0, 0))],
        out_specs=[pl.BlockSpec((None, DEC_PAD, g_count * HEAD_DIM), lambda b, h, i: (b, 0, h)),
                   pl.BlockSpec((None, None, DEC_PAD, LANE), lambda b, h, i: (b, h, 0, 0))],
        out_shape=[jax.ShapeDtypeStruct((nb, DEC_PAD, N_HEADS * HEAD_DIM), F32),
                   jax.ShapeDtypeStruct((nb, NSA_KV_HEADS, DEC_PAD, LANE), jnp.int32)],
        compiler_params=_params(("arbitrary", "arbitrary", "arbitrary")),
        name="cmp_attention_sample",
    )(slopes, qkv8, cmp_kv, cmp_kv, ov)


def _sel_prompt_kernel(slopes_ref, q_ref, k_ref, v_ref, sel_ref, o_ref):
    kvh = pl.program_id(1)
    i = pl.program_id(2)
    g_count = N_HEADS // NSA_KV_HEADS
    tb = BAND_BLOCK
    rows = g_count * tb
    qs = _stack_heads(q_ref[...], g_count).astype(BF16)
    sel = sel_ref[...].astype(BF16)
    slope_col = jnp.concatenate(
        [jnp.full((tb, 1), slopes_ref[kvh * g_count + g], F32) for g in range(g_count)], axis=0)
    tq = i * tb + lax.broadcasted_iota(jnp.int32, (tb, tb), 0)
    kk = lax.broadcasted_iota(jnp.int32, (tb, tb), 1)
    jj = lax.broadcasted_iota(jnp.int32, (LANE, tb), 0)
    kblk = lax.broadcasted_iota(jnp.int32, (LANE, tb), 1) // SEL_BLOCK

    def body(kt, carry):
        m, l, acc = carry
        k0 = pl.multiple_of(kt * tb, tb)
        ks = k_ref[pl.ds(k0, tb), :].astype(BF16)
        vs = v_ref[pl.ds(k0, tb), :].astype(BF16)
        s = lax.dot_general(qs, ks, (((1,), (1,)), ((), ())), preferred_element_type=F32) * SCALE
        expand = (jj == kt * (tb // SEL_BLOCK) + kblk).astype(BF16)
        chosen = jnp.dot(sel, expand, preferred_element_type=F32)
        dist = (tq - (k0 + kk)).astype(F32)
        dist = jnp.where(chosen > 0.5, dist, -1.0)
        distf = jnp.concatenate([dist] * g_count, axis=0)
        valid = distf >= 0.0
        s = jnp.where(valid, s - slope_col * distf, NEG_FILL)
        m_new = jnp.maximum(m, jnp.max(s, axis=-1, keepdims=True))
        alpha = jnp.exp(m - m_new)
        p = jnp.where(valid, jnp.exp(s - m_new), 0.0)
        l = alpha * l + jnp.sum(p, axis=-1, keepdims=True)
        acc = alpha * acc + jnp.dot(p.astype(BF16), vs, preferred_element_type=F32)
        return m_new, l, acc

    init = (jnp.full((rows, 1), NEG_FILL, F32), jnp.zeros((rows, 1), F32), jnp.zeros((rows, HEAD_DIM), F32))
    _, l, acc = lax.fori_loop(0, i + 1, body, init)
    o = acc / jnp.maximum(l, 1e-30)
    o_ref[...] = jnp.concatenate([o[g * tb:(g + 1) * tb] for g in range(g_count)], axis=1)


def sel_attention_prompt(proj, sel_mask, slopes, *, batch, seq, k_col0, v_col0):
    g_count = N_HEADS // NSA_KV_HEADS
    tb = BAND_BLOCK
    nb = seq // tb
    kb0, vb0 = k_col0 // HEAD_DIM, v_col0 // HEAD_DIM
    smem = pl.BlockSpec(memory_space=pltpu.SMEM)
    return pl.pallas_call(
        _sel_prompt_kernel,
        grid=(batch, NSA_KV_HEADS, nb),
        in_specs=[smem,
                  pl.BlockSpec((tb, g_count * HEAD_DIM), lambda b, h, i: (b * nb + i, h)),
                  pl.BlockSpec((seq, HEAD_DIM), lambda b, h, i: (b, kb0 + h)),
                  pl.BlockSpec((seq, HEAD_DIM), lambda b, h, i: (b, vb0 + h)),
                  pl.BlockSpec((None, None, tb, LANE), lambda b, h, i: (b, h, i, 0))],
        out_specs=pl.BlockSpec((tb, g_count * HEAD_DIM), lambda b, h, i: (b * nb + i, h)),
        out_shape=jax.ShapeDtypeStruct((batch * seq, N_HEADS * HEAD_DIM), F32),
        compiler_params=_params(("arbitrary", "arbitrary", "arbitrary")),
        name="sel_attention_prompt",
    )(slopes, proj, proj, proj, sel_mask)


def _sel_sample_kernel(idx_ref, pt_ref, q_ref, *refs, n_past_blocks):
    kblocks = refs[:SEL_TOPN]
    vblocks = refs[SEL_TOPN:2 * SEL_TOPN]
    kn_ref, vn_ref, slope_ref, o_ref = refs[2 * SEL_TOPN:]
    b, kvh, t = pl.program_id(0), pl.program_id(1), pl.program_id(2)
    base = ((b * NSA_KV_HEADS + kvh) * DEC_SEQ + t) * SEL_TOPN
    g_count = N_HEADS // NSA_KV_HEADS
    pad = jnp.zeros((LANE - DEC_PAD, HEAD_DIM), F32)
    kall = jnp.concatenate([r[...] for r in kblocks] + [kn_ref[...], pad], axis=0).astype(BF16)
    vall = jnp.concatenate([r[...] for r in vblocks] + [vn_ref[...], pad], axis=0).astype(BF16)
    q = q_ref[...].astype(BF16)
    s = lax.dot_general(q, kall, (((1,), (1,)), ((), ())), preferred_element_type=F32) * SCALE
    lane = lax.broadcasted_iota(jnp.int32, (g_count, LANE), 1)
    low = lane < SEL_BLOCK
    qpos = PAST_LEN + t
    pos_parts, ok_parts = [], []
    for c in range(SEL_TOPN // 2):
        b0 = idx_ref[base + 2 * c]
        b1 = idx_ref[base + 2 * c + 1]
        p0 = jnp.where(b0 < n_past_blocks, b0 * SEL_BLOCK, PAST_LEN + DEC_SEQ)
        p1 = jnp.where(b1 < n_past_blocks, b1 * SEL_BLOCK, PAST_LEN + DEC_SEQ)
        pos_parts.append(jnp.where(low, p0 + lane, p1 + (lane - SEL_BLOCK)))
    pos_parts.append(PAST_LEN + lane)
    dist = qpos - jnp.concatenate(pos_parts, axis=1)
    valid = dist >= 0
    slope = jnp.concatenate([slope_ref[...]] * (SEL_TOPN // 2 + 1), axis=1)
    o_ref[...] = _softmax_pv(s - slope * dist.astype(F32), valid, vall)


def sel_attention_sample(q_s, pool_halves, page_table, sel_idx, new_kv8, slope_rows, layer_half0):
    nb = q_s.shape[0]
    g_count = N_HEADS // NSA_KV_HEADS
    n_past_blocks = PAST_LEN // SEL_BLOCK
    per_page = PAGE_SIZE // SEL_BLOCK

    def blk_spec(r, slot):
        def index(b, h, t, idx, pt):
            blk = jnp.minimum(idx[((b * NSA_KV_HEADS + h) * DEC_SEQ + t) * SEL_TOPN + r], n_past_blocks - 1)
            return (layer_half0 + pt[b, blk // per_page] * per_page + blk % per_page, 0, slot * NSA_KV_HEADS + h)
        return pl.BlockSpec((None, SEL_BLOCK, HEAD_DIM), index)

    q_block = (None, None, None, g_count, HEAD_DIM)
    new_block = (None, None, None, DEC_PAD, HEAD_DIM)
    grid_spec = pltpu.PrefetchScalarGridSpec(
        num_scalar_prefetch=2,
        grid=(nb, NSA_KV_HEADS, DEC_SEQ),
        in_specs=[pl.BlockSpec(q_block, lambda b, h, t, idx, pt: (b, t, h, 0, 0))]
        + [blk_spec(r, 2) for r in range(SEL_TOPN)] + [blk_spec(r, 3) for r in range(SEL_TOPN)]
        + [pl.BlockSpec(new_block, lambda b, h, t, idx, pt: (b, 2, h, 0, 0)),
           pl.BlockSpec(new_block, lambda b, h, t, idx, pt: (b, 3, h, 0, 0)),
           pl.BlockSpec((None, g_count, LANE), lambda b, h, t, idx, pt: (h, 0, 0))],
        out_specs=pl.BlockSpec(q_block, lambda b, h, t, idx, pt: (b, t, h, 0, 0)),
    )
    return pl.pallas_call(
        functools.partial(_sel_sample_kernel, n_past_blocks=n_past_blocks),
        grid_spec=grid_spec,
        out_shape=jax.ShapeDtypeStruct(q_s.shape, F32),
        compiler_params=_params(("arbitrary", "arbitrary", "arbitrary")),
        name="sel_attention_sample",
    )(sel_idx, page_table, q_s, *([pool_halves] * (2 * SEL_TOPN)), new_kv8, new_kv8, slope_rows)


def _combine_kernel(oc_ref, os_ref, ow_ref, gl_ref, gb_ref, o_ref):
    gate = jax.nn.sigmoid(gl_ref[...] + gb_ref[...])
    for h in range(N_HEADS):
        sl = slice(h * HEAD_DIM, (h + 1) * HEAD_DIM)
        o = (gate[:, h:h + 1] * oc_ref[:, sl] + gate[:, N_HEADS + h:N_HEADS + h + 1] * os_ref[:, sl]
             + gate[:, 2 * N_HEADS + h:2 * N_HEADS + h + 1] * ow_ref[:, sl])
        o_ref[:, sl] = o.astype(o_ref.dtype)


def gate_combine(o_c, o_s, o_w, gate_logits, gate_bias, tm, row_block0=0):
    m, n = o_c.shape
    big = pl.BlockSpec((tm, n), lambda i: (i, 0))
    return pl.pallas_call(
        _combine_kernel,
        grid=(m // tm,),
        in_specs=[big, big, big,
                  pl.BlockSpec((tm, LANE), lambda i: (row_block0 + i, 0)),
                  pl.BlockSpec((1, LANE), lambda i: (0, 0))],
        out_specs=big,
        out_shape=jax.ShapeDtypeStruct((m, n), BF16),
        compiler_params=_params(("arbitrary",)),
        name="gate_combine",
    )(o_c, o_s, o_w, gate_logits, gate_bias)


TM_DENSE = 1040
TN_PROJ = 512
TF_FFN = 256
TM_FFN_OUT = 640
TN_FFN_OUT = 256
TM_NORM = 320
TM_PROMPT = 256
NQ = N_HEADS * HEAD_DIM


def _ffn_block(x, g, w_in, w_out_bf, layer, sub):
    xn = rms_norm(x, g, BF16, TM_NORM)
    h = ffn_in(xn, w_in, layer, sub, TM_DENSE, TF_FFN)
    return ffn_out(h, w_out_bf, layer, sub, x, TM_FFN_OUT, TN_FFN_OUT)


def _sample_rows8(proj):
    s = proj[M_PROMPT:].reshape(DEC_BATCH, DEC_SEQ, proj.shape[1])
    return jnp.pad(s, ((0, 0), (0, DEC_PAD - DEC_SEQ), (0, 0)))


def _swa_layer(x, g, w_in, w_out, sinks_all, cache_all, li, slopes):
    xn = rms_norm(x, g, BF16, TM_NORM)
    kv_cols = 2 * SWA_KV_HEADS * HEAD_DIM
    proj = matmul_wres(xn, w_in, (li,), NQ + kv_cols, TM_DENSE, TN_PROJ)
    sinks = sinks_all[li].astype(F32)
    o_p = band_attention(proj, slopes, sinks, batch=BATCH, seq=SEQ, kv_heads=SWA_KV_HEADS, k_col0=NQ,
                         v_col0=NQ + SWA_KV_HEADS * HEAD_DIM, window=SWA_WINDOW, use_sink=True, out_dtype=BF16)
    cache = cache_all[li]
    lc = cache.shape[1]
    o_s = decode_attention(_sample_rows8(proj), cache.reshape(DEC_BATCH, lc, kv_cols), slopes, sinks,
                           kv_heads=SWA_KV_HEADS, k_col0=NQ, v_col0=NQ + SWA_KV_HEADS * HEAD_DIM,
                           window=SWA_WINDOW, use_sink=True, out_dtype=BF16)
    attn = jnp.concatenate([o_p, o_s[:, :DEC_SEQ].reshape(M_SAMPLE, NQ)], axis=0)
    x = matmul_wres(attn, w_out, (li,), D_MODEL, TM_DENSE, TN_PROJ, res=x)
    kv_p = proj[:M_PROMPT, NQ:].reshape(BATCH, SEQ, 2, SWA_KV_HEADS, HEAD_DIM)
    kv_s = proj[M_PROMPT:, NQ:].reshape(DEC_BATCH, DEC_SEQ, 2, SWA_KV_HEADS, HEAD_DIM)
    buf_p = kv_p[:, SEQ - min(SWA_WINDOW, SEQ):]
    buf_s = jnp.concatenate([cache, kv_s], axis=1)[:, DEC_SEQ:]
    return x, buf_p, buf_s


def _nsa_layer(x, g, w_in, gate_b, cmp_pe, cmp_w1, cmp_w2, w_out, win_cache_all, pool_all, page_table, li, slopes):
    xn = rms_norm(x, g, BF16, TM_NORM)
    kvw = NSA_KV_HEADS * HEAD_DIM
    main_cols = NQ + 6 * kvw
    proj = matmul_wres(xn, w_in, (li,), main_cols, TM_DENSE, TN_PROJ)
    n_gate = 3 * N_HEADS
    w_gate = jnp.pad(w_in[li][:, main_cols:], ((0, 0), (0, LANE - n_gate)))
    gate_logits = matmul_wres(xn, w_gate, (), LANE, TM_DENSE, LANE)
    gate_bias = jnp.pad(gate_b[li], (0, LANE - n_gate)).reshape(1, LANE)
    zeros = jnp.zeros((N_HEADS,), F32)

    pet = pe_term(cmp_pe, cmp_w1, li)
    w1 = cmp_w1[li].astype(BF16)
    w1_slots = jnp.concatenate(
        [jnp.concatenate([w1[s, :CMP_STRIDE], w1[s, CMP_STRIDE:]], axis=-1) for s in range(2)], axis=-2)
    w2_slots = jnp.concatenate([cmp_w2[li, 0], cmp_w2[li, 1]], axis=0).astype(BF16)
    pet_slots = jnp.concatenate([pet[0, :NSA_KV_HEADS], pet[1, :NSA_KV_HEADS]], axis=0)

    o_w_p = band_attention(proj, slopes, zeros, batch=BATCH, seq=SEQ, kv_heads=NSA_KV_HEADS, k_col0=NQ + 4 * kvw,
                           v_col0=NQ + 5 * kvw, window=NSA_WINDOW, use_sink=False, out_dtype=F32)
    cmp_p = compress_prompt(proj, cmp_w1, cmp_w2, pet, li, batch=BATCH, seq=SEQ, col0=NQ)
    o_c_p, sel_mask = cmp_attention_prompt(proj, cmp_p, slopes, batch=BATCH, seq=SEQ)
    o_s_p = sel_attention_prompt(proj, sel_mask, slopes, batch=BATCH, seq=SEQ, k_col0=NQ + 2 * kvw,
                                 v_col0=NQ + 3 * kvw)
    attn_p = gate_combine(o_c_p, o_s_p, o_w_p, gate_logits, gate_bias, TM_PROMPT)

    qkv8 = _sample_rows8(proj)
    win_cache = win_cache_all[li]
    lc = win_cache.shape[1]
    o_w_s = decode_attention(qkv8, win_cache.reshape(DEC_BATCH, lc, 2 * kvw), slopes, zeros,
                             kv_heads=NSA_KV_HEADS, k_col0=NQ + 4 * kvw, v_col0=NQ + 5 * kvw, window=NSA_WINDOW,
                             use_sink=False, out_dtype=F32)
    n_pool = pool_all.shape[1]
    pool = pool_all.reshape(pool_all.shape[0] * n_pool, PAGE_SIZE, 4 * kvw)
    pool4 = pool.reshape(pool.shape[0], PAGE_SIZE, 2 * CMP_HEADS, HEAD_DIM)
    part_a, part_b = compress_pages_partial(pool4, page_table, w1_slots, li * n_pool)
    cmp_s = compress_pages_finish(part_a, part_b, w2_slots, pet_slots)
    o_c_s, sel_idx = cmp_attention_sample(qkv8, cmp_s, slopes, lk=PAST_LEN + DEC_SEQ)
    sel_flat = sel_idx[:, :, :DEC_SEQ, :SEL_TOPN].reshape(-1)
    rows_s = proj[M_PROMPT:, NQ:NQ + 4 * kvw].reshape(DEC_BATCH, DEC_SEQ, 4, NSA_KV_HEADS, HEAD_DIM)
    new_kv8 = jnp.pad(rows_s.transpose(0, 2, 3, 1, 4), ((0, 0), (0, 0), (0, 0), (0, DEC_PAD - DEC_SEQ), (0, 0)))
    g_count = N_HEADS // NSA_KV_HEADS
    q_s = proj[M_PROMPT:, :NQ].reshape(DEC_BATCH, DEC_SEQ, NSA_KV_HEADS, g_count, HEAD_DIM)
    slope_rows = jnp.broadcast_to(slopes.reshape(NSA_KV_HEADS, g_count, 1), (NSA_KV_HEADS, g_count, LANE))
    per_page = PAGE_SIZE // SEL_BLOCK
    pool_halves = pool.reshape(pool.shape[0] * per_page, SEL_BLOCK, 4 * kvw)
    o_s_s = sel_attention_sample(q_s, pool_halves, page_table, sel_flat, new_kv8, slope_rows, li * n_pool * per_page)
    attn_s = gate_combine(o_c_s[:, :DEC_SEQ].reshape(M_SAMPLE, NQ), o_s_s.reshape(M_SAMPLE, NQ),
                          o_w_s[:, :DEC_SEQ].reshape(M_SAMPLE, NQ), gate_logits, gate_bias, M_SAMPLE,
                          row_block0=M_PROMPT // M_SAMPLE)

    attn = jnp.concatenate([attn_p, attn_s], axis=0)
    x = matmul_wres(attn, w_out, (li,), D_MODEL, TM_DENSE, TN_PROJ, res=x)

    kv_p = proj[:M_PROMPT, NQ:main_cols].reshape(BATCH, SEQ, 6, NSA_KV_HEADS, HEAD_DIM)
    kv_s = proj[M_PROMPT:, NQ:main_cols].reshape(DEC_BATCH, DEC_SEQ, 6, NSA_KV_HEADS, HEAD_DIM)
    win_p = kv_p[:, SEQ - min(NSA_WINDOW, SEQ):, 4:]
    win_s = jnp.concatenate([win_cache, kv_s[:, :, 4:]], axis=1)[:, DEC_SEQ:]
    return x, kv_p[:, :, :4], kv_s[:, :, :4], win_p, win_s


def kernel(x_prompt, x_sample, cache_swa_kv, cache_nsa_win_kv, cache_nsa_kv, page_table, norm_g, final_norm_g,
           ffn_w_in, ffn_w_out, swa_w_in, swa_w_out, swa_sinks, nsa_w_in, nsa_gate_b, nsa_cmp_pe, nsa_cmp_w1,
           nsa_cmp_w2, nsa_w_out):
    x = jnp.concatenate([x_prompt.reshape(M_PROMPT, D_MODEL), x_sample.reshape(M_SAMPLE, D_MODEL)], axis=0)
    slopes = jnp.exp2(-8.0 * jnp.arange(1, N_HEADS + 1, dtype=F32) / N_HEADS)
    w_out_bf = ffn_w_out.astype(BF16)
    swa_p, swa_s, win_p, win_s, kv_p, kv_s = [], [], [], [], [], []
    for i in range(DEPTH):
        x = _ffn_block(x, norm_g[i, 0], ffn_w_in, w_out_bf, i, 0)
        li = i // N_MIXERS
        if i % N_MIXERS == 0:
            x, bp, bs = _swa_layer(x, norm_g[i, 1], swa_w_in, swa_w_out, swa_sinks, cache_swa_kv, li, slopes)
            swa_p.append(bp)
            swa_s.append(bs)
        else:
            x, rp, rs, wp, ws = _nsa_layer(x, norm_g[i, 1], nsa_w_in, nsa_gate_b, nsa_cmp_pe, nsa_cmp_w1,
                                           nsa_cmp_w2, nsa_w_out, cache_nsa_win_kv, cache_nsa_kv, page_table, li,
                                           slopes)
            kv_p.append(rp)
            kv_s.append(rs)
            win_p.append(wp)
            win_s.append(ws)
        x = _ffn_block(x, norm_g[i, 2], ffn_w_in, w_out_bf, i, 1)
    y_p = rms_norm(x, final_norm_g, F32, TM_PROMPT, rows=M_PROMPT).reshape(BATCH, SEQ, D_MODEL)
    y_s = rms_norm(x, final_norm_g, F32, M_SAMPLE, row_block0=M_PROMPT // M_SAMPLE, rows=M_SAMPLE)
    return (y_p, y_s.reshape(DEC_BATCH, DEC_SEQ, D_MODEL), jnp.stack(swa_p), jnp.stack(swa_s), jnp.stack(win_p),
            jnp.stack(win_s), jnp.stack(kv_p), jnp.stack(kv_s))
```

```python
import functools
import math

import jax
import jax.numpy as jnp
from jax import lax
from jax.experimental import pallas as pl
from jax.experimental.pallas import tpu as pltpu

D_MODEL = 4096
BATCH = 4
SEQ = 2048
DEPTH = 2
DEC_BATCH = 32
DEC_SEQ = 4
PAST_LEN = 16384
PAGE_SIZE = 128
N_HEADS = 32
HEAD_DIM = 128
SWA_KV_HEADS = 8
SWA_WINDOW = 128
NSA_KV_HEADS = 4
CMP_STRIDE = 16
CMP_LEN = 32
SEL_BLOCK = 64
SEL_TOPN = 16
NSA_WINDOW = 512
D_FF = 11008
BAND_BLOCK = 128
N_MIXERS = 2
RMS_EPS = 1e-6
NEG_FILL = -1e30
SCALE = HEAD_DIM ** -0.5

M_PROMPT = BATCH * SEQ
M_SAMPLE = DEC_BATCH * DEC_SEQ
M_ALL = M_PROMPT + M_SAMPLE
DEC_PAD = 8
LANE = 128
F32 = jnp.float32
BF16 = jnp.bfloat16
VMEM_LIMIT = 60 * 1024 * 1024


def _params(sem):
    return pltpu.CompilerParams(dimension_semantics=sem, vmem_limit_bytes=VMEM_LIMIT)


def _rms_kernel(x_ref, g_ref, o_ref):
    x = x_ref[...]
    y = x * lax.rsqrt(jnp.mean(x * x, axis=-1, keepdims=True) + RMS_EPS)
    o_ref[...] = (y * g_ref[...]).astype(o_ref.dtype)


def rms_norm(x, g, out_dtype, tm, row_block0=0, rows=None):
    m, d = x.shape
    rows = m if rows is None else rows
    return pl.pallas_call(
        _rms_kernel,
        grid=(rows // tm,),
        in_specs=[pl.BlockSpec((tm, d), lambda i: (row_block0 + i, 0)),
                  pl.BlockSpec((1, d), lambda i: (0, 0))],
        out_specs=pl.BlockSpec((tm, d), lambda i: (i, 0)),
        out_shape=jax.ShapeDtypeStruct((rows, d), out_dtype),
        compiler_params=_params(("arbitrary",)),
        name="rms_norm",
    )(x, g.reshape(1, d))


def _mm_kernel(x_ref, w_ref, *rest, scale, has_res):
    if has_res:
        res_ref, o_ref, wbf_ref = rest
    else:
        o_ref, wbf_ref = rest

    @pl.when(pl.program_id(1) == 0)
    def _():
        wbf_ref[...] = w_ref[...].astype(BF16)

    acc = jnp.dot(x_ref[...], wbf_ref[...], preferred_element_type=F32)
    if has_res:
        o_ref[...] = res_ref[...] + (acc if scale == 1.0 else scale * acc)
    else:
        o_ref[...] = acc.astype(o_ref.dtype)


def matmul_wres(x, w, w_prefix, n_cols, tm, tn, res=None, scale=1.0, col_block0=0):
    m, k = x.shape
    npre = len(w_prefix)
    w_block = (None,) * npre + (k, tn)
    in_specs = [pl.BlockSpec((tm, k), lambda j, i: (i, 0)),
                pl.BlockSpec(w_block, lambda j, i: tuple(w_prefix) + (0, col_block0 + j))]
    args = [x, w]
    if res is not None:
        in_specs.append(pl.BlockSpec((tm, tn), lambda j, i: (i, j)))
        args.append(res)
    return pl.pallas_call(
        functools.partial(_mm_kernel, scale=scale, has_res=res is not None),
        grid=(n_cols // tn, m // tm),
        in_specs=in_specs,
        out_specs=pl.BlockSpec((tm, tn), lambda j, i: (i, j)),
        out_shape=jax.ShapeDtypeStruct((m, n_cols), F32),
        scratch_shapes=[pltpu.VMEM((k, tn), BF16)],
        compiler_params=_params(("arbitrary", "arbitrary")),
        name="matmul_wres",
    )(*args)


def _ffn_in_kernel(x_ref, wg_ref, wu_ref, o_ref, wg_bf, wu_bf):
    @pl.when(pl.program_id(1) == 0)
    def _():
        wg_bf[...] = wg_ref[...].astype(BF16)
        wu_bf[...] = wu_ref[...].astype(BF16)

    x = x_ref[...]
    g = jnp.dot(x, wg_bf[...], preferred_element_type=F32)
    u = jnp.dot(x, wu_bf[...], preferred_element_type=F32)
    o_ref[...] = (jax.nn.silu(g) * u).astype(o_ref.dtype)


def ffn_in(xn, w_in, layer, sub, tm, tf):
    m, k = xn.shape
    nf = D_FF // tf
    w_block = (None, None, k, tf)
    return pl.pallas_call(
        _ffn_in_kernel,
        grid=(nf, m // tm),
        in_specs=[pl.BlockSpec((tm, k), lambda j, i: (i, 0)),
                  pl.BlockSpec(w_block, lambda j, i: (layer, sub, 0, j)),
                  pl.BlockSpec(w_block, lambda j, i: (layer, sub, 0, nf + j))],
        out_specs=pl.BlockSpec((tm, tf), lambda j, i: (i, j)),
        out_shape=jax.ShapeDtypeStruct((m, D_FF), BF16),
        scratch_shapes=[pltpu.VMEM((k, tf), BF16), pltpu.VMEM((k, tf), BF16)],
        compiler_params=_params(("arbitrary", "arbitrary")),
        name="ffn_in",
    )(xn, w_in, w_in)


def _ffn_out_kernel(h_ref, w_ref, res_ref, o_ref):
    acc = jnp.dot(h_ref[...], w_ref[...], preferred_element_type=F32)
    o_ref[...] = res_ref[...] + 0.5 * acc


def ffn_out(h, w_out_bf, layer, sub, res, tm, tn):
    m, k = h.shape
    n = res.shape[1]
    return pl.pallas_call(
        _ffn_out_kernel,
        grid=(m // tm, n // tn),
        in_specs=[pl.BlockSpec((tm, k), lambda i, j: (i, 0)),
                  pl.BlockSpec((None, None, k, tn), lambda i, j: (layer, sub, 0, j)),
                  pl.BlockSpec((tm, tn), lambda i, j: (i, j))],
        out_specs=pl.BlockSpec((tm, tn), lambda i, j: (i, j)),
        out_shape=jax.ShapeDtypeStruct((m, n), F32),
        compiler_params=_params(("arbitrary", "arbitrary")),
        name="ffn_out",
    )(h, w_out_bf, res)


def _stack_heads(q, g_count):
    return jnp.concatenate([q[:, g * HEAD_DIM:(g + 1) * HEAD_DIM] for g in range(g_count)], axis=0)


def _softmax_pv(s, valid, v_bf, sink=None):
    s = jnp.where(valid, s, 2 * NEG_FILL)
    m = jnp.maximum(jnp.max(s, axis=-1, keepdims=True), NEG_FILL)
    if sink is not None:
        m = jnp.maximum(m, sink)
    p = jnp.exp(s - m)
    den = jnp.sum(p, axis=-1, keepdims=True)
    if sink is not None:
        den = den + jnp.exp(sink - m)
    o = jnp.dot(p.astype(BF16), v_bf, preferred_element_type=F32)
    return o / jnp.maximum(den, 1e-30)


def _band_kernel(slopes_ref, sinks_ref, q_ref, k_ref, v_ref, o_ref, *, g_count, nprev, window, use_sink):
    kvh = pl.program_id(1)
    i = pl.program_id(2)
    width = (nprev + 1) * BAND_BLOCK
    start = pl.multiple_of(jnp.maximum(i - nprev, 0) * BAND_BLOCK, BAND_BLOCK)
    k = k_ref[pl.ds(start, width), :].astype(BF16)
    v = v_ref[pl.ds(start, width), :].astype(BF16)
    qs = _stack_heads(q_ref[...], g_count).astype(BF16)
    s = lax.dot_general(qs, k, (((1,), (1,)), ((), ())), preferred_element_type=F32) * SCALE
    tq = i * BAND_BLOCK + lax.broadcasted_iota(jnp.int32, (BAND_BLOCK, width), 0)
    kp = start + lax.broadcasted_iota(jnp.int32, (BAND_BLOCK, width), 1)
    dist = tq - kp
    valid = (dist >= 0) & (dist <= window)
    distf = dist.astype(F32)
    outs = []
    for g in range(g_count):
        h = kvh * g_count + g
        sg = s[g * BAND_BLOCK:(g + 1) * BAND_BLOCK] - slopes_ref[h] * distf
        outs.append(_softmax_pv(sg, valid, v, sinks_ref[h] if use_sink else None))
    o_ref[...] = jnp.concatenate(outs, axis=1).astype(o_ref.dtype)


def band_attention(proj, slopes, sinks, *, batch, seq, kv_heads, k_col0, v_col0, window, use_sink, out_dtype):
    g_count = N_HEADS // kv_heads
    nb = seq // BAND_BLOCK
    nprev = -(-window // BAND_BLOCK)
    kb0, vb0 = k_col0 // HEAD_DIM, v_col0 // HEAD_DIM
    smem = pl.BlockSpec(memory_space=pltpu.SMEM)
    return pl.pallas_call(
        functools.partial(_band_kernel, g_count=g_count, nprev=nprev, window=window, use_sink=use_sink),
        grid=(batch, kv_heads, nb),
        in_specs=[smem, smem,
                  pl.BlockSpec((BAND_BLOCK, g_count * HEAD_DIM), lambda b, h, i: (b * nb + i, h)),
                  pl.BlockSpec((seq, HEAD_DIM), lambda b, h, i: (b, kb0 + h)),
                  pl.BlockSpec((seq, HEAD_DIM), lambda b, h, i: (b, vb0 + h))],
        out_specs=pl.BlockSpec((BAND_BLOCK, g_count * HEAD_DIM), lambda b, h, i: (b * nb + i, h)),
        out_shape=jax.ShapeDtypeStruct((batch * seq, N_HEADS * HEAD_DIM), out_dtype),
        compiler_params=_params(("arbitrary", "arbitrary", "arbitrary")),
        name="band_attention",
    )(slopes, sinks, proj, proj, proj)


def _dec_kernel(slopes_ref, sinks_ref, qkv_ref, c_ref, o_ref, *, kv_heads, g_count, lc, k_col0, v_col0,
                window, use_sink):
    rows = g_count * DEC_PAD
    width = lc + LANE
    t = lax.broadcasted_iota(jnp.int32, (DEC_PAD, width), 0)
    col = lax.broadcasted_iota(jnp.int32, (DEC_PAD, width), 1)
    dist = jnp.where(col < lc, t + (lc - col), t - (col - lc))
    valid = (dist >= 0) & (dist <= window) & (col < lc + DEC_SEQ)
    distf = dist.astype(F32)
    pad = jnp.zeros((LANE - DEC_PAD, HEAD_DIM), F32)
    for kvh in range(kv_heads):
        q0 = kvh * g_count * HEAD_DIM
        qs = _stack_heads(qkv_ref[:, q0:q0 + g_count * HEAD_DIM], g_count).astype(BF16)
        kn = qkv_ref[:, k_col0 + kvh * HEAD_DIM:k_col0 + (kvh + 1) * HEAD_DIM]
        vn = qkv_ref[:, v_col0 + kvh * HEAD_DIM:v_col0 + (kvh + 1) * HEAD_DIM]
        kc = c_ref[:, kvh * HEAD_DIM:(kvh + 1) * HEAD_DIM]
        vc = c_ref[:, (kv_heads + kvh) * HEAD_DIM:(kv_heads + kvh + 1) * HEAD_DIM]
        kall = jnp.concatenate([kc, kn, pad], axis=0).astype(BF16)
        vall = jnp.concatenate([vc, vn, pad], axis=0).astype(BF16)
        s = lax.dot_general(qs, kall, (((1,), (1,)), ((), ())), preferred_element_type=F32) * SCALE
        assert s.shape == (rows, width)
        for g in range(g_count):
            h = kvh * g_count + g
            sg = s[g * DEC_PAD:(g + 1) * DEC_PAD] - slopes_ref[h] * distf
            o = _softmax_pv(sg, valid, vall, sinks_ref[h] if use_sink else None)
            o_ref[:, h * HEAD_DIM:(h + 1) * HEAD_DIM] = o.astype(o_ref.dtype)


def decode_attention(qkv8, cache, slopes, sinks, *, kv_heads, k_col0, v_col0, window, use_sink, out_dtype):
    nb, _, ncols = qkv8.shape
    lc = cache.shape[1]
    g_count = N_HEADS // kv_heads
    smem = pl.BlockSpec(memory_space=pltpu.SMEM)
    return pl.pallas_call(
        functools.partial(_dec_kernel, kv_heads=kv_heads, g_count=g_count, lc=lc, k_col0=k_col0, v_col0=v_col0,
                          window=window, use_sink=use_sink),
        grid=(nb,),
        in_specs=[smem, smem,
                  pl.BlockSpec((None, DEC_PAD, ncols), lambda b: (b, 0, 0)),
                  pl.BlockSpec((None, lc, cache.shape[2]), lambda b: (b, 0, 0))],
        out_specs=pl.BlockSpec((None, DEC_PAD, N_HEADS * HEAD_DIM), lambda b: (b, 0, 0)),
        out_shape=jax.ShapeDtypeStruct((nb, DEC_PAD, N_HEADS * HEAD_DIM), out_dtype),
        compiler_params=_params(("arbitrary",)),
        name="decode_attention",
    )(slopes, sinks, qkv8, cache)


def _pe_term_kernel(pe_ref, w1_ref, o_ref):
    acc = jnp.zeros((8, HEAD_DIM), F32)
    for l in range(CMP_LEN):
        row = jnp.broadcast_to(pe_ref[l:l + 1, :], (8, HEAD_DIM)).astype(BF16)
        acc = acc + jnp.dot(row, w1_ref[l].astype(BF16), preferred_element_type=F32)
    o_ref[...] = acc


def pe_term(pe, w1, layer):
    return pl.pallas_call(
        _pe_term_kernel,
        grid=(2,),
        in_specs=[pl.BlockSpec((None, None, CMP_LEN, HEAD_DIM), lambda s: (layer, s, 0, 0)),
                  pl.BlockSpec((None, None, CMP_LEN, HEAD_DIM, HEAD_DIM), lambda s: (layer, s, 0, 0, 0))],
        out_specs=pl.BlockSpec((None, 8, HEAD_DIM), lambda s: (s, 0, 0)),
        out_shape=jax.ShapeDtypeStruct((2, 8, HEAD_DIM), F32),
        compiler_params=_params(("arbitrary",)),
        name="pe_term",
    )(pe, w1)


def _compress_finish(acc_a, acc_b, pet_row, w2_bf):
    n = acc_a.shape[0]
    pre = acc_a + pltpu.roll(acc_b, n - 1, 0) + pet_row
    return jnp.dot(jax.nn.gelu(pre).astype(BF16), w2_bf, preferred_element_type=F32)


def _cmp_prompt_kernel(x_ref, w1_ref, w2_ref, pet_ref, o_ref, *, nch):
    acc_a = jnp.zeros((nch, HEAD_DIM), F32)
    acc_b = jnp.zeros((nch, HEAD_DIM), F32)
    for l in range(CMP_STRIDE):
        xl = x_ref[pl.ds(l, nch, stride=CMP_STRIDE), :].astype(BF16)
        acc_a = acc_a + jnp.dot(xl, w1_ref[l].astype(BF16), preferred_element_type=F32)
        acc_b = acc_b + jnp.dot(xl, w1_ref[CMP_STRIDE + l].astype(BF16), preferred_element_type=F32)
    o_ref[...] = _compress_finish(acc_a, acc_b, pet_ref[0:1, :], w2_ref[...].astype(BF16))


def compress_prompt(proj, w1, w2, pet, layer, *, batch, seq, col0):
    nch = seq // CMP_STRIDE
    cb0 = col0 // HEAD_DIM
    return pl.pallas_call(
        functools.partial(_cmp_prompt_kernel, nch=nch),
        grid=(batch, 2, NSA_KV_HEADS),
        in_specs=[pl.BlockSpec((seq, HEAD_DIM), lambda b, s, h: (b, cb0 + s * NSA_KV_HEADS + h)),
                  pl.BlockSpec((None, None, CMP_LEN, HEAD_DIM, HEAD_DIM), lambda b, s, h: (layer, s, 0, 0, 0)),
                  pl.BlockSpec((None, None, HEAD_DIM, HEAD_DIM), lambda b, s, h: (layer, s, 0, 0)),
                  pl.BlockSpec((None, 8, HEAD_DIM), lambda b, s, h: (s, 0, 0))],
        out_specs=pl.BlockSpec((None, None, None, nch, HEAD_DIM), lambda b, s, h: (b, s, h, 0, 0)),
        out_shape=jax.ShapeDtypeStruct((batch, 2, NSA_KV_HEADS, nch, HEAD_DIM), F32),
        compiler_params=_params(("arbitrary", "arbitrary", "arbitrary")),
        name="compress_prompt",
    )(proj, w1, w2, pet)


CMP_PAGES = 16
CHUNKS_PER_PAGE = PAGE_SIZE // CMP_STRIDE
CMP_HEADS = 2 * NSA_KV_HEADS
CMP_ROWS = CMP_PAGES * CHUNKS_PER_PAGE * CMP_HEADS


def _slot_split(x, slot1):
    return jnp.concatenate([jnp.where(slot1, 0.0, x), jnp.where(slot1, x, 0.0)], axis=1).astype(BF16)


def _slot1_rows(rows):
    return (lax.broadcasted_iota(jnp.int32, (rows, HEAD_DIM), 0) % CMP_HEADS) >= NSA_KV_HEADS


def _cmp_pages_kernel(pt_ref, *refs):
    pages = refs[:CMP_PAGES]
    w_ref, a_ref, b_ref = refs[CMP_PAGES:]
    slot1 = _slot1_rows(CMP_ROWS)
    acc = jnp.zeros((CMP_ROWS, 2 * HEAD_DIM), F32)
    for l in range(CMP_STRIDE):
        x = jnp.concatenate([pages[r][l + CMP_STRIDE * c] for r in range(CMP_PAGES) for c in range(CHUNKS_PER_PAGE)],
                            axis=0)
        acc = acc + jnp.dot(_slot_split(x, slot1), w_ref[l], preferred_element_type=F32)
    a_ref[...] = acc[:, :HEAD_DIM]
    b_ref[...] = acc[:, HEAD_DIM:]


def compress_pages_partial(pool4, page_table, w1_slots, layer_page0):
    nb, n_pages = page_table.shape
    n_groups = n_pages // CMP_PAGES

    def page_spec(r):
        return pl.BlockSpec((None, PAGE_SIZE, CMP_HEADS, HEAD_DIM),
                            lambda b, j, pt: (layer_page0 + pt[b, j * CMP_PAGES + r], 0, 0, 0))

    out_spec = pl.BlockSpec((None, CMP_ROWS, HEAD_DIM), lambda b, j, pt: (b, j, 0))
    out_shape = jax.ShapeDtypeStruct((nb, n_groups * CMP_ROWS, HEAD_DIM), F32)
    grid_spec = pltpu.PrefetchScalarGridSpec(
        num_scalar_prefetch=1,
        grid=(nb, n_groups),
        in_specs=[page_spec(r) for r in range(CMP_PAGES)] + [pl.BlockSpec(w1_slots.shape, lambda b, j, pt: (0, 0, 0))],
        out_specs=[out_spec, out_spec],
    )
    return pl.pallas_call(
        _cmp_pages_kernel,
        grid_spec=grid_spec,
        out_shape=[out_shape, out_shape],
        compiler_params=_params(("arbitrary", "arbitrary")),
        name="compress_pages_partial",
    )(page_table, *([pool4] * CMP_PAGES), w1_slots)


def _cmp_finish_kernel(a_ref, b_ref, bnext_ref, w2_ref, pet_ref, o_ref, scr_ref):
    rows = a_ref.shape[0]
    last_part = pl.program_id(1) == pl.num_programs(1) - 1
    b_shift = jnp.concatenate([b_ref[CMP_HEADS:, :], bnext_ref[:CMP_HEADS, :]], axis=0)
    pet = jnp.concatenate([pet_ref[...]] * (rows // CMP_HEADS), axis=0)
    h = jax.nn.gelu(a_ref[...] + b_shift + pet)
    res = jnp.dot(_slot_split(h, _slot1_rows(rows)), w2_ref[...], preferred_element_type=F32)
    row = lax.broadcasted_iota(jnp.int32, (rows, HEAD_DIM), 0)
    scr_ref[...] = jnp.where(row >= jnp.where(last_part, rows - CMP_HEADS, rows), 0.0, res)
    for sk in range(CMP_HEADS):
        o_ref[sk] = scr_ref[pl.ds(sk, rows // CMP_HEADS, stride=CMP_HEADS), :]


def compress_pages_finish(a, b, w2_slots, pet_slots):
    nb, total_rows, _ = a.shape
    n_parts = total_rows // CMP_ROWS
    per_part = CMP_ROWS // CMP_HEADS
    blk = (None, CMP_ROWS, HEAD_DIM)
    out = pl.pallas_call(
        _cmp_finish_kernel,
        grid=(nb, n_parts),
        in_specs=[pl.BlockSpec(blk, lambda b_, p: (b_, p, 0)),
                  pl.BlockSpec(blk, lambda b_, p: (b_, p, 0)),
                  pl.BlockSpec(blk, lambda b_, p: (b_, jnp.minimum(p + 1, n_parts - 1), 0)),
                  pl.BlockSpec(w2_slots.shape, lambda b_, p: (0, 0)),
                  pl.BlockSpec(pet_slots.shape, lambda b_, p: (0, 0))],
        out_specs=pl.BlockSpec((None, CMP_HEADS, per_part, HEAD_DIM), lambda b_, p: (b_, 0, p, 0)),
        out_shape=jax.ShapeDtypeStruct((nb, CMP_HEADS, n_parts * per_part, HEAD_DIM), F32),
        scratch_shapes=[pltpu.VMEM((CMP_ROWS, HEAD_DIM), F32)],
        compiler_params=_params(("arbitrary", "arbitrary")),
        name="compress_pages_finish",
    )(a, b, b, w2_slots, pet_slots)
    return out.reshape(nb, 2, NSA_KV_HEADS, n_parts * per_part, HEAD_DIM)


def _split3(x):
    hi = x.astype(BF16)
    r1 = x - hi.astype(F32)
    mid = r1.astype(BF16)
    lo = (r1 - mid.astype(F32)).astype(BF16)
    return hi, mid, lo


def _cmp_attn_kernel(slopes_ref, q_ref, kc_ref, vc_ref, ov_ref, o_ref, sel_ref, *, tb, nc, n_sel, pos0, emit_idx):
    kvh = pl.program_id(1)
    i = pl.program_id(2)
    g_count = N_HEADS // NSA_KV_HEADS
    ncp = kc_ref.shape[0]
    nsp = ov_ref.shape[1]
    qs = _stack_heads(q_ref[...], g_count).astype(BF16)
    kc = kc_ref[...].astype(BF16)
    vc = vc_ref[...].astype(BF16)
    s = lax.dot_general(qs, kc, (((1,), (1,)), ((), ())), preferred_element_type=F32) * SCALE
    t = pos0 + i * tb + lax.broadcasted_iota(jnp.int32, (tb, ncp), 0)
    c = lax.broadcasted_iota(jnp.int32, (tb, ncp), 1)
    valid = (c * CMP_STRIDE + (CMP_LEN - 1) <= t) & (c < nc)
    rel = t.astype(F32) - ((c * CMP_STRIDE).astype(F32) + (CMP_LEN - 1) / 2)
    psum = jnp.zeros((tb, ncp), F32)
    outs = []
    for g in range(g_count):
        sg = s[g * tb:(g + 1) * tb] - slopes_ref[kvh * g_count + g] * rel
        sg = jnp.where(valid, sg, 2 * NEG_FILL)
        m = jnp.maximum(jnp.max(sg, axis=-1, keepdims=True), NEG_FILL)
        p = jnp.exp(sg - m)
        p = p / jnp.maximum(jnp.sum(p, axis=-1, keepdims=True), 1e-30)
        psum = psum + p
        outs.append(jnp.dot(p.astype(BF16), vc, preferred_element_type=F32))
    o_ref[...] = jnp.concatenate(outs, axis=1)

    ov = ov_ref[...]
    imp = sum(jnp.dot(part, ov, preferred_element_type=F32) for part in _split3(psum))
    tq = pos0 + i * tb + lax.broadcasted_iota(jnp.int32, (tb, nsp), 0)
    j = lax.broadcasted_iota(jnp.int32, (tb, nsp), 1)
    cur = tq // SEL_BLOCK
    forced = (j == 0) | (j == cur) | (j == cur - 1)
    visible = j * SEL_BLOCK <= tq
    rank = jnp.where(forced, 1e9, jnp.where(visible, imp, -1.0))
    rank = jnp.where(j < n_sel, rank, -2.0)
    cnt = jnp.zeros((tb, nsp), jnp.int32)
    for jp in range(n_sel):
        col = rank[:, jp:jp + 1]
        beats = (col > rank) | ((col == rank) & (j > jp))
        cnt = cnt + beats.astype(jnp.int32)
    if emit_idx:
        lane = lax.broadcasted_iota(jnp.int32, (tb, LANE), 1)
        idx = jnp.zeros((tb, LANE), jnp.int32)
        jf = j.astype(F32)
        for r in range(SEL_TOPN):
            val = jnp.sum(jnp.where(cnt == r, jf, 0.0), axis=1, keepdims=True).astype(jnp.int32)
            idx = jnp.where(lane == r, val, idx)
        sel_ref[...] = idx
    else:
        sel_ref[...] = (cnt < SEL_TOPN).astype(F32)


def _overlap_matrix(ncp, n_sel, nsp):
    c_start = jnp.arange(ncp, dtype=jnp.int32)[:, None] * CMP_STRIDE
    j = jnp.arange(nsp, dtype=jnp.int32)[None, :]
    ov = jnp.clip(jnp.minimum(c_start + CMP_LEN, (j + 1) * SEL_BLOCK) - jnp.maximum(c_start, j * SEL_BLOCK), 0, None)
    ov = jnp.where(j < n_sel, ov, 0)
    return (ov.astype(F32) / CMP_LEN).astype(BF16)


def cmp_attention_prompt(proj, cmp_kv, slopes, *, batch, seq):
    g_count = N_HEADS // NSA_KV_HEADS
    tb = BAND_BLOCK
    nb = seq // tb
    ncp = cmp_kv.shape[3]
    n_sel = -(-seq // SEL_BLOCK)
    nsp = LANE
    ov = _overlap_matrix(ncp, n_sel, nsp)
    smem = pl.BlockSpec(memory_space=pltpu.SMEM)
    kv_block = (None, None, None, ncp, HEAD_DIM)
    return pl.pallas_call(
        functools.partial(_cmp_attn_kernel, tb=tb, nc=ncp - 1, n_sel=n_sel, pos0=0, emit_idx=False),
        grid=(batch, NSA_KV_HEADS, nb),
        in_specs=[smem,
                  pl.BlockSpec((tb, g_count * HEAD_DIM), lambda b, h, i: (b * nb + i, h)),
                  pl.BlockSpec(kv_block, lambda b, h, i: (b, 0, h, 0, 0)),
                  pl.BlockSpec(kv_block, lambda b, h, i: (b, 1, h, 0, 0)),
                  pl.BlockSpec((ncp, nsp), lambda b, h, i: (0, 0))],
        out_specs=[pl.BlockSpec((tb, g_count * HEAD_DIM), lambda b, h, i: (b * nb + i, h)),
                   pl.BlockSpec((None, None, tb, nsp), lambda b, h, i: (b, h, i, 0))],
        out_shape=[jax.ShapeDtypeStruct((batch * seq, N_HEADS * HEAD_DIM), F32),
                   jax.ShapeDtypeStruct((batch, NSA_KV_HEADS, seq, nsp), F32)],
        compiler_params=_params(("arbitrary", "arbitrary", "arbitrary")),
        name="cmp_attention_prompt",
    )(slopes, proj, cmp_kv, cmp_kv, ov)


def cmp_attention_sample(qkv8, cmp_kv, slopes, *, lk):
    g_count = N_HEADS // NSA_KV_HEADS
    nb = qkv8.shape[0]
    ncp = cmp_kv.shape[3]
    n_sel = -(-lk // SEL_BLOCK)
    nsp = -(-n_sel // LANE) * LANE
    ov = _overlap_matrix(ncp, n_sel, nsp)
    smem = pl.BlockSpec(memory_space=pltpu.SMEM)
    kv_block = (None, None, None, ncp, HEAD_DIM)
    return pl.pallas_call(
        functools.partial(_cmp_attn_kernel, tb=DEC_PAD, nc=ncp - 1, n_sel=n_sel, pos0=PAST_LEN, emit_idx=True),
        grid=(nb, NSA_KV_HEADS, 1),
        in_specs=[smem,
                  pl.BlockSpec((None, DEC_PAD, g_count * HEAD_DIM), lambda b, h, i: (b, 0, h)),
                  pl.BlockSpec(kv_block, lambda b, h, i: (b, 0, h, 0, 0)),
                  pl.BlockSpec(kv_block, lambda b, h, i: (b, 1, h, 0, 0)),
                  pl.BlockSpec((ncp, nsp), lambda b, h, i: (0, 0))],
        out_specs=[pl.BlockSpec((None, DEC_PAD, g_count * HEAD_DIM), lambda b, h, i: (b, 0, h)),
                   pl.BlockSpec((None, None, DEC_PAD, LANE), lambda b, h, i: (b, h, 0, 0))],
        out_shape=[jax.ShapeDtypeStruct((nb, DEC_PAD, N_HEADS * HEAD_DIM), F32),
                   jax.ShapeDtypeStruct((nb, NSA_KV_HEADS, DEC_PAD, LANE), jnp.int32)],
        compiler_params=_params(("arbitrary", "arbitrary", "arbitrary")),
        name="cmp_attention_sample",
    )(slopes, qkv8, cmp_kv, cmp_kv, ov)


SEL_KEY_TILE = 256

def _sel_prompt_kernel(slopes_ref, q_ref, k_ref, v_ref, sel_ref, o_ref, m_ref, l_ref, acc_ref):
    kvh = pl.program_id(1)
    i = pl.program_id(2)
    g_count = N_HEADS // NSA_KV_HEADS
    tb = BAND_BLOCK
    qs = _stack_heads(q_ref[...], g_count).astype(BF16)
    sel = sel_ref[...].astype(BF16)
    slope_col = jnp.concatenate(
        [jnp.full((tb, 1), slopes_ref[kvh * g_count + g], F32) for g in range(g_count)], axis=0)
    kt_w = SEL_KEY_TILE
    m_ref[...] = jnp.full(m_ref.shape, NEG_FILL, F32)
    l_ref[...] = jnp.zeros(l_ref.shape, F32)
    acc_ref[...] = jnp.zeros(acc_ref.shape, F32)
    tq = i * tb + lax.broadcasted_iota(jnp.int32, (tb, kt_w), 0)
    kk = lax.broadcasted_iota(jnp.int32, (tb, kt_w), 1)
    jj = lax.broadcasted_iota(jnp.int32, (LANE, kt_w), 0)
    kblk = lax.broadcasted_iota(jnp.int32, (LANE, kt_w), 1) // SEL_BLOCK

    def attend(k0, dist):
        ks = k_ref[pl.ds(k0, kt_w), :].astype(BF16)
        vs = v_ref[pl.ds(k0, kt_w), :].astype(BF16)
        s = lax.dot_general(qs, ks, (((1,), (1,)), ((), ())), preferred_element_type=F32) * SCALE
        distf = jnp.concatenate([dist] * g_count, axis=0)
        s = jnp.where(distf >= 0.0, s - slope_col * distf, 2 * NEG_FILL)
        m_old = m_ref[...]
        m_new = jnp.maximum(m_old, jnp.max(s, axis=-1, keepdims=True))
        alpha = jnp.exp(m_old - m_new)
        p = jnp.exp(s - jnp.concatenate([m_new] * (kt_w // LANE), axis=1))
        l_ref[...] = alpha * l_ref[...] + jnp.sum(p, axis=-1, keepdims=True)
        acc_ref[...] = alpha * acc_ref[...] + jnp.dot(p.astype(BF16), vs, preferred_element_type=F32)
        m_ref[...] = m_new

    def body(kt, carry):
        k0 = pl.multiple_of(kt * kt_w, kt_w)
        expand = (jj == kt * (kt_w // SEL_BLOCK) + kblk).astype(BF16)
        chosen = jnp.dot(sel, expand, preferred_element_type=F32)
        dist = (tq - (k0 + kk)).astype(F32)
        dist = jnp.where(chosen > 0.5, dist, -1.0)

        @pl.when(jnp.max(dist) >= 0.0)
        def _():
            attend(k0, dist)

        return carry

    n_tiles = (i * tb + tb + kt_w - 1) // kt_w
    lax.fori_loop(0, n_tiles, body, 0)
    o = acc_ref[...] / jnp.maximum(l_ref[...], 1e-30)
    o_ref[...] = jnp.concatenate([o[g * tb:(g + 1) * tb] for g in range(g_count)], axis=1)


def sel_attention_prompt(proj, sel_mask, slopes, *, batch, seq, k_col0, v_col0):
    g_count = N_HEADS // NSA_KV_HEADS
    tb = BAND_BLOCK
    nb = seq // tb
    kb0, vb0 = k_col0 // HEAD_DIM, v_col0 // HEAD_DIM
    smem = pl.BlockSpec(memory_space=pltpu.SMEM)
    return pl.pallas_call(
        _sel_prompt_kernel,
        grid=(batch, NSA_KV_HEADS, nb),
        in_specs=[smem,
                  pl.BlockSpec((tb, g_count * HEAD_DIM), lambda b, h, i: (b * nb + i, h)),
                  pl.BlockSpec((seq, HEAD_DIM), lambda b, h, i: (b, kb0 + h)),
                  pl.BlockSpec((seq, HEAD_DIM), lambda b, h, i: (b, vb0 + h)),
                  pl.BlockSpec((None, None, tb, LANE), lambda b, h, i: (b, h, i, 0))],
        out_specs=pl.BlockSpec((tb, g_count * HEAD_DIM), lambda b, h, i: (b * nb + i, h)),
        out_shape=jax.ShapeDtypeStruct((batch * seq, N_HEADS * HEAD_DIM), F32),
        scratch_shapes=[pltpu.VMEM((g_count * tb, HEAD_DIM), F32)] * 3,
        compiler_params=_params(("arbitrary", "arbitrary", "arbitrary")),
        name="sel_attention_prompt",
    )(slopes, proj, proj, proj, sel_mask)


def _sel_sample_kernel(idx_ref, pt_ref, q_ref, *refs, n_past_blocks):
    blocks = refs[:SEL_TOPN]
    kn_ref, vn_ref, slope_ref, o_ref = refs[SEL_TOPN:]
    b, kvh, t = pl.program_id(0), pl.program_id(1), pl.program_id(2)
    base = ((b * NSA_KV_HEADS + kvh) * DEC_SEQ + t) * SEL_TOPN
    g_count = N_HEADS // NSA_KV_HEADS
    pad = jnp.zeros((LANE - DEC_PAD, HEAD_DIM), F32)
    kall = jnp.concatenate([r[:, kvh, :] for r in blocks] + [kn_ref[...], pad], axis=0).astype(BF16)
    vall = jnp.concatenate([r[:, NSA_KV_HEADS + kvh, :] for r in blocks] + [vn_ref[...], pad], axis=0).astype(BF16)
    q = q_ref[...].astype(BF16)
    s = lax.dot_general(q, kall, (((1,), (1,)), ((), ())), preferred_element_type=F32) * SCALE
    lane = lax.broadcasted_iota(jnp.int32, (g_count, LANE), 1)
    low = lane < SEL_BLOCK
    qpos = PAST_LEN + t
    pos_parts, ok_parts = [], []
    for c in range(SEL_TOPN // 2):
        b0 = idx_ref[base + 2 * c]
        b1 = idx_ref[base + 2 * c + 1]
        p0 = jnp.where(b0 < n_past_blocks, b0 * SEL_BLOCK, PAST_LEN + DEC_SEQ)
        p1 = jnp.where(b1 < n_past_blocks, b1 * SEL_BLOCK, PAST_LEN + DEC_SEQ)
        pos_parts.append(jnp.where(low, p0 + lane, p1 + (lane - SEL_BLOCK)))
    pos_parts.append(PAST_LEN + lane)
    dist = qpos - jnp.concatenate(pos_parts, axis=1)
    valid = dist >= 0
    slope = jnp.concatenate([slope_ref[...]] * (SEL_TOPN // 2 + 1), axis=1)
    o_ref[...] = _softmax_pv(s - slope * dist.astype(F32), valid, vall)


def sel_attention_sample(q_s, pool_halves, page_table, sel_idx, new_kv8, slope_rows, layer_half0):
    nb = q_s.shape[0]
    g_count = N_HEADS // NSA_KV_HEADS
    n_past_blocks = PAST_LEN // SEL_BLOCK
    per_page = PAGE_SIZE // SEL_BLOCK

    def blk_spec(r):
        def index(b, h, t, idx, pt):
            blk = jnp.minimum(idx[((b * NSA_KV_HEADS + h) * DEC_SEQ + t) * SEL_TOPN + r], n_past_blocks - 1)
            return (layer_half0 + pt[b, blk // per_page] * per_page + blk % per_page, 0, 1, 0)
        return pl.BlockSpec((None, SEL_BLOCK, CMP_HEADS, HEAD_DIM), index)

    q_block = (None, None, None, g_count, HEAD_DIM)
    new_block = (None, None, None, DEC_PAD, HEAD_DIM)
    grid_spec = pltpu.PrefetchScalarGridSpec(
        num_scalar_prefetch=2,
        grid=(nb, NSA_KV_HEADS, DEC_SEQ),
        in_specs=[pl.BlockSpec(q_block, lambda b, h, t, idx, pt: (b, t, h, 0, 0))]
        + [blk_spec(r) for r in range(SEL_TOPN)]
        + [pl.BlockSpec(new_block, lambda b, h, t, idx, pt: (b, 2, h, 0, 0)),
           pl.BlockSpec(new_block, lambda b, h, t, idx, pt: (b, 3, h, 0, 0)),
           pl.BlockSpec((None, g_count, LANE), lambda b, h, t, idx, pt: (h, 0, 0))],
        out_specs=pl.BlockSpec(q_block, lambda b, h, t, idx, pt: (b, t, h, 0, 0)),
    )
    return pl.pallas_call(
        functools.partial(_sel_sample_kernel, n_past_blocks=n_past_blocks),
        grid_spec=grid_spec,
        out_shape=jax.ShapeDtypeStruct(q_s.shape, F32),
        compiler_params=_params(("arbitrary", "arbitrary", "arbitrary")),
        name="sel_attention_sample",
    )(sel_idx, page_table, q_s, *([pool_halves] * SEL_TOPN), new_kv8, new_kv8, slope_rows)


def _combine_kernel(oc_ref, os_ref, ow_ref, gl_ref, gb_ref, o_ref):
    gate = jax.nn.sigmoid(gl_ref[...] + gb_ref[...])
    for h in range(N_HEADS):
        sl = slice(h * HEAD_DIM, (h + 1) * HEAD_DIM)
        o = (gate[:, h:h + 1] * oc_ref[:, sl] + gate[:, N_HEADS + h:N_HEADS + h + 1] * os_ref[:, sl]
             + gate[:, 2 * N_HEADS + h:2 * N_HEADS + h + 1] * ow_ref[:, sl])
        o_ref[:, sl] = o.astype(o_ref.dtype)


def gate_combine(o_c, o_s, o_w, gate_logits, gate_bias, tm, row_block0=0):
    m, n = o_c.shape
    big = pl.BlockSpec((tm, n), lambda i: (i, 0))
    return pl.pallas_call(
        _combine_kernel,
        grid=(m // tm,),
        in_specs=[big, big, big,
                  pl.BlockSpec((tm, LANE), lambda i: (row_block0 + i, 0)),
                  pl.BlockSpec((1, LANE), lambda i: (0, 0))],
        out_specs=big,
        out_shape=jax.ShapeDtypeStruct((m, n), BF16),
        compiler_params=_params(("arbitrary",)),
        name="gate_combine",
    )(o_c, o_s, o_w, gate_logits, gate_bias)


TM_DENSE = 1040
TN_PROJ = 512
TF_FFN = 256
TM_FFN_OUT = 640
TN_FFN_OUT = 256
TM_NORM = 320
TM_PROMPT = 256
NQ = N_HEADS * HEAD_DIM


def _ffn_block(x, g, w_in, w_out_bf, layer, sub):
    xn = rms_norm(x, g, BF16, TM_NORM)
    h = ffn_in(xn, w_in, layer, sub, TM_DENSE, TF_FFN)
    return ffn_out(h, w_out_bf, layer, sub, x, TM_FFN_OUT, TN_FFN_OUT)


def _sample_rows8(proj):
    s = proj[M_PROMPT:].reshape(DEC_BATCH, DEC_SEQ, proj.shape[1])
    return jnp.pad(s, ((0, 0), (0, DEC_PAD - DEC_SEQ), (0, 0)))


def _swa_layer(x, g, w_in, w_out, sinks_all, cache_all, li, slopes):
    xn = rms_norm(x, g, BF16, TM_NORM)
    kv_cols = 2 * SWA_KV_HEADS * HEAD_DIM
    proj = matmul_wres(xn, w_in, (li,), NQ + kv_cols, TM_DENSE, TN_PROJ)
    sinks = sinks_all[li].astype(F32)
    o_p = band_attention(proj, slopes, sinks, batch=BATCH, seq=SEQ, kv_heads=SWA_KV_HEADS, k_col0=NQ,
                         v_col0=NQ + SWA_KV_HEADS * HEAD_DIM, window=SWA_WINDOW, use_sink=True, out_dtype=BF16)
    cache = cache_all[li]
    lc = cache.shape[1]
    o_s = decode_attention(_sample_rows8(proj), cache.reshape(DEC_BATCH, lc, kv_cols), slopes, sinks,
                           kv_heads=SWA_KV_HEADS, k_col0=NQ, v_col0=NQ + SWA_KV_HEADS * HEAD_DIM,
                           window=SWA_WINDOW, use_sink=True, out_dtype=BF16)
    attn = jnp.concatenate([o_p, o_s[:, :DEC_SEQ].reshape(M_SAMPLE, NQ)], axis=0)
    x = matmul_wres(attn, w_out, (li,), D_MODEL, TM_DENSE, TN_PROJ, res=x)
    kv_p = proj[:M_PROMPT, NQ:].reshape(BATCH, SEQ, 2, SWA_KV_HEADS, HEAD_DIM)
    kv_s = proj[M_PROMPT:, NQ:].reshape(DEC_BATCH, DEC_SEQ, 2, SWA_KV_HEADS, HEAD_DIM)
    buf_p = kv_p[:, SEQ - min(SWA_WINDOW, SEQ):]
    buf_s = jnp.concatenate([cache, kv_s], axis=1)[:, DEC_SEQ:]
    return x, buf_p, buf_s


def _nsa_layer(x, g, w_in, gate_b, cmp_pe, cmp_w1, cmp_w2, w_out, win_cache_all, pool_all, page_table, li, slopes):
    xn = rms_norm(x, g, BF16, TM_NORM)
    kvw = NSA_KV_HEADS * HEAD_DIM
    main_cols = NQ + 6 * kvw
    proj = matmul_wres(xn, w_in, (li,), main_cols, TM_DENSE, TN_PROJ)
    n_gate = 3 * N_HEADS
    w_gate = jnp.pad(w_in[li][:, main_cols:], ((0, 0), (0, LANE - n_gate)))
    gate_logits = matmul_wres(xn, w_gate, (), LANE, TM_DENSE, LANE)
    gate_bias = jnp.pad(gate_b[li], (0, LANE - n_gate)).reshape(1, LANE)
    zeros = jnp.zeros((N_HEADS,), F32)

    pet = pe_term(cmp_pe, cmp_w1, li)
    w1 = cmp_w1[li].astype(BF16)
    w1_slots = jnp.concatenate(
        [jnp.concatenate([w1[s, :CMP_STRIDE], w1[s, CMP_STRIDE:]], axis=-1) for s in range(2)], axis=-2)
    w2_slots = jnp.concatenate([cmp_w2[li, 0], cmp_w2[li, 1]], axis=0).astype(BF16)
    pet_slots = jnp.concatenate([pet[0, :NSA_KV_HEADS], pet[1, :NSA_KV_HEADS]], axis=0)

    o_w_p = band_attention(proj, slopes, zeros, batch=BATCH, seq=SEQ, kv_heads=NSA_KV_HEADS, k_col0=NQ + 4 * kvw,
                           v_col0=NQ + 5 * kvw, window=NSA_WINDOW, use_sink=False, out_dtype=F32)
    cmp_p = compress_prompt(proj, cmp_w1, cmp_w2, pet, li, batch=BATCH, seq=SEQ, col0=NQ)
    o_c_p, sel_mask = cmp_attention_prompt(proj, cmp_p, slopes, batch=BATCH, seq=SEQ)
    o_s_p = sel_attention_prompt(proj, sel_mask, slopes, batch=BATCH, seq=SEQ, k_col0=NQ + 2 * kvw,
                                 v_col0=NQ + 3 * kvw)
    attn_p = gate_combine(o_c_p, o_s_p, o_w_p, gate_logits, gate_bias, TM_PROMPT)

    qkv8 = _sample_rows8(proj)
    win_cache = win_cache_all[li]
    lc = win_cache.shape[1]
    o_w_s = decode_attention(qkv8, win_cache.reshape(DEC_BATCH, lc, 2 * kvw), slopes, zeros,
                             kv_heads=NSA_KV_HEADS, k_col0=NQ + 4 * kvw, v_col0=NQ + 5 * kvw, window=NSA_WINDOW,
                             use_sink=False, out_dtype=F32)
    n_pool = pool_all.shape[1]
    pool = pool_all.reshape(pool_all.shape[0] * n_pool, PAGE_SIZE, 4 * kvw)
    pool4 = pool.reshape(pool.shape[0], PAGE_SIZE, 2 * CMP_HEADS, HEAD_DIM)
    part_a, part_b = compress_pages_partial(pool4, page_table, w1_slots, li * n_pool)
    cmp_s = compress_pages_finish(part_a, part_b, w2_slots, pet_slots)
    o_c_s, sel_idx = cmp_attention_sample(qkv8, cmp_s, slopes, lk=PAST_LEN + DEC_SEQ)
    sel_flat = sel_idx[:, :, :DEC_SEQ, :SEL_TOPN].reshape(-1)
    rows_s = proj[M_PROMPT:, NQ:NQ + 4 * kvw].reshape(DEC_BATCH, DEC_SEQ, 4, NSA_KV_HEADS, HEAD_DIM)
    new_kv8 = jnp.pad(rows_s.transpose(0, 2, 3, 1, 4), ((0, 0), (0, 0), (0, 0), (0, DEC_PAD - DEC_SEQ), (0, 0)))
    g_count = N_HEADS // NSA_KV_HEADS
    q_s = proj[M_PROMPT:, :NQ].reshape(DEC_BATCH, DEC_SEQ, NSA_KV_HEADS, g_count, HEAD_DIM)
    slope_rows = jnp.broadcast_to(slopes.reshape(NSA_KV_HEADS, g_count, 1), (NSA_KV_HEADS, g_count, LANE))
    per_page = PAGE_SIZE // SEL_BLOCK
    pool_halves = pool.reshape(pool.shape[0] * per_page, SEL_BLOCK, 2 * CMP_HEADS, HEAD_DIM)
    o_s_s = sel_attention_sample(q_s, pool_halves, page_table, sel_flat, new_kv8, slope_rows, li * n_pool * per_page)
    attn_s = gate_combine(o_c_s[:, :DEC_SEQ].reshape(M_SAMPLE, NQ), o_s_s.reshape(M_SAMPLE, NQ),
                          o_w_s[:, :DEC_SEQ].reshape(M_SAMPLE, NQ), gate_logits, gate_bias, M_SAMPLE,
                          row_block0=M_PROMPT // M_SAMPLE)

    attn = jnp.concatenate([attn_p, attn_s], axis=0)
    x = matmul_wres(attn, w_out, (li,), D_MODEL, TM_DENSE, TN_PROJ, res=x)

    kv_p = proj[:M_PROMPT, NQ:main_cols].reshape(BATCH, SEQ, 6, NSA_KV_HEADS, HEAD_DIM)
    kv_s = proj[M_PROMPT:, NQ:main_cols].reshape(DEC_BATCH, DEC_SEQ, 6, NSA_KV_HEADS, HEAD_DIM)
    win_p = kv_p[:, SEQ - min(NSA_WINDOW, SEQ):, 4:]
    win_s = jnp.concatenate([win_cache, kv_s[:, :, 4:]], axis=1)[:, DEC_SEQ:]
    return x, kv_p[:, :, :4], kv_s[:, :, :4], win_p, win_s


def kernel(x_prompt, x_sample, cache_swa_kv, cache_nsa_win_kv, cache_nsa_kv, page_table, norm_g, final_norm_g,
           ffn_w_in, ffn_w_out, swa_w_in, swa_w_out, swa_sinks, nsa_w_in, nsa_gate_b, nsa_cmp_pe, nsa_cmp_w1,
           nsa_cmp_w2, nsa_w_out):
    x = jnp.concatenate([x_prompt.reshape(M_PROMPT, D_MODEL), x_sample.reshape(M_SAMPLE, D_MODEL)], axis=0)
    slopes = jnp.exp2(-8.0 * jnp.arange(1, N_HEADS + 1, dtype=F32) / N_HEADS)
    w_out_bf = ffn_w_out.astype(BF16)
    swa_p, swa_s, win_p, win_s, kv_p, kv_s = [], [], [], [], [], []
    for i in range(DEPTH):
        x = _ffn_block(x, norm_g[i, 0], ffn_w_in, w_out_bf, i, 0)
        li = i // N_MIXERS
        if i % N_MIXERS == 0:
            x, bp, bs = _swa_layer(x, norm_g[i, 1], swa_w_in, swa_w_out, swa_sinks, cache_swa_kv, li, slopes)
            swa_p.append(bp)
            swa_s.append(bs)
        else:
            x, rp, rs, wp, ws = _nsa_layer(x, norm_g[i, 1], nsa_w_in, nsa_gate_b, nsa_cmp_pe, nsa_cmp_w1,
                                           nsa_cmp_w2, nsa_w_out, cache_nsa_win_kv, cache_nsa_kv, page_table, li,
                                           slopes)
            kv_p.append(rp)
            kv_s.append(rs)
            win_p.append(wp)
            win_s.append(ws)
        x = _ffn_block(x, norm_g[i, 2], ffn_w_in, w_out_bf, i, 1)
    y_p = rms_norm(x, final_norm_g, F32, TM_PROMPT, rows=M_PROMPT).reshape(BATCH, SEQ, D_MODEL)
    y_s = rms_norm(x, final_norm_g, F32, M_SAMPLE, row_block0=M_PROMPT // M_SAMPLE, rows=M_SAMPLE)
    return (y_p, y_s.reshape(DEC_BATCH, DEC_SEQ, D_MODEL), jnp.stack(swa_p), jnp.stack(swa_s), jnp.stack(win_p),
            jnp.stack(win_s), jnp.stack(kv_p), jnp.stack(kv_s))
```

```python
import functools
import math

import jax
import jax.numpy as jnp
from jax import lax
from jax.experimental import pallas as pl
from jax.experimental.pallas import tpu as pltpu

D_MODEL = 4096
BATCH = 4
SEQ = 2048
DEPTH = 2
DEC_BATCH = 32
DEC_SEQ = 4
PAST_LEN = 16384
PAGE_SIZE = 128
N_HEADS = 32
HEAD_DIM = 128
SWA_KV_HEADS = 8
SWA_WINDOW = 128
NSA_KV_HEADS = 4
CMP_STRIDE = 16
CMP_LEN = 32
SEL_BLOCK = 64
SEL_TOPN = 16
NSA_WINDOW = 512
D_FF = 11008
BAND_BLOCK = 128
N_MIXERS = 2
RMS_EPS = 1e-6
NEG_FILL = -1e30
SCALE = HEAD_DIM ** -0.5

M_PROMPT = BATCH * SEQ
M_SAMPLE = DEC_BATCH * DEC_SEQ
M_ALL = M_PROMPT + M_SAMPLE
DEC_PAD = 8
LANE = 128
F32 = jnp.float32
BF16 = jnp.bfloat16
VMEM_LIMIT = 60 * 1024 * 1024


def _params(sem):
    return pltpu.CompilerParams(dimension_semantics=sem, vmem_limit_bytes=VMEM_LIMIT)


def _rms_kernel(x_ref, g_ref, o_ref):
    x = x_ref[...]
    y = x * lax.rsqrt(jnp.mean(x * x, axis=-1, keepdims=True) + RMS_EPS)
    o_ref[...] = (y * g_ref[...]).astype(o_ref.dtype)


def rms_norm(x, g, out_dtype, tm, row_block0=0, rows=None):
    m, d = x.shape
    rows = m if rows is None else rows
    return pl.pallas_call(
        _rms_kernel,
        grid=(rows // tm,),
        in_specs=[pl.BlockSpec((tm, d), lambda i: (row_block0 + i, 0)),
                  pl.BlockSpec((1, d), lambda i: (0, 0))],
        out_specs=pl.BlockSpec((tm, d), lambda i: (i, 0)),
        out_shape=jax.ShapeDtypeStruct((rows, d), out_dtype),
        compiler_params=_params(("arbitrary",)),
        name="rms_norm",
    )(x, g.reshape(1, d))


def _mm_kernel(x_ref, w_ref, *rest, scale, has_res):
    if has_res:
        res_ref, o_ref, wbf_ref = rest
    else:
        o_ref, wbf_ref = rest

    @pl.when(pl.program_id(1) == 0)
    def _():
        wbf_ref[...] = w_ref[...].astype(BF16)

    acc = jnp.dot(x_ref[...], wbf_ref[...], preferred_element_type=F32)
    if has_res:
        o_ref[...] = res_ref[...] + (acc if scale == 1.0 else scale * acc)
    else:
        o_ref[...] = acc.astype(o_ref.dtype)


def matmul_wres(x, w, w_prefix, n_cols, tm, tn, res=None, scale=1.0, col_block0=0):
    m, k = x.shape
    npre = len(w_prefix)
    w_block = (None,) * npre + (k, tn)
    in_specs = [pl.BlockSpec((tm, k), lambda j, i: (i, 0)),
                pl.BlockSpec(w_block, lambda j, i: tuple(w_prefix) + (0, col_block0 + j))]
    args = [x, w]
    if res is not None:
        in_specs.append(pl.BlockSpec((tm, tn), lambda j, i: (i, j)))
        args.append(res)
    return pl.pallas_call(
        functools.partial(_mm_kernel, scale=scale, has_res=res is not None),
        grid=(n_cols // tn, m // tm),
        in_specs=in_specs,
        out_specs=pl.BlockSpec((tm, tn), lambda j, i: (i, j)),
        out_shape=jax.ShapeDtypeStruct((m, n_cols), F32),
        scratch_shapes=[pltpu.VMEM((k, tn), BF16)],
        compiler_params=_params(("arbitrary", "arbitrary")),
        name="matmul_wres",
    )(*args)


def _ffn_in_kernel(x_ref, wg_ref, wu_ref, wo_ref, o_ref, wo_bf_ref, wg_bf, wu_bf):
    @pl.when(pl.program_id(1) == 0)
    def _():
        wg_bf[...] = wg_ref[...].astype(BF16)
        wu_bf[...] = wu_ref[...].astype(BF16)
        wo_bf_ref[...] = wo_ref[...].astype(BF16)

    x = x_ref[...]
    g = jnp.dot(x, wg_bf[...], preferred_element_type=F32)
    u = jnp.dot(x, wu_bf[...], preferred_element_type=F32)
    o_ref[...] = (jax.nn.silu(g) * u).astype(o_ref.dtype)


def ffn_in(xn, w_in, w_out, layer, sub, tm, tf):
    m, k = xn.shape
    nf = D_FF // tf
    n_out = w_out.shape[-1]
    w_block = (None, None, k, tf)
    return pl.pallas_call(
        _ffn_in_kernel,
        grid=(nf, m // tm),
        in_specs=[pl.BlockSpec((tm, k), lambda j, i: (i, 0)),
                  pl.BlockSpec(w_block, lambda j, i: (layer, sub, 0, j)),
                  pl.BlockSpec(w_block, lambda j, i: (layer, sub, 0, nf + j)),
                  pl.BlockSpec((None, None, tf, n_out), lambda j, i: (layer, sub, j, 0))],
        out_specs=[pl.BlockSpec((tm, tf), lambda j, i: (i, j)),
                   pl.BlockSpec((tf, n_out), lambda j, i: (j, 0))],
        out_shape=[jax.ShapeDtypeStruct((m, D_FF), BF16), jax.ShapeDtypeStruct((D_FF, n_out), BF16)],
        scratch_shapes=[pltpu.VMEM((k, tf), BF16), pltpu.VMEM((k, tf), BF16)],
        compiler_params=_params(("arbitrary", "arbitrary")),
        name="ffn_in",
    )(xn, w_in, w_in, w_out)


def _ffn_out_kernel(h_ref, w_ref, res_ref, o_ref):
    acc = jnp.dot(h_ref[...], w_ref[...], preferred_element_type=F32)
    o_ref[...] = res_ref[...] + 0.5 * acc


def ffn_out(h, w_out_bf, res, tm, tn):
    m, k = h.shape
    n = res.shape[1]
    return pl.pallas_call(
        _ffn_out_kernel,
        grid=(m // tm, n // tn),
        in_specs=[pl.BlockSpec((tm, k), lambda i, j: (i, 0)),
                  pl.BlockSpec((k, tn), lambda i, j: (0, j)),
                  pl.BlockSpec((tm, tn), lambda i, j: (i, j))],
        out_specs=pl.BlockSpec((tm, tn), lambda i, j: (i, j)),
        out_shape=jax.ShapeDtypeStruct((m, n), F32),
        compiler_params=_params(("arbitrary", "arbitrary")),
        name="ffn_out",
    )(h, w_out_bf, res)


def _stack_heads(q, g_count):
    return jnp.concatenate([q[:, g * HEAD_DIM:(g + 1) * HEAD_DIM] for g in range(g_count)], axis=0)


def _softmax_pv(s, valid, v_bf, sink=None):
    s = jnp.where(valid, s, 2 * NEG_FILL)
    m = jnp.maximum(jnp.max(s, axis=-1, keepdims=True), NEG_FILL)
    if sink is not None:
        m = jnp.maximum(m, sink)
    p = jnp.exp(s - m)
    den = jnp.sum(p, axis=-1, keepdims=True)
    if sink is not None:
        den = den + jnp.exp(sink - m)
    o = jnp.dot(p.astype(BF16), v_bf, preferred_element_type=F32)
    return o / jnp.maximum(den, 1e-30)


def _band_kernel(slopes_ref, sinks_ref, q_ref, k_ref, v_ref, o_ref, *, g_count, nprev, window, use_sink):
    kvh = pl.program_id(1)
    i = pl.program_id(2)
    width = (nprev + 1) * BAND_BLOCK
    start = pl.multiple_of(jnp.maximum(i - nprev, 0) * BAND_BLOCK, BAND_BLOCK)
    k = k_ref[pl.ds(start, width), :].astype(BF16)
    v = v_ref[pl.ds(start, width), :].astype(BF16)
    qs = _stack_heads(q_ref[...], g_count).astype(BF16)
    s = lax.dot_general(qs, k, (((1,), (1,)), ((), ())), preferred_element_type=F32) * SCALE
    tq = i * BAND_BLOCK + lax.broadcasted_iota(jnp.int32, (BAND_BLOCK, width), 0)
    kp = start + lax.broadcasted_iota(jnp.int32, (BAND_BLOCK, width), 1)
    dist = tq - kp
    valid = (dist >= 0) & (dist <= window)
    distf = dist.astype(F32)
    outs = []
    for g in range(g_count):
        h = kvh * g_count + g
        sg = s[g * BAND_BLOCK:(g + 1) * BAND_BLOCK] - slopes_ref[h] * distf
        outs.append(_softmax_pv(sg, valid, v, sinks_ref[h] if use_sink else None))
    o_ref[...] = jnp.concatenate(outs, axis=1).astype(o_ref.dtype)


def band_attention(proj, slopes, sinks, *, batch, seq, kv_heads, k_col0, v_col0, window, use_sink, out_dtype):
    g_count = N_HEADS // kv_heads
    nb = seq // BAND_BLOCK
    nprev = -(-window // BAND_BLOCK)
    kb0, vb0 = k_col0 // HEAD_DIM, v_col0 // HEAD_DIM
    smem = pl.BlockSpec(memory_space=pltpu.SMEM)
    return pl.pallas_call(
        functools.partial(_band_kernel, g_count=g_count, nprev=nprev, window=window, use_sink=use_sink),
        grid=(batch, kv_heads, nb),
        in_specs=[smem, smem,
                  pl.BlockSpec((BAND_BLOCK, g_count * HEAD_DIM), lambda b, h, i: (b * nb + i, h)),
                  pl.BlockSpec((seq, HEAD_DIM), lambda b, h, i: (b, kb0 + h)),
                  pl.BlockSpec((seq, HEAD_DIM), lambda b, h, i: (b, vb0 + h))],
        out_specs=pl.BlockSpec((BAND_BLOCK, g_count * HEAD_DIM), lambda b, h, i: (b * nb + i, h)),
        out_shape=jax.ShapeDtypeStruct((batch * seq, N_HEADS * HEAD_DIM), out_dtype),
        compiler_params=_params(("arbitrary", "arbitrary", "arbitrary")),
        name="band_attention",
    )(slopes, sinks, proj, proj, proj)


def _dec_kernel(slopes_ref, sinks_ref, qkv_ref, c_ref, o_ref, *, kv_heads, g_count, lc, k_col0, v_col0,
                window, use_sink):
    rows = g_count * DEC_PAD
    width = lc + LANE
    t = lax.broadcasted_iota(jnp.int32, (DEC_PAD, width), 0)
    col = lax.broadcasted_iota(jnp.int32, (DEC_PAD, width), 1)
    dist = jnp.where(col < lc, t + (lc - col), t - (col - lc))
    valid = (dist >= 0) & (dist <= window) & (col < lc + DEC_SEQ)
    distf = dist.astype(F32)
    pad = jnp.zeros((LANE - DEC_PAD, HEAD_DIM), F32)
    for kvh in range(kv_heads):
        q0 = kvh * g_count * HEAD_DIM
        qs = _stack_heads(qkv_ref[:, q0:q0 + g_count * HEAD_DIM], g_count).astype(BF16)
        kn = qkv_ref[:, k_col0 + kvh * HEAD_DIM:k_col0 + (kvh + 1) * HEAD_DIM]
        vn = qkv_ref[:, v_col0 + kvh * HEAD_DIM:v_col0 + (kvh + 1) * HEAD_DIM]
        kc = c_ref[:, kvh * HEAD_DIM:(kvh + 1) * HEAD_DIM]
        vc = c_ref[:, (kv_heads + kvh) * HEAD_DIM:(kv_heads + kvh + 1) * HEAD_DIM]
        kall = jnp.concatenate([kc, kn, pad], axis=0).astype(BF16)
        vall = jnp.concatenate([vc, vn, pad], axis=0).astype(BF16)
        s = lax.dot_general(qs, kall, (((1,), (1,)), ((), ())), preferred_element_type=F32) * SCALE
        assert s.shape == (rows, width)
        for g in range(g_count):
            h = kvh * g_count + g
            sg = s[g * DEC_PAD:(g + 1) * DEC_PAD] - slopes_ref[h] * distf
            o = _softmax_pv(sg, valid, vall, sinks_ref[h] if use_sink else None)
            o_ref[:, h * HEAD_DIM:(h + 1) * HEAD_DIM] = o.astype(o_ref.dtype)


def decode_attention(qkv8, cache, slopes, sinks, *, kv_heads, k_col0, v_col0, window, use_sink, out_dtype):
    nb, _, ncols = qkv8.shape
    lc = cache.shape[1]
    g_count = N_HEADS // kv_heads
    smem = pl.BlockSpec(memory_space=pltpu.SMEM)
    return pl.pallas_call(
        functools.partial(_dec_kernel, kv_heads=kv_heads, g_count=g_count, lc=lc, k_col0=k_col0, v_col0=v_col0,
                          window=window, use_sink=use_sink),
        grid=(nb,),
        in_specs=[smem, smem,
                  pl.BlockSpec((None, DEC_PAD, ncols), lambda b: (b, 0, 0)),
                  pl.BlockSpec((None, lc, cache.shape[2]), lambda b: (b, 0, 0))],
        out_specs=pl.BlockSpec((None, DEC_PAD, N_HEADS * HEAD_DIM), lambda b: (b, 0, 0)),
        out_shape=jax.ShapeDtypeStruct((nb, DEC_PAD, N_HEADS * HEAD_DIM), out_dtype),
        compiler_params=_params(("arbitrary",)),
        name="decode_attention",
    )(slopes, sinks, qkv8, cache)


def _pe_term_kernel(pe_ref, w1_ref, o_ref):
    acc = jnp.zeros((8, HEAD_DIM), F32)
    for l in range(CMP_LEN):
        row = jnp.broadcast_to(pe_ref[l:l + 1, :], (8, HEAD_DIM)).astype(BF16)
        acc = acc + jnp.dot(row, w1_ref[l].astype(BF16), preferred_element_type=F32)
    o_ref[...] = acc


def pe_term(pe, w1, layer):
    return pl.pallas_call(
        _pe_term_kernel,
        grid=(2,),
        in_specs=[pl.BlockSpec((None, None, CMP_LEN, HEAD_DIM), lambda s: (layer, s, 0, 0)),
                  pl.BlockSpec((None, None, CMP_LEN, HEAD_DIM, HEAD_DIM), lambda s: (layer, s, 0, 0, 0))],
        out_specs=pl.BlockSpec((None, 8, HEAD_DIM), lambda s: (s, 0, 0)),
        out_shape=jax.ShapeDtypeStruct((2, 8, HEAD_DIM), F32),
        compiler_params=_params(("arbitrary",)),
        name="pe_term",
    )(pe, w1)


def _compress_finish(acc_a, acc_b, pet_row, w2_bf):
    n = acc_a.shape[0]
    pre = acc_a + pltpu.roll(acc_b, n - 1, 0) + pet_row
    return jnp.dot(jax.nn.gelu(pre).astype(BF16), w2_bf, preferred_element_type=F32)


def _cmp_prompt_kernel(x_ref, w1_ref, w2_ref, pet_ref, o_ref, *, nch):
    acc_a = jnp.zeros((nch, HEAD_DIM), F32)
    acc_b = jnp.zeros((nch, HEAD_DIM), F32)
    for l in range(CMP_STRIDE):
        xl = x_ref[pl.ds(l, nch, stride=CMP_STRIDE), :].astype(BF16)
        acc_a = acc_a + jnp.dot(xl, w1_ref[l].astype(BF16), preferred_element_type=F32)
        acc_b = acc_b + jnp.dot(xl, w1_ref[CMP_STRIDE + l].astype(BF16), preferred_element_type=F32)
    o_ref[...] = _compress_finish(acc_a, acc_b, pet_ref[0:1, :], w2_ref[...].astype(BF16))


def compress_prompt(proj, w1, w2, pet, layer, *, batch, seq, col0):
    nch = seq // CMP_STRIDE
    cb0 = col0 // HEAD_DIM
    return pl.pallas_call(
        functools.partial(_cmp_prompt_kernel, nch=nch),
        grid=(batch, 2, NSA_KV_HEADS),
        in_specs=[pl.BlockSpec((seq, HEAD_DIM), lambda b, s, h: (b, cb0 + s * NSA_KV_HEADS + h)),
                  pl.BlockSpec((None, None, CMP_LEN, HEAD_DIM, HEAD_DIM), lambda b, s, h: (layer, s, 0, 0, 0)),
                  pl.BlockSpec((None, None, HEAD_DIM, HEAD_DIM), lambda b, s, h: (layer, s, 0, 0)),
                  pl.BlockSpec((None, 8, HEAD_DIM), lambda b, s, h: (s, 0, 0))],
        out_specs=pl.BlockSpec((None, None, None, nch, HEAD_DIM), lambda b, s, h: (b, s, h, 0, 0)),
        out_shape=jax.ShapeDtypeStruct((batch, 2, NSA_KV_HEADS, nch, HEAD_DIM), F32),
        compiler_params=_params(("arbitrary", "arbitrary", "arbitrary")),
        name="compress_prompt",
    )(proj, w1, w2, pet)


CMP_PAGES = 16
CHUNKS_PER_PAGE = PAGE_SIZE // CMP_STRIDE
CMP_HEADS = 2 * NSA_KV_HEADS
CMP_SLOT_ROWS = CMP_PAGES * CHUNKS_PER_PAGE * NSA_KV_HEADS


def _cmp_pages_kernel(pt_ref, *refs):
    pages = refs[:CMP_PAGES]
    w_ref, a_ref, b_ref = refs[CMP_PAGES:]
    low = lax.broadcasted_iota(jnp.int32, (CMP_HEADS, HEAD_DIM), 0) < NSA_KV_HEADS
    acc = [jnp.zeros((CMP_SLOT_ROWS, 2 * HEAD_DIM), F32) for _ in range(2)]
    for lp in range(CMP_STRIDE // 2):
        halves = ([], [])
        for l in (2 * lp, 2 * lp + 1):
            tiles = ([], [])
            for r in range(CMP_PAGES):
                for c in range(0, CHUNKS_PER_PAGE, 2):
                    even = pages[r][l + CMP_STRIDE * c]
                    odd = pages[r][l + CMP_STRIDE * (c + 1)]
                    tiles[0].append(jnp.where(low, even, pltpu.roll(odd, NSA_KV_HEADS, 0)))
                    tiles[1].append(jnp.where(low, pltpu.roll(even, NSA_KV_HEADS, 0), odd))
            for s in range(2):
                halves[s].append(jnp.concatenate(tiles[s], axis=0))
        for s in range(2):
            lhs = jnp.concatenate(halves[s], axis=1).astype(BF16)
            acc[s] = acc[s] + jnp.dot(lhs, w_ref[s, lp], preferred_element_type=F32)
    for s in range(2):
        a_ref[s] = acc[s][:, :HEAD_DIM]
        b_ref[s] = acc[s][:, HEAD_DIM:]


def compress_pages_partial(pool4, page_table, w1_pairs, layer_page0):
    nb, n_pages = page_table.shape
    n_groups = n_pages // CMP_PAGES

    def page_spec(r):
        return pl.BlockSpec((None, PAGE_SIZE, CMP_HEADS, HEAD_DIM),
                            lambda b, j, pt: (layer_page0 + pt[b, j * CMP_PAGES + r], 0, 0, 0))

    out_spec = pl.BlockSpec((None, 2, CMP_SLOT_ROWS, HEAD_DIM), lambda b, j, pt: (b, 0, j, 0))
    out_shape = jax.ShapeDtypeStruct((nb, 2, n_groups * CMP_SLOT_ROWS, HEAD_DIM), F32)
    grid_spec = pltpu.PrefetchScalarGridSpec(
        num_scalar_prefetch=1,
        grid=(nb, n_groups),
        in_specs=[page_spec(r) for r in range(CMP_PAGES)]
        + [pl.BlockSpec(w1_pairs.shape, lambda b, j, pt: (0, 0, 0, 0))],
        out_specs=[out_spec, out_spec],
    )
    return pl.pallas_call(
        _cmp_pages_kernel,
        grid_spec=grid_spec,
        out_shape=[out_shape, out_shape],
        compiler_params=_params(("arbitrary", "arbitrary")),
        name="compress_pages_partial",
    )(page_table, *([pool4] * CMP_PAGES), w1_pairs)


def _cmp_finish_kernel(a_ref, b_ref, w2_ref, pet_ref, o_ref, scr_ref):
    rows = a_ref.shape[0]
    b_next = pltpu.roll(b_ref[...], rows - NSA_KV_HEADS, 0)
    h = jax.nn.gelu(a_ref[...] + b_next + pet_ref[0:1, :])
    res = jnp.dot(h.astype(BF16), w2_ref[...].astype(BF16), preferred_element_type=F32)
    row = lax.broadcasted_iota(jnp.int32, (rows, HEAD_DIM), 0)
    scr_ref[...] = jnp.where(row >= rows - NSA_KV_HEADS, 0.0, res)
    for head in range(NSA_KV_HEADS):
        o_ref[head] = scr_ref[pl.ds(head, rows // NSA_KV_HEADS, stride=NSA_KV_HEADS), :]


def compress_pages_finish(a, b, w2, pet, layer):
    nb, _, rows, _ = a.shape
    blk = (None, None, rows, HEAD_DIM)
    return pl.pallas_call(
        _cmp_finish_kernel,
        grid=(nb, 2),
        in_specs=[pl.BlockSpec(blk, lambda b_, s: (b_, s, 0, 0)),
                  pl.BlockSpec(blk, lambda b_, s: (b_, s, 0, 0)),
                  pl.BlockSpec((None, None, HEAD_DIM, HEAD_DIM), lambda b_, s: (layer, s, 0, 0)),
                  pl.BlockSpec((None, 8, HEAD_DIM), lambda b_, s: (s, 0, 0))],
        out_specs=pl.BlockSpec((None, None, NSA_KV_HEADS, rows // NSA_KV_HEADS, HEAD_DIM),
                               lambda b_, s: (b_, s, 0, 0, 0)),
        out_shape=jax.ShapeDtypeStruct((nb, 2, NSA_KV_HEADS, rows // NSA_KV_HEADS, HEAD_DIM), F32),
        scratch_shapes=[pltpu.VMEM((rows, HEAD_DIM), F32)],
        compiler_params=_params(("arbitrary", "arbitrary")),
        name="compress_pages_finish",
    )(a, b, w2, pet)


def _split3(x):
    hi = x.astype(BF16)
    r1 = x - hi.astype(F32)
    mid = r1.astype(BF16)
    lo = (r1 - mid.astype(F32)).astype(BF16)
    return hi, mid, lo


def _cmp_attn_kernel(slopes_ref, q_ref, kc_ref, vc_ref, ov_ref, o_ref, sel_ref, *, tb, nc, n_sel, pos0, emit_idx):
    kvh = pl.program_id(1)
    i = pl.program_id(2)
    g_count = N_HEADS // NSA_KV_HEADS
    ncp = kc_ref.shape[0]
    nsp = ov_ref.shape[1]
    qs = _stack_heads(q_ref[...], g_count).astype(BF16)
    kc = kc_ref[...].astype(BF16)
    vc = vc_ref[...].astype(BF16)
    s = lax.dot_general(qs, kc, (((1,), (1,)), ((), ())), preferred_element_type=F32) * SCALE
    t = pos0 + i * tb + lax.broadcasted_iota(jnp.int32, (tb, ncp), 0)
    c = lax.broadcasted_iota(jnp.int32, (tb, ncp), 1)
    valid = (c * CMP_STRIDE + (CMP_LEN - 1) <= t) & (c < nc)
    rel = t.astype(F32) - ((c * CMP_STRIDE).astype(F32) + (CMP_LEN - 1) / 2)
    psum = jnp.zeros((tb, ncp), F32)
    outs = []
    for g in range(g_count):
        sg = s[g * tb:(g + 1) * tb] - slopes_ref[kvh * g_count + g] * rel
        sg = jnp.where(valid, sg, 2 * NEG_FILL)
        m = jnp.maximum(jnp.max(sg, axis=-1, keepdims=True), NEG_FILL)
        p = jnp.exp(sg - m)
        p = p / jnp.maximum(jnp.sum(p, axis=-1, keepdims=True), 1e-30)
        psum = psum + p
        outs.append(jnp.dot(p.astype(BF16), vc, preferred_element_type=F32))
    o_ref[...] = jnp.concatenate(outs, axis=1)

    ov = ov_ref[...]
    imp = sum(jnp.dot(part, ov, preferred_element_type=F32) for part in _split3(psum))
    tq = pos0 + i * tb + lax.broadcasted_iota(jnp.int32, (tb, nsp), 0)
    j = lax.broadcasted_iota(jnp.int32, (tb, nsp), 1)
    cur = tq // SEL_BLOCK
    forced = (j == 0) | (j == cur) | (j == cur - 1)
    visible = j * SEL_BLOCK <= tq
    rank = jnp.where(forced, 1e9, jnp.where(visible, imp, -1.0))
    rank = jnp.where(j < n_sel, rank, -2.0)
    cnt = jnp.zeros((tb, nsp), jnp.int32)
    for jp in range(n_sel):
        col = rank[:, jp:jp + 1]
        beats = (col > rank) | ((col == rank) & (j > jp))
        cnt = cnt + beats.astype(jnp.int32)
    if emit_idx:
        lane = lax.broadcasted_iota(jnp.int32, (tb, LANE), 1)
        idx = jnp.zeros((tb, LANE), jnp.int32)
        jf = j.astype(F32)
        for r in range(SEL_TOPN):
            val = jnp.sum(jnp.where(cnt == r, jf, 0.0), axis=1, keepdims=True).astype(jnp.int32)
            idx = jnp.where(lane == r, val, idx)
        sel_ref[...] = idx
    else:
        sel_ref[...] = (cnt < SEL_TOPN).astype(F32)


def _overlap_matrix(ncp, n_sel, nsp):
    c_start = jnp.arange(ncp, dtype=jnp.int32)[:, None] * CMP_STRIDE
    j = jnp.arange(nsp, dtype=jnp.int32)[None, :]
    ov = jnp.clip(jnp.minimum(c_start + CMP_LEN, (j + 1) * SEL_BLOCK) - jnp.maximum(c_start, j * SEL_BLOCK), 0, None)
    ov = jnp.where(j < n_sel, ov, 0)
    return (ov.astype(F32) / CMP_LEN).astype(BF16)


def cmp_attention_prompt(proj, cmp_kv, slopes, *, batch, seq):
    g_count = N_HEADS // NSA_KV_HEADS
    tb = BAND_BLOCK
    nb = seq // tb
    ncp = cmp_kv.shape[3]
    n_sel = -(-seq // SEL_BLOCK)
    nsp = LANE
    ov = _overlap_matrix(ncp, n_sel, nsp)
    smem = pl.BlockSpec(memory_space=pltpu.SMEM)
    kv_block = (None, None, None, ncp, HEAD_DIM)
    return pl.pallas_call(
        functools.partial(_cmp_attn_kernel, tb=tb, nc=ncp - 1, n_sel=n_sel, pos0=0, emit_idx=False),
        grid=(batch, NSA_KV_HEADS, nb),
        in_specs=[smem,
                  pl.BlockSpec((tb, g_count * HEAD_DIM), lambda b, h, i: (b * nb + i, h)),
                  pl.BlockSpec(kv_block, lambda b, h, i: (b, 0, h, 0, 0)),
                  pl.BlockSpec(kv_block, lambda b, h, i: (b, 1, h, 0, 0)),
                  pl.BlockSpec((ncp, nsp), lambda b, h, i: (0, 0))],
        out_specs=[pl.BlockSpec((tb, g_count * HEAD_DIM), lambda b, h, i: (b * nb + i, h)),
                   pl.BlockSpec((None, None, tb, nsp), lambda b, h, i: (b, h, i, 0))],
        out_shape=[jax.ShapeDtypeStruct((batch * seq, N_HEADS * HEAD_DIM), F32),
                   jax.ShapeDtypeStruct((batch, NSA_KV_HEADS, seq, nsp), F32)],
        compiler_params=_params(("arbitrary", "arbitrary", "arbitrary")),
        name="cmp_attention_prompt",
    )(slopes, proj, cmp_kv, cmp_kv, ov)


def cmp_attention_sample(qkv8, cmp_kv, slopes, *, lk):
    g_count = N_HEADS // NSA_KV_HEADS
    nb = qkv8.shape[0]
    ncp = cmp_kv.shape[3]
    n_sel = -(-lk // SEL_BLOCK)
    nsp = -(-n_sel // LANE) * LANE
    ov = _overlap_matrix(ncp, n_sel, nsp)
    smem = pl.BlockSpec(memory_space=pltpu.SMEM)
    kv_block = (None, None, None, ncp, HEAD_DIM)
    return pl.pallas_call(
        functools.partial(_cmp_attn_kernel, tb=DEC_PAD, nc=ncp - 1, n_sel=n_sel, pos0=PAST_LEN, emit_idx=True),
        grid=(nb, NSA_KV_HEADS, 1),
        in_specs=[smem,
                  pl.BlockSpec((None, DEC_PAD, g_count * HEAD_DIM), lambda b, h, i: (b, 0, h)),
                  pl.BlockSpec(kv_block, lambda b, h, i: (b, 0, h, 0, 0)),
                  pl.BlockSpec(kv_block, lambda b, h, i: (b, 1, h, 0, 0)),
                  pl.BlockSpec((ncp, nsp), lambda b, h, i: (0, 0))],
        out_specs=[pl.BlockSpec((None, DEC_PAD, g_count * HEAD_DIM), lambda b, h, i: (b, 0, h)),
                   pl.BlockSpec((None, None, DEC_PAD, LANE), lambda b, h, i: (b, h, 0, 0))],
        out_shape=[jax.ShapeDtypeStruct((nb, DEC_PAD, N_HEADS * HEAD_DIM), F32),
                   jax.ShapeDtypeStruct((nb, NSA_KV_HEADS, DEC_PAD, LANE), jnp.int32)],
        compiler_params=_params(("arbitrary", "arbitrary", "arbitrary")),
        name="cmp_attention_sample",
    )(slopes, qkv8, cmp_kv, cmp_kv, ov)


SEL_KEY_TILE = 256

def _sel_prompt_kernel(slopes_ref, q_ref, k_ref, v_ref, sel_ref, ex_ref, o_ref, m_ref, l_ref, acc_ref, chosen_ref):
    kvh = pl.program_id(1)
    i = pl.program_id(2)
    g_count = N_HEADS // NSA_KV_HEADS
    tb = BAND_BLOCK
    qs = _stack_heads(q_ref[...], g_count).astype(BF16)
    sel = sel_ref[...].astype(BF16)
    slope_col = jnp.concatenate(
        [jnp.full((tb, 1), slopes_ref[kvh * g_count + g], F32) for g in range(g_count)], axis=0)
    kt_w = SEL_KEY_TILE
    m_ref[...] = jnp.full(m_ref.shape, NEG_FILL, F32)
    l_ref[...] = jnp.zeros(l_ref.shape, F32)
    acc_ref[...] = jnp.zeros(acc_ref.shape, F32)
    for kt in range(chosen_ref.shape[0]):
        chosen_ref[kt] = jnp.dot(sel, ex_ref[:, kt * kt_w:(kt + 1) * kt_w], preferred_element_type=F32)
    tq = i * tb + lax.broadcasted_iota(jnp.int32, (tb, kt_w), 0)
    kk = lax.broadcasted_iota(jnp.int32, (tb, kt_w), 1)

    def attend(k0, dist):
        ks = k_ref[pl.ds(k0, kt_w), :].astype(BF16)
        vs = v_ref[pl.ds(k0, kt_w), :].astype(BF16)
        s = lax.dot_general(qs, ks, (((1,), (1,)), ((), ())), preferred_element_type=F32) * SCALE
        distf = jnp.concatenate([dist] * g_count, axis=0)
        s = jnp.where(distf >= 0.0, s - slope_col * distf, 2 * NEG_FILL)
        m_old = m_ref[...]
        m_new = jnp.maximum(m_old, jnp.max(s, axis=-1, keepdims=True))
        alpha = jnp.exp(m_old - m_new)
        p = jnp.exp(s - jnp.concatenate([m_new] * (kt_w // LANE), axis=1))
        l_ref[...] = alpha * l_ref[...] + jnp.sum(p, axis=-1, keepdims=True)
        acc_ref[...] = alpha * acc_ref[...] + jnp.dot(p.astype(BF16), vs, preferred_element_type=F32)
        m_ref[...] = m_new

    def body(kt, carry):
        k0 = pl.multiple_of(kt * kt_w, kt_w)
        dist = (tq - (k0 + kk)).astype(F32)
        dist = jnp.where(chosen_ref[kt] > 0.5, dist, -1.0)

        @pl.when(jnp.max(dist) >= 0.0)
        def _():
            attend(k0, dist)

        return carry

    n_tiles = (i * tb + tb + kt_w - 1) // kt_w
    lax.fori_loop(0, n_tiles, body, 0)
    o = acc_ref[...] / jnp.maximum(l_ref[...], 1e-30)
    o_ref[...] = jnp.concatenate([o[g * tb:(g + 1) * tb] for g in range(g_count)], axis=1)


def sel_attention_prompt(proj, sel_mask, slopes, *, batch, seq, k_col0, v_col0):
    g_count = N_HEADS // NSA_KV_HEADS
    tb = BAND_BLOCK
    nb = seq // tb
    kb0, vb0 = k_col0 // HEAD_DIM, v_col0 // HEAD_DIM
    smem = pl.BlockSpec(memory_space=pltpu.SMEM)
    block_of_key = jnp.arange(seq, dtype=jnp.int32)[None, :] // SEL_BLOCK
    expand = (jnp.arange(LANE, dtype=jnp.int32)[:, None] == block_of_key).astype(BF16)
    return pl.pallas_call(
        _sel_prompt_kernel,
        grid=(batch, NSA_KV_HEADS, nb),
        in_specs=[smem,
                  pl.BlockSpec((tb, g_count * HEAD_DIM), lambda b, h, i: (b * nb + i, h)),
                  pl.BlockSpec((seq, HEAD_DIM), lambda b, h, i: (b, kb0 + h)),
                  pl.BlockSpec((seq, HEAD_DIM), lambda b, h, i: (b, vb0 + h)),
                  pl.BlockSpec((None, None, tb, LANE), lambda b, h, i: (b, h, i, 0)),
                  pl.BlockSpec((LANE, seq), lambda b, h, i: (0, 0))],
        out_specs=pl.BlockSpec((tb, g_count * HEAD_DIM), lambda b, h, i: (b * nb + i, h)),
        out_shape=jax.ShapeDtypeStruct((batch * seq, N_HEADS * HEAD_DIM), F32),
        scratch_shapes=[pltpu.VMEM((g_count * tb, HEAD_DIM), F32)] * 3
        + [pltpu.VMEM((seq // SEL_KEY_TILE, tb, SEL_KEY_TILE), F32)],
        compiler_params=_params(("arbitrary", "arbitrary", "arbitrary")),
        name="sel_attention_prompt",
    )(slopes, proj, proj, proj, sel_mask, expand)


def _sel_sample_kernel(idx_ref, pt_ref, q_ref, *refs, n_past_blocks):
    blocks = refs[:SEL_TOPN]
    kn_ref, vn_ref, slope_ref, o_ref = refs[SEL_TOPN:]
    b, kvh, t = pl.program_id(0), pl.program_id(1), pl.program_id(2)
    base = ((b * NSA_KV_HEADS + kvh) * DEC_SEQ + t) * SEL_TOPN
    g_count = N_HEADS // NSA_KV_HEADS
    pad = jnp.zeros((LANE - DEC_PAD, HEAD_DIM), F32)
    kall = jnp.concatenate([r[:, kvh, :] for r in blocks] + [kn_ref[...], pad], axis=0).astype(BF16)
    vall = jnp.concatenate([r[:, NSA_KV_HEADS + kvh, :] for r in blocks] + [vn_ref[...], pad], axis=0).astype(BF16)
    q = q_ref[...].astype(BF16)
    s = lax.dot_general(q, kall, (((1,), (1,)), ((), ())), preferred_element_type=F32) * SCALE
    lane = lax.broadcasted_iota(jnp.int32, (g_count, LANE), 1)
    low = lane < SEL_BLOCK
    qpos = PAST_LEN + t
    pos_parts, ok_parts = [], []
    for c in range(SEL_TOPN // 2):
        b0 = idx_ref[base + 2 * c]
        b1 = idx_ref[base + 2 * c + 1]
        p0 = jnp.where(b0 < n_past_blocks, b0 * SEL_BLOCK, PAST_LEN + DEC_SEQ)
        p1 = jnp.where(b1 < n_past_blocks, b1 * SEL_BLOCK, PAST_LEN + DEC_SEQ)
        pos_parts.append(jnp.where(low, p0 + lane, p1 + (lane - SEL_BLOCK)))
    pos_parts.append(PAST_LEN + lane)
    dist = qpos - jnp.concatenate(pos_parts, axis=1)
    valid = dist >= 0
    slope = jnp.concatenate([slope_ref[...]] * (SEL_TOPN // 2 + 1), axis=1)
    o_ref[...] = _softmax_pv(s - slope * dist.astype(F32), valid, vall)


def sel_attention_sample(q_s, pool_halves, page_table, sel_idx, new_kv8, slope_rows, layer_half0):
    nb = q_s.shape[0]
    g_count = N_HEADS // NSA_KV_HEADS
    n_past_blocks = PAST_LEN // SEL_BLOCK
    per_page = PAGE_SIZE // SEL_BLOCK
    page_shift = per_page.bit_length() - 1
    assert per_page == 1 << page_shift

    def blk_spec(r):
        def index(b, h, t, idx, pt):
            blk = jnp.minimum(idx[((b * NSA_KV_HEADS + h) * DEC_SEQ + t) * SEL_TOPN + r], n_past_blocks - 1)
            page = pt[b, lax.shift_right_logical(blk, page_shift)]
            return (layer_half0 + page * per_page + jnp.bitwise_and(blk, per_page - 1), 0, 1, 0)
        return pl.BlockSpec((None, SEL_BLOCK, CMP_HEADS, HEAD_DIM), index)

    q_block = (None, None, None, g_count, HEAD_DIM)
    new_block = (None, None, None, DEC_PAD, HEAD_DIM)
    grid_spec = pltpu.PrefetchScalarGridSpec(
        num_scalar_prefetch=2,
        grid=(nb, NSA_KV_HEADS, DEC_SEQ),
        in_specs=[pl.BlockSpec(q_block, lambda b, h, t, idx, pt: (b, t, h, 0, 0))]
        + [blk_spec(r) for r in range(SEL_TOPN)]
        + [pl.BlockSpec(new_block, lambda b, h, t, idx, pt: (b, 2, h, 0, 0)),
           pl.BlockSpec(new_block, lambda b, h, t, idx, pt: (b, 3, h, 0, 0)),
           pl.BlockSpec((None, g_count, LANE), lambda b, h, t, idx, pt: (h, 0, 0))],
        out_specs=pl.BlockSpec(q_block, lambda b, h, t, idx, pt: (b, t, h, 0, 0)),
    )
    return pl.pallas_call(
        functools.partial(_sel_sample_kernel, n_past_blocks=n_past_blocks),
        grid_spec=grid_spec,
        out_shape=jax.ShapeDtypeStruct(q_s.shape, F32),
        compiler_params=_params(("arbitrary", "arbitrary", "arbitrary")),
        name="sel_attention_sample",
    )(sel_idx, page_table, q_s, *([pool_halves] * SEL_TOPN), new_kv8, new_kv8, slope_rows)


def _combine_kernel(oc_ref, os_ref, ow_ref, gl_ref, gb_ref, o_ref):
    gate = jax.nn.sigmoid(gl_ref[...] + gb_ref[...])
    for h in range(N_HEADS):
        sl = slice(h * HEAD_DIM, (h + 1) * HEAD_DIM)
        o = (gate[:, h:h + 1] * oc_ref[:, sl] + gate[:, N_HEADS + h:N_HEADS + h + 1] * os_ref[:, sl]
             + gate[:, 2 * N_HEADS + h:2 * N_HEADS + h + 1] * ow_ref[:, sl])
        o_ref[:, sl] = o.astype(o_ref.dtype)


def gate_combine(o_c, o_s, o_w, gate_logits, gate_bias, tm, row_block0=0):
    m, n = o_c.shape
    big = pl.BlockSpec((tm, n), lambda i: (i, 0))
    return pl.pallas_call(
        _combine_kernel,
        grid=(m // tm,),
        in_specs=[big, big, big,
                  pl.BlockSpec((tm, LANE), lambda i: (row_block0 + i, 0)),
                  pl.BlockSpec((1, LANE), lambda i: (0, 0))],
        out_specs=big,
        out_shape=jax.ShapeDtypeStruct((m, n), BF16),
        compiler_params=_params(("arbitrary",)),
        name="gate_combine",
    )(o_c, o_s, o_w, gate_logits, gate_bias)


TM_DENSE = 1040
TN_PROJ = 512
TF_FFN = 256
TM_FFN_OUT = 640
TN_FFN_OUT = 256
TM_NORM = 320
TM_PROMPT = 256
NQ = N_HEADS * HEAD_DIM


def _ffn_block(x, g, w_in, w_out, layer, sub):
    xn = rms_norm(x, g, BF16, TM_NORM)
    h, w_out_bf = ffn_in(xn, w_in, w_out, layer, sub, TM_DENSE, TF_FFN)
    return ffn_out(h, w_out_bf, x, TM_FFN_OUT, TN_FFN_OUT)


def _sample_rows8(proj):
    s = proj[M_PROMPT:].reshape(DEC_BATCH, DEC_SEQ, proj.shape[1])
    return jnp.pad(s, ((0, 0), (0, DEC_PAD - DEC_SEQ), (0, 0)))


def _swa_layer(x, g, w_in, w_out, sinks_all, cache_all, li, slopes):
    xn = rms_norm(x, g, BF16, TM_NORM)
    kv_cols = 2 * SWA_KV_HEADS * HEAD_DIM
    proj = matmul_wres(xn, w_in, (li,), NQ + kv_cols, TM_DENSE, TN_PROJ)
    sinks = sinks_all[li].astype(F32)
    o_p = band_attention(proj, slopes, sinks, batch=BATCH, seq=SEQ, kv_heads=SWA_KV_HEADS, k_col0=NQ,
                         v_col0=NQ + SWA_KV_HEADS * HEAD_DIM, window=SWA_WINDOW, use_sink=True, out_dtype=BF16)
    cache = cache_all[li]
    lc = cache.shape[1]
    o_s = decode_attention(_sample_rows8(proj), cache.reshape(DEC_BATCH, lc, kv_cols), slopes, sinks,
                           kv_heads=SWA_KV_HEADS, k_col0=NQ, v_col0=NQ + SWA_KV_HEADS * HEAD_DIM,
                           window=SWA_WINDOW, use_sink=True, out_dtype=BF16)
    attn = jnp.concatenate([o_p, o_s[:, :DEC_SEQ].reshape(M_SAMPLE, NQ)], axis=0)
    x = matmul_wres(attn, w_out, (li,), D_MODEL, TM_DENSE, TN_PROJ, res=x)
    kv_p = proj[:M_PROMPT, NQ:].reshape(BATCH, SEQ, 2, SWA_KV_HEADS, HEAD_DIM)
    kv_s = proj[M_PROMPT:, NQ:].reshape(DEC_BATCH, DEC_SEQ, 2, SWA_KV_HEADS, HEAD_DIM)
    buf_p = kv_p[:, SEQ - min(SWA_WINDOW, SEQ):]
    buf_s = jnp.concatenate([cache, kv_s], axis=1)[:, DEC_SEQ:]
    return x, buf_p, buf_s


def _nsa_layer(x, g, w_in, gate_b, cmp_pe, cmp_w1, cmp_w2, w_out, win_cache_all, pool_all, page_table, li, slopes):
    xn = rms_norm(x, g, BF16, TM_NORM)
    kvw = NSA_KV_HEADS * HEAD_DIM
    main_cols = NQ + 6 * kvw
    proj = matmul_wres(xn, w_in, (li,), main_cols, TM_DENSE, TN_PROJ)
    n_gate = 3 * N_HEADS
    w_gate = jnp.pad(w_in[li][:, main_cols:], ((0, 0), (0, LANE - n_gate)))
    gate_logits = matmul_wres(xn, w_gate, (), LANE, TM_DENSE, LANE)
    gate_bias = jnp.pad(gate_b[li], (0, LANE - n_gate)).reshape(1, LANE)
    zeros = jnp.zeros((N_HEADS,), F32)

    pet = pe_term(cmp_pe, cmp_w1, li)
    w1 = cmp_w1[li].astype(BF16)
    top = jnp.concatenate([w1[:, 0:CMP_STRIDE:2], w1[:, CMP_STRIDE::2]], axis=-1)
    bot = jnp.concatenate([w1[:, 1:CMP_STRIDE:2], w1[:, CMP_STRIDE + 1::2]], axis=-1)
    w1_pairs = jnp.concatenate([top, bot], axis=-2)

    o_w_p = band_attention(proj, slopes, zeros, batch=BATCH, seq=SEQ, kv_heads=NSA_KV_HEADS, k_col0=NQ + 4 * kvw,
                           v_col0=NQ + 5 * kvw, window=NSA_WINDOW, use_sink=False, out_dtype=F32)
    cmp_p = compress_prompt(proj, cmp_w1, cmp_w2, pet, li, batch=BATCH, seq=SEQ, col0=NQ)
    o_c_p, sel_mask = cmp_attention_prompt(proj, cmp_p, slopes, batch=BATCH, seq=SEQ)
    o_s_p = sel_attention_prompt(proj, sel_mask, slopes, batch=BATCH, seq=SEQ, k_col0=NQ + 2 * kvw,
                                 v_col0=NQ + 3 * kvw)
    attn_p = gate_combine(o_c_p, o_s_p, o_w_p, gate_logits, gate_bias, TM_PROMPT)

    qkv8 = _sample_rows8(proj)
    win_cache = win_cache_all[li]
    lc = win_cache.shape[1]
    o_w_s = decode_attention(qkv8, win_cache.reshape(DEC_BATCH, lc, 2 * kvw), slopes, zeros,
                             kv_heads=NSA_KV_HEADS, k_col0=NQ + 4 * kvw, v_col0=NQ + 5 * kvw, window=NSA_WINDOW,
                             use_sink=False, out_dtype=F32)
    n_pool = pool_all.shape[1]
    pool = pool_all.reshape(pool_all.shape[0] * n_pool, PAGE_SIZE, 4 * kvw)
    pool4 = pool.reshape(pool.shape[0], PAGE_SIZE, 2 * CMP_HEADS, HEAD_DIM)
    part_a, part_b = compress_pages_partial(pool4, page_table, w1_pairs, li * n_pool)
    cmp_s = compress_pages_finish(part_a, part_b, cmp_w2, pet, li)
    o_c_s, sel_idx = cmp_attention_sample(qkv8, cmp_s, slopes, lk=PAST_LEN + DEC_SEQ)
    sel_flat = sel_idx[:, :, :DEC_SEQ, :SEL_TOPN].reshape(-1)
    rows_s = proj[M_PROMPT:, NQ:NQ + 4 * kvw].reshape(DEC_BATCH, DEC_SEQ, 4, NSA_KV_HEADS, HEAD_DIM)
    new_kv8 = jnp.pad(rows_s.transpose(0, 2, 3, 1, 4), ((0, 0), (0, 0), (0, 0), (0, DEC_PAD - DEC_SEQ), (0, 0)))
    g_count = N_HEADS // NSA_KV_HEADS
    q_s = proj[M_PROMPT:, :NQ].reshape(DEC_BATCH, DEC_SEQ, NSA_KV_HEADS, g_count, HEAD_DIM)
    slope_rows = jnp.broadcast_to(slopes.reshape(NSA_KV_HEADS, g_count, 1), (NSA_KV_HEADS, g_count, LANE))
    per_page = PAGE_SIZE // SEL_BLOCK
    pool_halves = pool.reshape(pool.shape[0] * per_page, SEL_BLOCK, 2 * CMP_HEADS, HEAD_DIM)
    o_s_s = sel_attention_sample(q_s, pool_halves, page_table, sel_flat, new_kv8, slope_rows, li * n_pool * per_page)
    attn_s = gate_combine(o_c_s[:, :DEC_SEQ].reshape(M_SAMPLE, NQ), o_s_s.reshape(M_SAMPLE, NQ),
                          o_w_s[:, :DEC_SEQ].reshape(M_SAMPLE, NQ), gate_logits, gate_bias, M_SAMPLE,
                          row_block0=M_PROMPT // M_SAMPLE)

    attn = jnp.concatenate([attn_p, attn_s], axis=0)
    x = matmul_wres(attn, w_out, (li,), D_MODEL, TM_DENSE, TN_PROJ, res=x)

    kv_p = proj[:M_PROMPT, NQ:main_cols].reshape(BATCH, SEQ, 6, NSA_KV_HEADS, HEAD_DIM)
    kv_s = proj[M_PROMPT:, NQ:main_cols].reshape(DEC_BATCH, DEC_SEQ, 6, NSA_KV_HEADS, HEAD_DIM)
    win_p = kv_p[:, SEQ - min(NSA_WINDOW, SEQ):, 4:]
    win_s = jnp.concatenate([win_cache, kv_s[:, :, 4:]], axis=1)[:, DEC_SEQ:]
    return x, kv_p[:, :, :4], kv_s[:, :, :4], win_p, win_s


def kernel(x_prompt, x_sample, cache_swa_kv, cache_nsa_win_kv, cache_nsa_kv, page_table, norm_g, final_norm_g,
           ffn_w_in, ffn_w_out, swa_w_in, swa_w_out, swa_sinks, nsa_w_in, nsa_gate_b, nsa_cmp_pe, nsa_cmp_w1,
           nsa_cmp_w2, nsa_w_out):
    x = jnp.concatenate([x_prompt.reshape(M_PROMPT, D_MODEL), x_sample.reshape(M_SAMPLE, D_MODEL)], axis=0)
    slopes = jnp.exp2(-8.0 * jnp.arange(1, N_HEADS + 1, dtype=F32) / N_HEADS)
    swa_p, swa_s, win_p, win_s, kv_p, kv_s = [], [], [], [], [], []
    for i in range(DEPTH):
        x = _ffn_block(x, norm_g[i, 0], ffn_w_in, ffn_w_out, i, 0)
        li = i // N_MIXERS
        if i % N_MIXERS == 0:
            x, bp, bs = _swa_layer(x, norm_g[i, 1], swa_w_in, swa_w_out, swa_sinks, cache_swa_kv, li, slopes)
            swa_p.append(bp)
            swa_s.append(bs)
        else:
            x, rp, rs, wp, ws = _nsa_layer(x, norm_g[i, 1], nsa_w_in, nsa_gate_b, nsa_cmp_pe, nsa_cmp_w1,
                                           nsa_cmp_w2, nsa_w_out, cache_nsa_win_kv, cache_nsa_kv, page_table, li,
                                           slopes)
            kv_p.append(rp)
            kv_s.append(rs)
            win_p.append(wp)
            win_s.append(ws)
        x = _ffn_block(x, norm_g[i, 2], ffn_w_in, ffn_w_out, i, 1)
    y_p = rms_norm(x, final_norm_g, F32, TM_PROMPT, rows=M_PROMPT).reshape(BATCH, SEQ, D_MODEL)
    y_s = rms_norm(x, final_norm_g, F32, M_SAMPLE, row_block0=M_PROMPT // M_SAMPLE, rows=M_SAMPLE)
    return (y_p, y_s.reshape(DEC_BATCH, DEC_SEQ, D_MODEL), jnp.stack(swa_p), jnp.stack(swa_s), jnp.stack(win_p),
            jnp.stack(win_s), jnp.stack(kv_p), jnp.stack(kv_s))
```

```python
import functools
import math

import jax
import jax.numpy as jnp
from jax import lax
from jax.experimental import pallas as pl
from jax.experimental.pallas import tpu as pltpu

D_MODEL = 4096
BATCH = 4
SEQ = 2048
DEPTH = 2
DEC_BATCH = 32
DEC_SEQ = 4
PAST_LEN = 16384
PAGE_SIZE = 128
N_HEADS = 32
HEAD_DIM = 128
SWA_KV_HEADS = 8
SWA_WINDOW = 128
NSA_KV_HEADS = 4
CMP_STRIDE = 16
CMP_LEN = 32
SEL_BLOCK = 64
SEL_TOPN = 16
NSA_WINDOW = 512
D_FF = 11008
BAND_BLOCK = 128
N_MIXERS = 2
RMS_EPS = 1e-6
NEG_FILL = -1e30
SCALE = HEAD_DIM ** -0.5

M_PROMPT = BATCH * SEQ
M_SAMPLE = DEC_BATCH * DEC_SEQ
M_ALL = M_PROMPT + M_SAMPLE
DEC_PAD = 8
LANE = 128
F32 = jnp.float32
BF16 = jnp.bfloat16
VMEM_LIMIT = 60 * 1024 * 1024


def _params(sem):
    return pltpu.CompilerParams(dimension_semantics=sem, vmem_limit_bytes=VMEM_LIMIT)


def _rms_kernel(x_ref, g_ref, o_ref):
    x = x_ref[...]
    y = x * lax.rsqrt(jnp.mean(x * x, axis=-1, keepdims=True) + RMS_EPS)
    o_ref[...] = (y * g_ref[...]).astype(o_ref.dtype)


def rms_norm(x, g, out_dtype, tm, row_block0=0, rows=None):
    m, d = x.shape
    rows = m if rows is None else rows
    return pl.pallas_call(
        _rms_kernel,
        grid=(rows // tm,),
        in_specs=[pl.BlockSpec((tm, d), lambda i: (row_block0 + i, 0)),
                  pl.BlockSpec((1, d), lambda i: (0, 0))],
        out_specs=pl.BlockSpec((tm, d), lambda i: (i, 0)),
        out_shape=jax.ShapeDtypeStruct((rows, d), out_dtype),
        compiler_params=_params(("arbitrary",)),
        name="rms_norm",
    )(x, g.reshape(1, d))


def _row_tile_spec(tm, k):
    return pl.BlockSpec((tm, k), lambda i, j: (i, 0), pipeline_mode=pl.Buffered(1))


def _mm_kernel(x_ref, w_ref, *rest, scale, has_res):
    acc = jnp.dot(x_ref[...], w_ref[...].astype(BF16), preferred_element_type=F32)
    if has_res:
        res_ref, o_ref = rest
        o_ref[...] = res_ref[...] + (acc if scale == 1.0 else scale * acc)
    else:
        (o_ref,) = rest
        o_ref[...] = acc.astype(o_ref.dtype)


def matmul_wres(x, w, w_prefix, n_cols, tm, tn, res=None, scale=1.0, col_block0=0):
    m, k = x.shape
    npre = len(w_prefix)
    w_block = (None,) * npre + (k, tn)
    in_specs = [_row_tile_spec(tm, k),
                pl.BlockSpec(w_block, lambda i, j: tuple(w_prefix) + (0, col_block0 + j))]
    args = [x, w]
    if res is not None:
        in_specs.append(pl.BlockSpec((tm, tn), lambda i, j: (i, j)))
        args.append(res)
    return pl.pallas_call(
        functools.partial(_mm_kernel, scale=scale, has_res=res is not None),
        grid=(m // tm, n_cols // tn),
        in_specs=in_specs,
        out_specs=pl.BlockSpec((tm, tn), lambda i, j: (i, j)),
        out_shape=jax.ShapeDtypeStruct((m, n_cols), F32),
        compiler_params=_params(("arbitrary", "arbitrary")),
        name="matmul_wres",
    )(*args)


def _ffn_in_kernel(x_ref, wg_ref, wu_ref, o_ref):
    x = x_ref[...]
    g = jnp.dot(x, wg_ref[...].astype(BF16), preferred_element_type=F32)
    u = jnp.dot(x, wu_ref[...].astype(BF16), preferred_element_type=F32)
    o_ref[...] = (jax.nn.silu(g) * u).astype(o_ref.dtype)


def ffn_in(xn, w_in, layer, sub, tm, tf):
    m, k = xn.shape
    nf = D_FF // tf
    w_block = (None, None, k, tf)
    return pl.pallas_call(
        _ffn_in_kernel,
        grid=(m // tm, nf),
        in_specs=[_row_tile_spec(tm, k),
                  pl.BlockSpec(w_block, lambda i, j: (layer, sub, 0, j)),
                  pl.BlockSpec(w_block, lambda i, j: (layer, sub, 0, nf + j))],
        out_specs=pl.BlockSpec((tm, tf), lambda i, j: (i, j)),
        out_shape=jax.ShapeDtypeStruct((m, D_FF), BF16),
        compiler_params=_params(("arbitrary", "arbitrary")),
        name="ffn_in",
    )(xn, w_in, w_in)


def _ffn_out_kernel(h_ref, w_ref, res_ref, o_ref):
    acc = jnp.dot(h_ref[...], w_ref[...], preferred_element_type=F32)
    o_ref[...] = res_ref[...] + 0.5 * acc


def ffn_out(h, w_out_bf, layer, sub, res, tm, tn):
    m, k = h.shape
    n = res.shape[1]
    return pl.pallas_call(
        _ffn_out_kernel,
        grid=(m // tm, n // tn),
        in_specs=[_row_tile_spec(tm, k),
                  pl.BlockSpec((None, None, k, tn), lambda i, j: (layer, sub, 0, j)),
                  pl.BlockSpec((tm, tn), lambda i, j: (i, j))],
        out_specs=pl.BlockSpec((tm, tn), lambda i, j: (i, j)),
        out_shape=jax.ShapeDtypeStruct((m, n), F32),
        compiler_params=_params(("arbitrary", "arbitrary")),
        name="ffn_out",
    )(h, w_out_bf, res)


def _stack_heads(q, g_count):
    return jnp.concatenate([q[:, g * HEAD_DIM:(g + 1) * HEAD_DIM] for g in range(g_count)], axis=0)


def _softmax_pv(s, valid, v_bf, sink=None):
    s = jnp.where(valid, s, 2 * NEG_FILL)
    m = jnp.maximum(jnp.max(s, axis=-1, keepdims=True), NEG_FILL)
    if sink is not None:
        m = jnp.maximum(m, sink)
    p = jnp.exp(s - m)
    den = jnp.sum(p, axis=-1, keepdims=True)
    if sink is not None:
        den = den + jnp.exp(sink - m)
    o = jnp.dot(p.astype(BF16), v_bf, preferred_element_type=F32)
    return o / jnp.maximum(den, 1e-30)


def _band_kernel(slopes_ref, sinks_ref, q_ref, k_ref, v_ref, o_ref, *, g_count, nprev, window, use_sink):
    kvh = pl.program_id(1)
    i = pl.program_id(2)
    width = (nprev + 1) * BAND_BLOCK
    start = pl.multiple_of(jnp.maximum(i - nprev, 0) * BAND_BLOCK, BAND_BLOCK)
    k = k_ref[pl.ds(start, width), :].astype(BF16)
    v = v_ref[pl.ds(start, width), :].astype(BF16)
    qs = _stack_heads(q_ref[...], g_count).astype(BF16)
    s = lax.dot_general(qs, k, (((1,), (1,)), ((), ())), preferred_element_type=F32) * SCALE
    tq = i * BAND_BLOCK + lax.broadcasted_iota(jnp.int32, (BAND_BLOCK, width), 0)
    kp = start + lax.broadcasted_iota(jnp.int32, (BAND_BLOCK, width), 1)
    dist = tq - kp
    valid = (dist >= 0) & (dist <= window)
    distf = dist.astype(F32)
    outs = []
    for g in range(g_count):
        h = kvh * g_count + g
        sg = s[g * BAND_BLOCK:(g + 1) * BAND_BLOCK] - slopes_ref[h] * distf
        outs.append(_softmax_pv(sg, valid, v, sinks_ref[h] if use_sink else None))
    o_ref[...] = jnp.concatenate(outs, axis=1).astype(o_ref.dtype)


def band_attention(proj, slopes, sinks, *, batch, seq, kv_heads, k_col0, v_col0, window, use_sink, out_dtype):
    g_count = N_HEADS // kv_heads
    nb = seq // BAND_BLOCK
    nprev = -(-window // BAND_BLOCK)
    kb0, vb0 = k_col0 // HEAD_DIM, v_col0 // HEAD_DIM
    smem = pl.BlockSpec(memory_space=pltpu.SMEM)
    return pl.pallas_call(
        functools.partial(_band_kernel, g_count=g_count, nprev=nprev, window=window, use_sink=use_sink),
        grid=(batch, kv_heads, nb),
        in_specs=[smem, smem,
                  pl.BlockSpec((BAND_BLOCK, g_count * HEAD_DIM), lambda b, h, i: (b * nb + i, h)),
                  pl.BlockSpec((seq, HEAD_DIM), lambda b, h, i: (b, kb0 + h)),
                  pl.BlockSpec((seq, HEAD_DIM), lambda b, h, i: (b, vb0 + h))],
        out_specs=pl.BlockSpec((BAND_BLOCK, g_count * HEAD_DIM), lambda b, h, i: (b * nb + i, h)),
        out_shape=jax.ShapeDtypeStruct((batch * seq, N_HEADS * HEAD_DIM), out_dtype),
        compiler_params=_params(("arbitrary", "arbitrary", "arbitrary")),
        name="band_attention",
    )(slopes, sinks, proj, proj, proj)


def _dec_kernel(slopes_ref, sinks_ref, qkv_ref, c_ref, o_ref, *, kv_heads, g_count, lc, k_col0, v_col0,
                window, use_sink):
    rows = g_count * DEC_PAD
    width = lc + LANE
    t = lax.broadcasted_iota(jnp.int32, (DEC_PAD, width), 0)
    col = lax.broadcasted_iota(jnp.int32, (DEC_PAD, width), 1)
    dist = jnp.where(col < lc, t + (lc - col), t - (col - lc))
    valid = (dist >= 0) & (dist <= window) & (col < lc + DEC_SEQ)
    distf = dist.astype(F32)
    pad = jnp.zeros((LANE - DEC_PAD, HEAD_DIM), F32)
    for kvh in range(kv_heads):
        q0 = kvh * g_count * HEAD_DIM
        qs = _stack_heads(qkv_ref[:, q0:q0 + g_count * HEAD_DIM], g_count).astype(BF16)
        kn = qkv_ref[:, k_col0 + kvh * HEAD_DIM:k_col0 + (kvh + 1) * HEAD_DIM]
        vn = qkv_ref[:, v_col0 + kvh * HEAD_DIM:v_col0 + (kvh + 1) * HEAD_DIM]
        kc = c_ref[:, kvh * HEAD_DIM:(kvh + 1) * HEAD_DIM]
        vc = c_ref[:, (kv_heads + kvh) * HEAD_DIM:(kv_heads + kvh + 1) * HEAD_DIM]
        kall = jnp.concatenate([kc, kn, pad], axis=0).astype(BF16)
        vall = jnp.concatenate([vc, vn, pad], axis=0).astype(BF16)
        s = lax.dot_general(qs, kall, (((1,), (1,)), ((), ())), preferred_element_type=F32) * SCALE
        assert s.shape == (rows, width)
        for g in range(g_count):
            h = kvh * g_count + g
            sg = s[g * DEC_PAD:(g + 1) * DEC_PAD] - slopes_ref[h] * distf
            o = _softmax_pv(sg, valid, vall, sinks_ref[h] if use_sink else None)
            o_ref[:, h * HEAD_DIM:(h + 1) * HEAD_DIM] = o.astype(o_ref.dtype)


def decode_attention(qkv8, cache, slopes, sinks, *, kv_heads, k_col0, v_col0, window, use_sink, out_dtype):
    nb, _, ncols = qkv8.shape
    lc = cache.shape[1]
    g_count = N_HEADS // kv_heads
    smem = pl.BlockSpec(memory_space=pltpu.SMEM)
    return pl.pallas_call(
        functools.partial(_dec_kernel, kv_heads=kv_heads, g_count=g_count, lc=lc, k_col0=k_col0, v_col0=v_col0,
                          window=window, use_sink=use_sink),
        grid=(nb,),
        in_specs=[smem, smem,
                  pl.BlockSpec((None, DEC_PAD, ncols), lambda b: (b, 0, 0)),
                  pl.BlockSpec((None, lc, cache.shape[2]), lambda b: (b, 0, 0))],
        out_specs=pl.BlockSpec((None, DEC_PAD, N_HEADS * HEAD_DIM), lambda b: (b, 0, 0)),
        out_shape=jax.ShapeDtypeStruct((nb, DEC_PAD, N_HEADS * HEAD_DIM), out_dtype),
        compiler_params=_params(("arbitrary",)),
        name="decode_attention",
    )(slopes, sinks, qkv8, cache)


def _pe_term_kernel(pe_ref, w1_ref, o_ref):
    acc = jnp.zeros((8, HEAD_DIM), F32)
    for l in range(CMP_LEN):
        row = jnp.broadcast_to(pe_ref[l:l + 1, :], (8, HEAD_DIM)).astype(BF16)
        acc = acc + jnp.dot(row, w1_ref[l].astype(BF16), preferred_element_type=F32)
    o_ref[...] = acc


def pe_term(pe, w1, layer):
    return pl.pallas_call(
        _pe_term_kernel,
        grid=(2,),
        in_specs=[pl.BlockSpec((None, None, CMP_LEN, HEAD_DIM), lambda s: (layer, s, 0, 0)),
                  pl.BlockSpec((None, None, CMP_LEN, HEAD_DIM, HEAD_DIM), lambda s: (layer, s, 0, 0, 0))],
        out_specs=pl.BlockSpec((None, 8, HEAD_DIM), lambda s: (s, 0, 0)),
        out_shape=jax.ShapeDtypeStruct((2, 8, HEAD_DIM), F32),
        compiler_params=_params(("arbitrary",)),
        name="pe_term",
    )(pe, w1)


def _compress_finish(acc_a, acc_b, pet_row, w2_bf):
    n = acc_a.shape[0]
    pre = acc_a + pltpu.roll(acc_b, n - 1, 0) + pet_row
    return jnp.dot(jax.nn.gelu(pre).astype(BF16), w2_bf, preferred_element_type=F32)


def _cmp_prompt_kernel(x_ref, w1_ref, w2_ref, pet_ref, o_ref, *, nch):
    acc_a = jnp.zeros((nch, HEAD_DIM), F32)
    acc_b = jnp.zeros((nch, HEAD_DIM), F32)
    for l in range(CMP_STRIDE):
        xl = x_ref[pl.ds(l, nch, stride=CMP_STRIDE), :].astype(BF16)
        acc_a = acc_a + jnp.dot(xl, w1_ref[l].astype(BF16), preferred_element_type=F32)
        acc_b = acc_b + jnp.dot(xl, w1_ref[CMP_STRIDE + l].astype(BF16), preferred_element_type=F32)
    o_ref[...] = _compress_finish(acc_a, acc_b, pet_ref[0:1, :], w2_ref[...].astype(BF16))


def compress_prompt(proj, w1, w2, pet, layer, *, batch, seq, col0):
    nch = seq // CMP_STRIDE
    cb0 = col0 // HEAD_DIM
    return pl.pallas_call(
        functools.partial(_cmp_prompt_kernel, nch=nch),
        grid=(batch, 2, NSA_KV_HEADS),
        in_specs=[pl.BlockSpec((seq, HEAD_DIM), lambda b, s, h: (b, cb0 + s * NSA_KV_HEADS + h)),
                  pl.BlockSpec((None, None, CMP_LEN, HEAD_DIM, HEAD_DIM), lambda b, s, h: (layer, s, 0, 0, 0)),
                  pl.BlockSpec((None, None, HEAD_DIM, HEAD_DIM), lambda b, s, h: (layer, s, 0, 0)),
                  pl.BlockSpec((None, 8, HEAD_DIM), lambda b, s, h: (s, 0, 0))],
        out_specs=pl.BlockSpec((None, None, None, nch, HEAD_DIM), lambda b, s, h: (b, s, h, 0, 0)),
        out_shape=jax.ShapeDtypeStruct((batch, 2, NSA_KV_HEADS, nch, HEAD_DIM), F32),
        compiler_params=_params(("arbitrary", "arbitrary", "arbitrary")),
        name="compress_prompt",
    )(proj, w1, w2, pet)


CMP_PAGES = 16
CHUNKS_PER_PAGE = PAGE_SIZE // CMP_STRIDE
CMP_HEADS = 2 * NSA_KV_HEADS
CMP_SLOT_ROWS = CMP_PAGES * CHUNKS_PER_PAGE * NSA_KV_HEADS


def _cmp_pages_kernel(pt_ref, *refs):
    pages = refs[:CMP_PAGES]
    w_ref, a_ref, b_ref = refs[CMP_PAGES:]
    low = lax.broadcasted_iota(jnp.int32, (CMP_HEADS, HEAD_DIM), 0) < NSA_KV_HEADS
    acc = [jnp.zeros((CMP_SLOT_ROWS, 2 * HEAD_DIM), F32) for _ in range(2)]
    for lp in range(CMP_STRIDE // 2):
        halves = ([], [])
        for l in (2 * lp, 2 * lp + 1):
            tiles = ([], [])
            for r in range(CMP_PAGES):
                for c in range(0, CHUNKS_PER_PAGE, 2):
                    even = pages[r][l + CMP_STRIDE * c]
                    odd = pages[r][l + CMP_STRIDE * (c + 1)]
                    tiles[0].append(jnp.where(low, even, pltpu.roll(odd, NSA_KV_HEADS, 0)))
                    tiles[1].append(jnp.where(low, pltpu.roll(even, NSA_KV_HEADS, 0), odd))
            for s in range(2):
                halves[s].append(jnp.concatenate(tiles[s], axis=0))
        for s in range(2):
            lhs = jnp.concatenate(halves[s], axis=1).astype(BF16)
            acc[s] = acc[s] + jnp.dot(lhs, w_ref[s, lp], preferred_element_type=F32)
    for s in range(2):
        a_ref[s] = acc[s][:, :HEAD_DIM]
        b_ref[s] = acc[s][:, HEAD_DIM:]


def compress_pages_partial(pool4, page_table, w1_pairs, layer_page0):
    nb, n_pages = page_table.shape
    n_groups = n_pages // CMP_PAGES

    def page_spec(r):
        return pl.BlockSpec((None, PAGE_SIZE, CMP_HEADS, HEAD_DIM),
                            lambda b, j, pt: (layer_page0 + pt[b, j * CMP_PAGES + r], 0, 0, 0))

    out_spec = pl.BlockSpec((None, 2, CMP_SLOT_ROWS, HEAD_DIM), lambda b, j, pt: (b, 0, j, 0))
    out_shape = jax.ShapeDtypeStruct((nb, 2, n_groups * CMP_SLOT_ROWS, HEAD_DIM), F32)
    grid_spec = pltpu.PrefetchScalarGridSpec(
        num_scalar_prefetch=1,
        grid=(nb, n_groups),
        in_specs=[page_spec(r) for r in range(CMP_PAGES)]
        + [pl.BlockSpec(w1_pairs.shape, lambda b, j, pt: (0, 0, 0, 0))],
        out_specs=[out_spec, out_spec],
    )
    return pl.pallas_call(
        _cmp_pages_kernel,
        grid_spec=grid_spec,
        out_shape=[out_shape, out_shape],
        compiler_params=_params(("arbitrary", "arbitrary")),
        name="compress_pages_partial",
    )(page_table, *([pool4] * CMP_PAGES), w1_pairs)


def _cmp_finish_kernel(a_ref, b_ref, w2_ref, pet_ref, o_ref, scr_ref):
    rows = a_ref.shape[0]
    b_next = pltpu.roll(b_ref[...], rows - NSA_KV_HEADS, 0)
    h = jax.nn.gelu(a_ref[...] + b_next + pet_ref[0:1, :])
    res = jnp.dot(h.astype(BF16), w2_ref[...].astype(BF16), preferred_element_type=F32)
    row = lax.broadcasted_iota(jnp.int32, (rows, HEAD_DIM), 0)
    scr_ref[...] = jnp.where(row >= rows - NSA_KV_HEADS, 0.0, res)
    for head in range(NSA_KV_HEADS):
        o_ref[head] = scr_ref[pl.ds(head, rows // NSA_KV_HEADS, stride=NSA_KV_HEADS), :]


def compress_pages_finish(a, b, w2, pet, layer):
    nb, _, rows, _ = a.shape
    blk = (None, None, rows, HEAD_DIM)
    return pl.pallas_call(
        _cmp_finish_kernel,
        grid=(nb, 2),
        in_specs=[pl.BlockSpec(blk, lambda b_, s: (b_, s, 0, 0)),
                  pl.BlockSpec(blk, lambda b_, s: (b_, s, 0, 0)),
                  pl.BlockSpec((None, None, HEAD_DIM, HEAD_DIM), lambda b_, s: (layer, s, 0, 0)),
                  pl.BlockSpec((None, 8, HEAD_DIM), lambda b_, s: (s, 0, 0))],
        out_specs=pl.BlockSpec((None, None, NSA_KV_HEADS, rows // NSA_KV_HEADS, HEAD_DIM),
                               lambda b_, s: (b_, s, 0, 0, 0)),
        out_shape=jax.ShapeDtypeStruct((nb, 2, NSA_KV_HEADS, rows // NSA_KV_HEADS, HEAD_DIM), F32),
        scratch_shapes=[pltpu.VMEM((rows, HEAD_DIM), F32)],
        compiler_params=_params(("arbitrary", "arbitrary")),
        name="compress_pages_finish",
    )(a, b, w2, pet)


def _split3(x):
    hi = x.astype(BF16)
    r1 = x - hi.astype(F32)
    mid = r1.astype(BF16)
    lo = (r1 - mid.astype(F32)).astype(BF16)
    return hi, mid, lo


def _cmp_attn_kernel(slopes_ref, q_ref, kc_ref, vc_ref, ov_ref, o_ref, sel_ref, *, tb, nc, n_sel, pos0, emit_idx):
    kvh = pl.program_id(1)
    i = pl.program_id(2)
    g_count = N_HEADS // NSA_KV_HEADS
    ncp = kc_ref.shape[0]
    nsp = ov_ref.shape[1]
    qs = _stack_heads(q_ref[...], g_count).astype(BF16)
    kc = kc_ref[...].astype(BF16)
    vc = vc_ref[...].astype(BF16)
    s = lax.dot_general(qs, kc, (((1,), (1,)), ((), ())), preferred_element_type=F32) * SCALE
    t = pos0 + i * tb + lax.broadcasted_iota(jnp.int32, (tb, ncp), 0)
    c = lax.broadcasted_iota(jnp.int32, (tb, ncp), 1)
    valid = (c * CMP_STRIDE + (CMP_LEN - 1) <= t) & (c < nc)
    rel = t.astype(F32) - ((c * CMP_STRIDE).astype(F32) + (CMP_LEN - 1) / 2)
    psum = jnp.zeros((tb, ncp), F32)
    outs = []
    for g in range(g_count):
        sg = s[g * tb:(g + 1) * tb] - slopes_ref[kvh * g_count + g] * rel
        sg = jnp.where(valid, sg, 2 * NEG_FILL)
        m = jnp.maximum(jnp.max(sg, axis=-1, keepdims=True), NEG_FILL)
        p = jnp.exp(sg - m)
        p = p / jnp.maximum(jnp.sum(p, axis=-1, keepdims=True), 1e-30)
        psum = psum + p
        outs.append(jnp.dot(p.astype(BF16), vc, preferred_element_type=F32))
    o_ref[...] = jnp.concatenate(outs, axis=1)

    ov = ov_ref[...]
    imp = sum(jnp.dot(part, ov, preferred_element_type=F32) for part in _split3(psum))
    tq = pos0 + i * tb + lax.broadcasted_iota(jnp.int32, (tb, nsp), 0)
    j = lax.broadcasted_iota(jnp.int32, (tb, nsp), 1)
    cur = tq // SEL_BLOCK
    forced = (j == 0) | (j == cur) | (j == cur - 1)
    visible = j * SEL_BLOCK <= tq
    rank = jnp.where(forced, 1e9, jnp.where(visible, imp, -1.0))
    rank = jnp.where(j < n_sel, rank, -2.0)
    cnt = jnp.zeros((tb, nsp), jnp.int32)
    for jp in range(n_sel):
        col = rank[:, jp:jp + 1]
        beats = (col > rank) | ((col == rank) & (j > jp))
        cnt = cnt + beats.astype(jnp.int32)
    if emit_idx:
        lane = lax.broadcasted_iota(jnp.int32, (tb, LANE), 1)
        idx = jnp.zeros((tb, LANE), jnp.int32)
        jf = j.astype(F32)
        for r in range(SEL_TOPN):
            val = jnp.sum(jnp.where(cnt == r, jf, 0.0), axis=1, keepdims=True).astype(jnp.int32)
            idx = jnp.where(lane == r, val, idx)
        sel_ref[...] = idx
    else:
        sel_ref[...] = (cnt < SEL_TOPN).astype(F32)


def _overlap_matrix(ncp, n_sel, nsp):
    c_start = jnp.arange(ncp, dtype=jnp.int32)[:, None] * CMP_STRIDE
    j = jnp.arange(nsp, dtype=jnp.int32)[None, :]
    ov = jnp.clip(jnp.minimum(c_start + CMP_LEN, (j + 1) * SEL_BLOCK) - jnp.maximum(c_start, j * SEL_BLOCK), 0, None)
    ov = jnp.where(j < n_sel, ov, 0)
    return (ov.astype(F32) / CMP_LEN).astype(BF16)


def cmp_attention_prompt(proj, cmp_kv, slopes, *, batch, seq):
    g_count = N_HEADS // NSA_KV_HEADS
    tb = BAND_BLOCK
    nb = seq // tb
    ncp = cmp_kv.shape[3]
    n_sel = -(-seq // SEL_BLOCK)
    nsp = LANE
    ov = _overlap_matrix(ncp, n_sel, nsp)
    smem = pl.BlockSpec(memory_space=pltpu.SMEM)
    kv_block = (None, None, None, ncp, HEAD_DIM)
    return pl.pallas_call(
        functools.partial(_cmp_attn_kernel, tb=tb, nc=ncp - 1, n_sel=n_sel, pos0=0, emit_idx=False),
        grid=(batch, NSA_KV_HEADS, nb),
        in_specs=[smem,
                  pl.BlockSpec((tb, g_count * HEAD_DIM), lambda b, h, i: (b * nb + i, h)),
                  pl.BlockSpec(kv_block, lambda b, h, i: (b, 0, h, 0, 0)),
                  pl.BlockSpec(kv_block, lambda b, h, i: (b, 1, h, 0, 0)),
                  pl.BlockSpec((ncp, nsp), lambda b, h, i: (0, 0))],
        out_specs=[pl.BlockSpec((tb, g_count * HEAD_DIM), lambda b, h, i: (b * nb + i, h)),
                   pl.BlockSpec((None, None, tb, nsp), lambda b, h, i: (b, h, i, 0))],
        out_shape=[jax.ShapeDtypeStruct((batch * seq, N_HEADS * HEAD_DIM), F32),
                   jax.ShapeDtypeStruct((batch, NSA_KV_HEADS, seq, nsp), F32)],
        compiler_params=_params(("arbitrary", "arbitrary", "arbitrary")),
        name="cmp_attention_prompt",
    )(slopes, proj, cmp_kv, cmp_kv, ov)


def cmp_attention_sample(qkv8, cmp_kv, slopes, *, lk):
    g_count = N_HEADS // NSA_KV_HEADS
    nb = qkv8.shape[0]
    ncp = cmp_kv.shape[3]
    n_sel = -(-lk // SEL_BLOCK)
    nsp = -(-n_sel // LANE) * LANE
    ov = _overlap_matrix(ncp, n_sel, nsp)
    smem = pl.BlockSpec(memory_space=pltpu.SMEM)
    kv_block = (None, None, None, ncp, HEAD_DIM)
    return pl.pallas_call(
        functools.partial(_cmp_attn_kernel, tb=DEC_PAD, nc=ncp - 1, n_sel=n_sel, pos0=PAST_LEN, emit_idx=True),
        grid=(nb, NSA_KV_HEADS, 1),
        in_specs=[smem,
                  pl.BlockSpec((None, DEC_PAD, g_count * HEAD_DIM), lambda b, h, i: (b, 0, h)),
                  pl.BlockSpec(kv_block, lambda b, h, i: (b, 0, h, 0, 0)),
                  pl.BlockSpec(kv_block, lambda b, h, i: (b, 1, h, 0, 0)),
                  pl.BlockSpec((ncp, nsp), lambda b, h, i: (0, 0))],
        out_specs=[pl.BlockSpec((None, DEC_PAD, g_count * HEAD_DIM), lambda b, h, i: (b, 0, h)),
                   pl.BlockSpec((None, None, DEC_PAD, LANE), lambda b, h, i: (b, h, 0, 0))],
        out_shape=[jax.ShapeDtypeStruct((nb, DEC_PAD, N_HEADS * HEAD_DIM), F32),
                   jax.ShapeDtypeStruct((nb, NSA_KV_HEADS, DEC_PAD, LANE), jnp.int32)],
        compiler_params=_params(("arbitrary", "arbitrary", "arbitrary")),
        name="cmp_attention_sample",
    )(slopes, qkv8, cmp_kv, cmp_kv, ov)


SEL_KEY_TILE = 256

def _sel_prompt_kernel(slopes_ref, q_ref, k_ref, v_ref, sel_ref, ex_ref, o_ref, m_ref, l_ref, acc_ref, chosen_ref):
    kvh = pl.program_id(1)
    i = pl.program_id(2)
    g_count = N_HEADS // NSA_KV_HEADS
    tb = BAND_BLOCK
    qs = _stack_heads(q_ref[...], g_count).astype(BF16)
    sel = sel_ref[...].astype(BF16)
    slope_col = jnp.concatenate(
        [jnp.full((tb, 1), slopes_ref[kvh * g_count + g], F32) for g in range(g_count)], axis=0)
    kt_w = SEL_KEY_TILE
    m_ref[...] = jnp.full(m_ref.shape, NEG_FILL, F32)
    l_ref[...] = jnp.zeros(l_ref.shape, F32)
    acc_ref[...] = jnp.zeros(acc_ref.shape, F32)
    for kt in range(chosen_ref.shape[0]):
        chosen_ref[kt] = jnp.dot(sel, ex_ref[:, kt * kt_w:(kt + 1) * kt_w], preferred_element_type=F32)
    tq = i * tb + lax.broadcasted_iota(jnp.int32, (tb, kt_w), 0)
    kk = lax.broadcasted_iota(jnp.int32, (tb, kt_w), 1)

    def attend(k0, dist):
        ks = k_ref[pl.ds(k0, kt_w), :].astype(BF16)
        vs = v_ref[pl.ds(k0, kt_w), :].astype(BF16)
        s = lax.dot_general(qs, ks, (((1,), (1,)), ((), ())), preferred_element_type=F32) * SCALE
        distf = jnp.concatenate([dist] * g_count, axis=0)
        s = jnp.where(distf >= 0.0, s - slope_col * distf, 2 * NEG_FILL)
        m_old = m_ref[...]
        m_new = jnp.maximum(m_old, jnp.max(s, axis=-1, keepdims=True))
        alpha = jnp.exp(m_old - m_new)
        p = jnp.exp(s - jnp.concatenate([m_new] * (kt_w // LANE), axis=1))
        l_ref[...] = alpha * l_ref[...] + jnp.sum(p, axis=-1, keepdims=True)
        acc_ref[...] = alpha * acc_ref[...] + jnp.dot(p.astype(BF16), vs, preferred_element_type=F32)
        m_ref[...] = m_new

    def body(kt, carry):
        k0 = pl.multiple_of(kt * kt_w, kt_w)
        dist = (tq - (k0 + kk)).astype(F32)
        dist = jnp.where(chosen_ref[kt] > 0.5, dist, -1.0)

        @pl.when(jnp.max(dist) >= 0.0)
        def _():
            attend(k0, dist)

        return carry

    n_tiles = (i * tb + tb + kt_w - 1) // kt_w
    lax.fori_loop(0, n_tiles, body, 0)
    o = acc_ref[...] / jnp.maximum(l_ref[...], 1e-30)
    o_ref[...] = jnp.concatenate([o[g * tb:(g + 1) * tb] for g in range(g_count)], axis=1)


def sel_attention_prompt(proj, sel_mask, slopes, *, batch, seq, k_col0, v_col0):
    g_count = N_HEADS // NSA_KV_HEADS
    tb = BAND_BLOCK
    nb = seq // tb
    kb0, vb0 = k_col0 // HEAD_DIM, v_col0 // HEAD_DIM
    smem = pl.BlockSpec(memory_space=pltpu.SMEM)
    block_of_key = jnp.arange(seq, dtype=jnp.int32)[None, :] // SEL_BLOCK
    expand = (jnp.arange(LANE, dtype=jnp.int32)[:, None] == block_of_key).astype(BF16)
    return pl.pallas_call(
        _sel_prompt_kernel,
        grid=(batch, NSA_KV_HEADS, nb),
        in_specs=[smem,
                  pl.BlockSpec((tb, g_count * HEAD_DIM), lambda b, h, i: (b * nb + i, h)),
                  pl.BlockSpec((seq, HEAD_DIM), lambda b, h, i: (b, kb0 + h)),
                  pl.BlockSpec((seq, HEAD_DIM), lambda b, h, i: (b, vb0 + h)),
                  pl.BlockSpec((None, None, tb, LANE), lambda b, h, i: (b, h, i, 0)),
                  pl.BlockSpec((LANE, seq), lambda b, h, i: (0, 0))],
        out_specs=pl.BlockSpec((tb, g_count * HEAD_DIM), lambda b, h, i: (b * nb + i, h)),
        out_shape=jax.ShapeDtypeStruct((batch * seq, N_HEADS * HEAD_DIM), F32),
        scratch_shapes=[pltpu.VMEM((g_count * tb, HEAD_DIM), F32)] * 3
        + [pltpu.VMEM((seq // SEL_KEY_TILE, tb, SEL_KEY_TILE), F32)],
        compiler_params=_params(("arbitrary", "arbitrary", "arbitrary")),
        name="sel_attention_prompt",
    )(slopes, proj, proj, proj, sel_mask, expand)


def _sel_sample_kernel(idx_ref, pt_ref, q_ref, *refs, n_past_blocks):
    blocks = refs[:SEL_TOPN]
    kn_ref, vn_ref, slope_ref, o_ref = refs[SEL_TOPN:]
    b, kvh, t = pl.program_id(0), pl.program_id(1), pl.program_id(2)
    base = ((b * NSA_KV_HEADS + kvh) * DEC_SEQ + t) * SEL_TOPN
    g_count = N_HEADS // NSA_KV_HEADS
    pad = jnp.zeros((LANE - DEC_PAD, HEAD_DIM), F32)
    kall = jnp.concatenate([r[:, kvh, :] for r in blocks] + [kn_ref[...], pad], axis=0).astype(BF16)
    vall = jnp.concatenate([r[:, NSA_KV_HEADS + kvh, :] for r in blocks] + [vn_ref[...], pad], axis=0).astype(BF16)
    q = q_ref[...].astype(BF16)
    s = lax.dot_general(q, kall, (((1,), (1,)), ((), ())), preferred_element_type=F32) * SCALE
    lane = lax.broadcasted_iota(jnp.int32, (g_count, LANE), 1)
    low = lane < SEL_BLOCK
    qpos = PAST_LEN + t
    pos_parts, ok_parts = [], []
    for c in range(SEL_TOPN // 2):
        b0 = idx_ref[base + 2 * c]
        b1 = idx_ref[base + 2 * c + 1]
        p0 = jnp.where(b0 < n_past_blocks, b0 * SEL_BLOCK, PAST_LEN + DEC_SEQ)
        p1 = jnp.where(b1 < n_past_blocks, b1 * SEL_BLOCK, PAST_LEN + DEC_SEQ)
        pos_parts.append(jnp.where(low, p0 + lane, p1 + (lane - SEL_BLOCK)))
    pos_parts.append(PAST_LEN + lane)
    dist = qpos - jnp.concatenate(pos_parts, axis=1)
    valid = dist >= 0
    slope = jnp.concatenate([slope_ref[...]] * (SEL_TOPN // 2 + 1), axis=1)
    o_ref[...] = _softmax_pv(s - slope * dist.astype(F32), valid, vall)


def sel_attention_sample(q_s, pool_halves, page_table, sel_idx, new_kv8, slope_rows, layer_half0):
    nb = q_s.shape[0]
    g_count = N_HEADS // NSA_KV_HEADS
    n_past_blocks = PAST_LEN // SEL_BLOCK
    per_page = PAGE_SIZE // SEL_BLOCK
    page_shift = per_page.bit_length() - 1
    assert per_page == 1 << page_shift

    def blk_spec(r):
        def index(b, h, t, idx, pt):
            blk = jnp.minimum(idx[((b * NSA_KV_HEADS + h) * DEC_SEQ + t) * SEL_TOPN + r], n_past_blocks - 1)
            page = pt[b, lax.shift_right_logical(blk, page_shift)]
            return (layer_half0 + page * per_page + jnp.bitwise_and(blk, per_page - 1), 0, 1, 0)
        return pl.BlockSpec((None, SEL_BLOCK, CMP_HEADS, HEAD_DIM), index)

    q_block = (None, None, None, g_count, HEAD_DIM)
    new_block = (None, None, None, DEC_PAD, HEAD_DIM)
    grid_spec = pltpu.PrefetchScalarGridSpec(
        num_scalar_prefetch=2,
        grid=(nb, NSA_KV_HEADS, DEC_SEQ),
        in_specs=[pl.BlockSpec(q_block, lambda b, h, t, idx, pt: (b, t, h, 0, 0))]
        + [blk_spec(r) for r in range(SEL_TOPN)]
        + [pl.BlockSpec(new_block, lambda b, h, t, idx, pt: (b, 2, h, 0, 0)),
           pl.BlockSpec(new_block, lambda b, h, t, idx, pt: (b, 3, h, 0, 0)),
           pl.BlockSpec((None, g_count, LANE), lambda b, h, t, idx, pt: (h, 0, 0))],
        out_specs=pl.BlockSpec(q_block, lambda b, h, t, idx, pt: (b, t, h, 0, 0)),
    )
    return pl.pallas_call(
        functools.partial(_sel_sample_kernel, n_past_blocks=n_past_blocks),
        grid_spec=grid_spec,
        out_shape=jax.ShapeDtypeStruct(q_s.shape, F32),
        compiler_params=_params(("arbitrary", "arbitrary", "arbitrary")),
        name="sel_attention_sample",
    )(sel_idx, page_table, q_s, *([pool_halves] * SEL_TOPN), new_kv8, new_kv8, slope_rows)


def _combine_kernel(oc_ref, os_ref, ow_ref, gl_ref, gb_ref, o_ref):
    gate = jax.nn.sigmoid(gl_ref[...] + gb_ref[...])
    for h in range(N_HEADS):
        sl = slice(h * HEAD_DIM, (h + 1) * HEAD_DIM)
        o = (gate[:, h:h + 1] * oc_ref[:, sl] + gate[:, N_HEADS + h:N_HEADS + h + 1] * os_ref[:, sl]
             + gate[:, 2 * N_HEADS + h:2 * N_HEADS + h + 1] * ow_ref[:, sl])
        o_ref[:, sl] = o.astype(o_ref.dtype)


def gate_combine(o_c, o_s, o_w, gate_logits, gate_bias, tm, row_block0=0):
    m, n = o_c.shape
    big = pl.BlockSpec((tm, n), lambda i: (i, 0))
    return pl.pallas_call(
        _combine_kernel,
        grid=(m // tm,),
        in_specs=[big, big, big,
                  pl.BlockSpec((tm, LANE), lambda i: (row_block0 + i, 0)),
                  pl.BlockSpec((1, LANE), lambda i: (0, 0))],
        out_specs=big,
        out_shape=jax.ShapeDtypeStruct((m, n), BF16),
        compiler_params=_params(("arbitrary",)),
        name="gate_combine",
    )(o_c, o_s, o_w, gate_logits, gate_bias)


TM_DENSE = 2080
TN_PROJ = 256
TF_FFN = 256
TM_FFN_OUT = 1040
TN_FFN_OUT = 256
TM_NORM = 320
TM_PROMPT = 256
NQ = N_HEADS * HEAD_DIM


def _ffn_block(x, g, w_in, w_out_bf, layer, sub):
    xn = rms_norm(x, g, BF16, TM_NORM)
    h = ffn_in(xn, w_in, layer, sub, TM_DENSE, TF_FFN)
    return ffn_out(h, w_out_bf, layer, sub, x, TM_FFN_OUT, TN_FFN_OUT)


def _sample_rows8(proj):
    s = proj[M_PROMPT:].reshape(DEC_BATCH, DEC_SEQ, proj.shape[1])
    return jnp.pad(s, ((0, 0), (0, DEC_PAD - DEC_SEQ), (0, 0)))


def _swa_layer(x, g, w_in, w_out, sinks_all, cache_all, li, slopes):
    xn = rms_norm(x, g, BF16, TM_NORM)
    kv_cols = 2 * SWA_KV_HEADS * HEAD_DIM
    proj = matmul_wres(xn, w_in, (li,), NQ + kv_cols, TM_DENSE, TN_PROJ)
    sinks = sinks_all[li].astype(F32)
    o_p = band_attention(proj, slopes, sinks, batch=BATCH, seq=SEQ, kv_heads=SWA_KV_HEADS, k_col0=NQ,
                         v_col0=NQ + SWA_KV_HEADS * HEAD_DIM, window=SWA_WINDOW, use_sink=True, out_dtype=BF16)
    cache = cache_all[li]
    lc = cache.shape[1]
    o_s = decode_attention(_sample_rows8(proj), cache.reshape(DEC_BATCH, lc, kv_cols), slopes, sinks,
                           kv_heads=SWA_KV_HEADS, k_col0=NQ, v_col0=NQ + SWA_KV_HEADS * HEAD_DIM,
                           window=SWA_WINDOW, use_sink=True, out_dtype=BF16)
    attn = jnp.concatenate([o_p, o_s[:, :DEC_SEQ].reshape(M_SAMPLE, NQ)], axis=0)
    x = matmul_wres(attn, w_out, (li,), D_MODEL, TM_DENSE, TN_PROJ, res=x)
    kv_p = proj[:M_PROMPT, NQ:].reshape(BATCH, SEQ, 2, SWA_KV_HEADS, HEAD_DIM)
    kv_s = proj[M_PROMPT:, NQ:].reshape(DEC_BATCH, DEC_SEQ, 2, SWA_KV_HEADS, HEAD_DIM)
    buf_p = kv_p[:, SEQ - min(SWA_WINDOW, SEQ):]
    buf_s = jnp.concatenate([cache, kv_s], axis=1)[:, DEC_SEQ:]
    return x, buf_p, buf_s


def _nsa_layer(x, g, w_in, gate_b, cmp_pe, cmp_w1, cmp_w2, w_out, win_cache_all, pool_all, page_table, li, slopes):
    xn = rms_norm(x, g, BF16, TM_NORM)
    kvw = NSA_KV_HEADS * HEAD_DIM
    main_cols = NQ + 6 * kvw
    proj = matmul_wres(xn, w_in, (li,), main_cols, TM_DENSE, TN_PROJ)
    n_gate = 3 * N_HEADS
    w_gate = jnp.pad(w_in[li][:, main_cols:], ((0, 0), (0, LANE - n_gate)))
    gate_logits = matmul_wres(xn, w_gate, (), LANE, TM_DENSE, LANE)
    gate_bias = jnp.pad(gate_b[li], (0, LANE - n_gate)).reshape(1, LANE)
    zeros = jnp.zeros((N_HEADS,), F32)

    pet = pe_term(cmp_pe, cmp_w1, li)
    w1 = cmp_w1[li].astype(BF16)
    top = jnp.concatenate([w1[:, 0:CMP_STRIDE:2], w1[:, CMP_STRIDE::2]], axis=-1)
    bot = jnp.concatenate([w1[:, 1:CMP_STRIDE:2], w1[:, CMP_STRIDE + 1::2]], axis=-1)
    w1_pairs = jnp.concatenate([top, bot], axis=-2)

    o_w_p = band_attention(proj, slopes, zeros, batch=BATCH, seq=SEQ, kv_heads=NSA_KV_HEADS, k_col0=NQ + 4 * kvw,
                           v_col0=NQ + 5 * kvw, window=NSA_WINDOW, use_sink=False, out_dtype=F32)
    cmp_p = compress_prompt(proj, cmp_w1, cmp_w2, pet, li, batch=BATCH, seq=SEQ, col0=NQ)
    o_c_p, sel_mask = cmp_attention_prompt(proj, cmp_p, slopes, batch=BATCH, seq=SEQ)
    o_s_p = sel_attention_prompt(proj, sel_mask, slopes, batch=BATCH, seq=SEQ, k_col0=NQ + 2 * kvw,
                                 v_col0=NQ + 3 * kvw)
    attn_p = gate_combine(o_c_p, o_s_p, o_w_p, gate_logits, gate_bias, TM_PROMPT)

    qkv8 = _sample_rows8(proj)
    win_cache = win_cache_all[li]
    lc = win_cache.shape[1]
    o_w_s = decode_attention(qkv8, win_cache.reshape(DEC_BATCH, lc, 2 * kvw), slopes, zeros,
                             kv_heads=NSA_KV_HEADS, k_col0=NQ + 4 * kvw, v_col0=NQ + 5 * kvw, window=NSA_WINDOW,
                             use_sink=False, out_dtype=F32)
    n_pool = pool_all.shape[1]
    pool = pool_all.reshape(pool_all.shape[0] * n_pool, PAGE_SIZE, 4 * kvw)
    pool4 = pool.reshape(pool.shape[0], PAGE_SIZE, 2 * CMP_HEADS, HEAD_DIM)
    part_a, part_b = compress_pages_partial(pool4, page_table, w1_pairs, li * n_pool)
    cmp_s = compress_pages_finish(part_a, part_b, cmp_w2, pet, li)
    o_c_s, sel_idx = cmp_attention_sample(qkv8, cmp_s, slopes, lk=PAST_LEN + DEC_SEQ)
    sel_flat = sel_idx[:, :, :DEC_SEQ, :SEL_TOPN].reshape(-1)
    rows_s = proj[M_PROMPT:, NQ:NQ + 4 * kvw].reshape(DEC_BATCH, DEC_SEQ, 4, NSA_KV_HEADS, HEAD_DIM)
    new_kv8 = jnp.pad(rows_s.transpose(0, 2, 3, 1, 4), ((0, 0), (0, 0), (0, 0), (0, DEC_PAD - DEC_SEQ), (0, 0)))
    g_count = N_HEADS // NSA_KV_HEADS
    q_s = proj[M_PROMPT:, :NQ].reshape(DEC_BATCH, DEC_SEQ, NSA_KV_HEADS, g_count, HEAD_DIM)
    slope_rows = jnp.broadcast_to(slopes.reshape(NSA_KV_HEADS, g_count, 1), (NSA_KV_HEADS, g_count, LANE))
    per_page = PAGE_SIZE // SEL_BLOCK
    pool_halves = pool.reshape(pool.shape[0] * per_page, SEL_BLOCK, 2 * CMP_HEADS, HEAD_DIM)
    o_s_s = sel_attention_sample(q_s, pool_halves, page_table, sel_flat, new_kv8, slope_rows, li * n_pool * per_page)
    attn_s = gate_combine(o_c_s[:, :DEC_SEQ].reshape(M_SAMPLE, NQ), o_s_s.reshape(M_SAMPLE, NQ),
                          o_w_s[:, :DEC_SEQ].reshape(M_SAMPLE, NQ), gate_logits, gate_bias, M_SAMPLE,
                          row_block0=M_PROMPT // M_SAMPLE)

    attn = jnp.concatenate([attn_p, attn_s], axis=0)
    x = matmul_wres(attn, w_out, (li,), D_MODEL, TM_DENSE, TN_PROJ, res=x)

    kv_p = proj[:M_PROMPT, NQ:main_cols].reshape(BATCH, SEQ, 6, NSA_KV_HEADS, HEAD_DIM)
    kv_s = proj[M_PROMPT:, NQ:main_cols].reshape(DEC_BATCH, DEC_SEQ, 6, NSA_KV_HEADS, HEAD_DIM)
    win_p = kv_p[:, SEQ - min(NSA_WINDOW, SEQ):, 4:]
    win_s = jnp.concatenate([win_cache, kv_s[:, :, 4:]], axis=1)[:, DEC_SEQ:]
    return x, kv_p[:, :, :4], kv_s[:, :, :4], win_p, win_s


def kernel(x_prompt, x_sample, cache_swa_kv, cache_nsa_win_kv, cache_nsa_kv, page_table, norm_g, final_norm_g,
           ffn_w_in, ffn_w_out, swa_w_in, swa_w_out, swa_sinks, nsa_w_in, nsa_gate_b, nsa_cmp_pe, nsa_cmp_w1,
           nsa_cmp_w2, nsa_w_out):
    x = jnp.concatenate([x_prompt.reshape(M_PROMPT, D_MODEL), x_sample.reshape(M_SAMPLE, D_MODEL)], axis=0)
    slopes = jnp.exp2(-8.0 * jnp.arange(1, N_HEADS + 1, dtype=F32) / N_HEADS)
    w_out_bf = ffn_w_out.astype(BF16)
    swa_p, swa_s, win_p, win_s, kv_p, kv_s = [], [], [], [], [], []
    for i in range(DEPTH):
        x = _ffn_block(x, norm_g[i, 0], ffn_w_in, w_out_bf, i, 0)
        li = i // N_MIXERS
        if i % N_MIXERS == 0:
            x, bp, bs = _swa_layer(x, norm_g[i, 1], swa_w_in, swa_w_out, swa_sinks, cache_swa_kv, li, slopes)
            swa_p.append(bp)
            swa_s.append(bs)
        else:
            x, rp, rs, wp, ws = _nsa_layer(x, norm_g[i, 1], nsa_w_in, nsa_gate_b, nsa_cmp_pe, nsa_cmp_w1,
                                           nsa_cmp_w2, nsa_w_out, cache_nsa_win_kv, cache_nsa_kv, page_table, li,
                                           slopes)
            kv_p.append(rp)
            kv_s.append(rs)
            win_p.append(wp)
            win_s.append(ws)
        x = _ffn_block(x, norm_g[i, 2], ffn_w_in, w_out_bf, i, 1)
    y_p = rms_norm(x, final_norm_g, F32, TM_PROMPT, rows=M_PROMPT).reshape(BATCH, SEQ, D_MODEL)
    y_s = rms_norm(x, final_norm_g, F32, M_SAMPLE, row_block0=M_PROMPT // M_SAMPLE, rows=M_SAMPLE)
    return (y_p, y_s.reshape(DEC_BATCH, DEC_SEQ, D_MODEL), jnp.stack(swa_p), jnp.stack(swa_s), jnp.stack(win_p),
            jnp.stack(win_s), jnp.stack(kv_p), jnp.stack(kv_s))
```

```python
import functools
import math

import jax
import jax.numpy as jnp
from jax import lax
from jax.experimental import pallas as pl
from jax.experimental.pallas import tpu as pltpu

D_MODEL = 4096
BATCH = 4
SEQ = 2048
DEPTH = 2
DEC_BATCH = 32
DEC_SEQ = 4
PAST_LEN = 16384
PAGE_SIZE = 128
N_HEADS = 32
HEAD_DIM = 128
SWA_KV_HEADS = 8
SWA_WINDOW = 128
NSA_KV_HEADS = 4
CMP_STRIDE = 16
CMP_LEN = 32
SEL_BLOCK = 64
SEL_TOPN = 16
NSA_WINDOW = 512
D_FF = 11008
BAND_BLOCK = 128
N_MIXERS = 2
RMS_EPS = 1e-6
NEG_FILL = -1e30
SCALE = HEAD_DIM ** -0.5

M_PROMPT = BATCH * SEQ
M_SAMPLE = DEC_BATCH * DEC_SEQ
M_ALL = M_PROMPT + M_SAMPLE
DEC_PAD = 8
LANE = 128
F32 = jnp.float32
BF16 = jnp.bfloat16
VMEM_LIMIT = 60 * 1024 * 1024


def _params(sem):
    return pltpu.CompilerParams(dimension_semantics=sem, vmem_limit_bytes=VMEM_LIMIT)


def _rms_kernel(x_ref, g_ref, o_ref):
    x = x_ref[...]
    y = x * lax.rsqrt(jnp.mean(x * x, axis=-1, keepdims=True) + RMS_EPS)
    o_ref[...] = (y * g_ref[...]).astype(o_ref.dtype)


def rms_norm(x, g, out_dtype, tm, row_block0=0, rows=None):
    m, d = x.shape
    rows = m if rows is None else rows
    return pl.pallas_call(
        _rms_kernel,
        grid=(rows // tm,),
        in_specs=[pl.BlockSpec((tm, d), lambda i: (row_block0 + i, 0)),
                  pl.BlockSpec((1, d), lambda i: (0, 0))],
        out_specs=pl.BlockSpec((tm, d), lambda i: (i, 0)),
        out_shape=jax.ShapeDtypeStruct((rows, d), out_dtype),
        compiler_params=_params(("arbitrary",)),
        name="rms_norm",
    )(x, g.reshape(1, d))


def _row_tile_spec(tm, k):
    return pl.BlockSpec((tm, k), lambda i, j: (i, 0), pipeline_mode=pl.Buffered(1))


def _mm_kernel(x_ref, w_ref, *rest, scale, has_res):
    acc = jnp.dot(x_ref[...], w_ref[...].astype(BF16), preferred_element_type=F32)
    if has_res:
        res_ref, o_ref = rest
        o_ref[...] = res_ref[...] + (acc if scale == 1.0 else scale * acc)
    else:
        (o_ref,) = rest
        o_ref[...] = acc.astype(o_ref.dtype)


def matmul_wres(x, w, w_prefix, n_cols, tm, tn, res=None, scale=1.0, col_block0=0):
    m, k = x.shape
    npre = len(w_prefix)
    w_block = (None,) * npre + (k, tn)
    in_specs = [_row_tile_spec(tm, k),
                pl.BlockSpec(w_block, lambda i, j: tuple(w_prefix) + (0, col_block0 + j))]
    args = [x, w]
    if res is not None:
        in_specs.append(pl.BlockSpec((tm, tn), lambda i, j: (i, j)))
        args.append(res)
    return pl.pallas_call(
        functools.partial(_mm_kernel, scale=scale, has_res=res is not None),
        grid=(m // tm, n_cols // tn),
        in_specs=in_specs,
        out_specs=pl.BlockSpec((tm, tn), lambda i, j: (i, j)),
        out_shape=jax.ShapeDtypeStruct((m, n_cols), F32),
        compiler_params=_params(("arbitrary", "arbitrary")),
        name="matmul_wres",
    )(*args)


def _ffn_in_kernel(x_ref, wg_ref, wu_ref, o_ref):
    x = x_ref[...]
    g = jnp.dot(x, wg_ref[...].astype(BF16), preferred_element_type=F32)
    u = jnp.dot(x, wu_ref[...].astype(BF16), preferred_element_type=F32)
    o_ref[...] = (jax.nn.silu(g) * u).astype(o_ref.dtype)


def ffn_in(xn, w_in, layer, sub, tm, tf):
    m, k = xn.shape
    nf = D_FF // tf
    w_block = (None, None, k, tf)
    return pl.pallas_call(
        _ffn_in_kernel,
        grid=(m // tm, nf),
        in_specs=[_row_tile_spec(tm, k),
                  pl.BlockSpec(w_block, lambda i, j: (layer, sub, 0, j)),
                  pl.BlockSpec(w_block, lambda i, j: (layer, sub, 0, nf + j))],
        out_specs=pl.BlockSpec((tm, tf), lambda i, j: (i, j)),
        out_shape=jax.ShapeDtypeStruct((m, D_FF), BF16),
        compiler_params=_params(("arbitrary", "arbitrary")),
        name="ffn_in",
    )(xn, w_in, w_in)


def _ffn_out_kernel(h_ref, w_ref, res_ref, o_ref):
    acc = jnp.dot(h_ref[...], w_ref[...], preferred_element_type=F32)
    o_ref[...] = res_ref[...] + 0.5 * acc


def ffn_out(h, w_out_bf, layer, sub, res, tm, tn):
    m, k = h.shape
    n = res.shape[1]
    return pl.pallas_call(
        _ffn_out_kernel,
        grid=(m // tm, n // tn),
        in_specs=[_row_tile_spec(tm, k),
                  pl.BlockSpec((None, None, k, tn), lambda i, j: (layer, sub, 0, j)),
                  pl.BlockSpec((tm, tn), lambda i, j: (i, j))],
        out_specs=pl.BlockSpec((tm, tn), lambda i, j: (i, j)),
        out_shape=jax.ShapeDtypeStruct((m, n), F32),
        compiler_params=_params(("arbitrary", "arbitrary")),
        name="ffn_out",
    )(h, w_out_bf, res)


def _stack_heads(q, g_count):
    return jnp.concatenate([q[:, g * HEAD_DIM:(g + 1) * HEAD_DIM] for g in range(g_count)], axis=0)


def _softmax_pv(s, valid, v_bf, sink=None):
    s = jnp.where(valid, s, 2 * NEG_FILL)
    m = jnp.maximum(jnp.max(s, axis=-1, keepdims=True), NEG_FILL)
    if sink is not None:
        m = jnp.maximum(m, sink)
    p = jnp.exp(s - m)
    den = jnp.sum(p, axis=-1, keepdims=True)
    if sink is not None:
        den = den + jnp.exp(sink - m)
    o = jnp.dot(p.astype(BF16), v_bf, preferred_element_type=F32)
    return o / jnp.maximum(den, 1e-30)


def _band_kernel(slopes_ref, sinks_ref, q_ref, k_ref, v_ref, o_ref, *, g_count, nprev, window, use_sink):
    kvh = pl.program_id(1)
    i = pl.program_id(2)
    width = (nprev + 1) * BAND_BLOCK
    start = pl.multiple_of(jnp.maximum(i - nprev, 0) * BAND_BLOCK, BAND_BLOCK)
    k = k_ref[pl.ds(start, width), :].astype(BF16)
    v = v_ref[pl.ds(start, width), :].astype(BF16)
    qs = _stack_heads(q_ref[...], g_count).astype(BF16)
    s = lax.dot_general(qs, k, (((1,), (1,)), ((), ())), preferred_element_type=F32) * SCALE
    tq = i * BAND_BLOCK + lax.broadcasted_iota(jnp.int32, (BAND_BLOCK, width), 0)
    kp = start + lax.broadcasted_iota(jnp.int32, (BAND_BLOCK, width), 1)
    dist = tq - kp
    valid = (dist >= 0) & (dist <= window)
    distf = dist.astype(F32)
    outs = []
    for g in range(g_count):
        h = kvh * g_count + g
        sg = s[g * BAND_BLOCK:(g + 1) * BAND_BLOCK] - slopes_ref[h] * distf
        outs.append(_softmax_pv(sg, valid, v, sinks_ref[h] if use_sink else None))
    o_ref[...] = jnp.concatenate(outs, axis=1).astype(o_ref.dtype)


def band_attention(proj, slopes, sinks, *, batch, seq, kv_heads, k_col0, v_col0, window, use_sink, out_dtype):
    g_count = N_HEADS // kv_heads
    nb = seq // BAND_BLOCK
    nprev = -(-window // BAND_BLOCK)
    kb0, vb0 = k_col0 // HEAD_DIM, v_col0 // HEAD_DIM
    smem = pl.BlockSpec(memory_space=pltpu.SMEM)
    return pl.pallas_call(
        functools.partial(_band_kernel, g_count=g_count, nprev=nprev, window=window, use_sink=use_sink),
        grid=(batch, kv_heads, nb),
        in_specs=[smem, smem,
                  pl.BlockSpec((BAND_BLOCK, g_count * HEAD_DIM), lambda b, h, i: (b * nb + i, h)),
                  pl.BlockSpec((seq, HEAD_DIM), lambda b, h, i: (b, kb0 + h)),
                  pl.BlockSpec((seq, HEAD_DIM), lambda b, h, i: (b, vb0 + h))],
        out_specs=pl.BlockSpec((BAND_BLOCK, g_count * HEAD_DIM), lambda b, h, i: (b * nb + i, h)),
        out_shape=jax.ShapeDtypeStruct((batch * seq, N_HEADS * HEAD_DIM), out_dtype),
        compiler_params=_params(("arbitrary", "arbitrary", "arbitrary")),
        name="band_attention",
    )(slopes, sinks, proj, proj, proj)


def _dec_kernel(slopes_ref, sinks_ref, qkv_ref, c_ref, o_ref, *, kv_heads, g_count, lc, k_col0, v_col0,
                window, use_sink):
    rows = g_count * DEC_PAD
    width = lc + LANE
    t = lax.broadcasted_iota(jnp.int32, (DEC_PAD, width), 0)
    col = lax.broadcasted_iota(jnp.int32, (DEC_PAD, width), 1)
    dist = jnp.where(col < lc, t + (lc - col), t - (col - lc))
    valid = (dist >= 0) & (dist <= window) & (col < lc + DEC_SEQ)
    distf = dist.astype(F32)
    pad = jnp.zeros((LANE - DEC_PAD, HEAD_DIM), F32)
    for kvh in range(kv_heads):
        q0 = kvh * g_count * HEAD_DIM
        qs = _stack_heads(qkv_ref[:, q0:q0 + g_count * HEAD_DIM], g_count).astype(BF16)
        kn = qkv_ref[:, k_col0 + kvh * HEAD_DIM:k_col0 + (kvh + 1) * HEAD_DIM]
        vn = qkv_ref[:, v_col0 + kvh * HEAD_DIM:v_col0 + (kvh + 1) * HEAD_DIM]
        kc = c_ref[pl.ds(kvh, lc, stride=2 * kv_heads), :]
        vc = c_ref[pl.ds(kv_heads + kvh, lc, stride=2 * kv_heads), :]
        kall = jnp.concatenate([kc, kn, pad], axis=0).astype(BF16)
        vall = jnp.concatenate([vc, vn, pad], axis=0).astype(BF16)
        s = lax.dot_general(qs, kall, (((1,), (1,)), ((), ())), preferred_element_type=F32) * SCALE
        assert s.shape == (rows, width)
        for g in range(g_count):
            h = kvh * g_count + g
            sg = s[g * DEC_PAD:(g + 1) * DEC_PAD] - slopes_ref[h] * distf
            o = _softmax_pv(sg, valid, vall, sinks_ref[h] if use_sink else None)
            o_ref[:, h * HEAD_DIM:(h + 1) * HEAD_DIM] = o.astype(o_ref.dtype)


def decode_attention(qkv8, cache, slopes, sinks, *, kv_heads, k_col0, v_col0, window, use_sink, out_dtype):
    nb, _, ncols = qkv8.shape
    lc = cache.shape[1] // (2 * kv_heads)
    g_count = N_HEADS // kv_heads
    smem = pl.BlockSpec(memory_space=pltpu.SMEM)
    return pl.pallas_call(
        functools.partial(_dec_kernel, kv_heads=kv_heads, g_count=g_count, lc=lc, k_col0=k_col0, v_col0=v_col0,
                          window=window, use_sink=use_sink),
        grid=(nb,),
        in_specs=[smem, smem,
                  pl.BlockSpec((None, DEC_PAD, ncols), lambda b: (b, 0, 0)),
                  pl.BlockSpec((None, cache.shape[1], HEAD_DIM), lambda b: (b, 0, 0))],
        out_specs=pl.BlockSpec((None, DEC_PAD, N_HEADS * HEAD_DIM), lambda b: (b, 0, 0)),
        out_shape=jax.ShapeDtypeStruct((nb, DEC_PAD, N_HEADS * HEAD_DIM), out_dtype),
        compiler_params=_params(("arbitrary",)),
        name="decode_attention",
    )(slopes, sinks, qkv8, cache)


def _pe_term_kernel(pe_ref, w1_ref, o_ref):
    acc = jnp.zeros((8, HEAD_DIM), F32)
    for l in range(CMP_LEN):
        row = jnp.broadcast_to(pe_ref[l:l + 1, :], (8, HEAD_DIM)).astype(BF16)
        acc = acc + jnp.dot(row, w1_ref[l].astype(BF16), preferred_element_type=F32)
    o_ref[...] = acc


def pe_term(pe, w1, layer):
    return pl.pallas_call(
        _pe_term_kernel,
        grid=(2,),
        in_specs=[pl.BlockSpec((None, None, CMP_LEN, HEAD_DIM), lambda s: (layer, s, 0, 0)),
                  pl.BlockSpec((None, None, CMP_LEN, HEAD_DIM, HEAD_DIM), lambda s: (layer, s, 0, 0, 0))],
        out_specs=pl.BlockSpec((None, 8, HEAD_DIM), lambda s: (s, 0, 0)),
        out_shape=jax.ShapeDtypeStruct((2, 8, HEAD_DIM), F32),
        compiler_params=_params(("arbitrary",)),
        name="pe_term",
    )(pe, w1)


def _compress_finish(acc_a, acc_b, pet_row, w2_bf):
    n = acc_a.shape[0]
    pre = acc_a + pltpu.roll(acc_b, n - 1, 0) + pet_row
    return jnp.dot(jax.nn.gelu(pre).astype(BF16), w2_bf, preferred_element_type=F32)


def _cmp_prompt_kernel(x_ref, w1_ref, w2_ref, pet_ref, o_ref, *, nch):
    acc_a = jnp.zeros((nch, HEAD_DIM), F32)
    acc_b = jnp.zeros((nch, HEAD_DIM), F32)
    for l in range(CMP_STRIDE):
        xl = x_ref[pl.ds(l, nch, stride=CMP_STRIDE), :].astype(BF16)
        acc_a = acc_a + jnp.dot(xl, w1_ref[l].astype(BF16), preferred_element_type=F32)
        acc_b = acc_b + jnp.dot(xl, w1_ref[CMP_STRIDE + l].astype(BF16), preferred_element_type=F32)
    o_ref[...] = _compress_finish(acc_a, acc_b, pet_ref[0:1, :], w2_ref[...].astype(BF16))


def compress_prompt(proj, w1, w2, pet, layer, *, batch, seq, col0):
    nch = seq // CMP_STRIDE
    cb0 = col0 // HEAD_DIM
    return pl.pallas_call(
        functools.partial(_cmp_prompt_kernel, nch=nch),
        grid=(batch, 2, NSA_KV_HEADS),
        in_specs=[pl.BlockSpec((seq, HEAD_DIM), lambda b, s, h: (b, cb0 + s * NSA_KV_HEADS + h)),
                  pl.BlockSpec((None, None, CMP_LEN, HEAD_DIM, HEAD_DIM), lambda b, s, h: (layer, s, 0, 0, 0)),
                  pl.BlockSpec((None, None, HEAD_DIM, HEAD_DIM), lambda b, s, h: (layer, s, 0, 0)),
                  pl.BlockSpec((None, 8, HEAD_DIM), lambda b, s, h: (s, 0, 0))],
        out_specs=pl.BlockSpec((None, None, None, nch, HEAD_DIM), lambda b, s, h: (b, s, h, 0, 0)),
        out_shape=jax.ShapeDtypeStruct((batch, 2, NSA_KV_HEADS, nch, HEAD_DIM), F32),
        compiler_params=_params(("arbitrary", "arbitrary", "arbitrary")),
        name="compress_prompt",
    )(proj, w1, w2, pet)


CMP_PAGES = 16
CHUNKS_PER_PAGE = PAGE_SIZE // CMP_STRIDE
CMP_HEADS = 2 * NSA_KV_HEADS
CMP_SLOT_ROWS = CMP_PAGES * CHUNKS_PER_PAGE * NSA_KV_HEADS


def _cmp_pages_kernel(pt_ref, *refs):
    pages = refs[:CMP_PAGES]
    w_ref, a_ref, b_ref = refs[CMP_PAGES:]
    low = lax.broadcasted_iota(jnp.int32, (CMP_HEADS, HEAD_DIM), 0) < NSA_KV_HEADS
    acc = [jnp.zeros((CMP_SLOT_ROWS, 2 * HEAD_DIM), F32) for _ in range(2)]
    for lp in range(CMP_STRIDE // 2):
        halves = ([], [])
        for l in (2 * lp, 2 * lp + 1):
            tiles = ([], [])
            for r in range(CMP_PAGES):
                for c in range(0, CHUNKS_PER_PAGE, 2):
                    even = pages[r][l + CMP_STRIDE * c]
                    odd = pages[r][l + CMP_STRIDE * (c + 1)]
                    tiles[0].append(jnp.where(low, even, pltpu.roll(odd, NSA_KV_HEADS, 0)))
                    tiles[1].append(jnp.where(low, pltpu.roll(even, NSA_KV_HEADS, 0), odd))
            for s in range(2):
                halves[s].append(jnp.concatenate(tiles[s], axis=0))
        for s in range(2):
            lhs = jnp.concatenate(halves[s], axis=1).astype(BF16)
            acc[s] = acc[s] + jnp.dot(lhs, w_ref[s, lp], preferred_element_type=F32)
    for s in range(2):
        a_ref[s] = acc[s][:, :HEAD_DIM]
        b_ref[s] = acc[s][:, HEAD_DIM:]


def compress_pages_partial(pool4, page_table, w1_pairs, layer_page0):
    nb, n_pages = page_table.shape
    n_groups = n_pages // CMP_PAGES

    def page_spec(r):
        return pl.BlockSpec((None, PAGE_SIZE, CMP_HEADS, HEAD_DIM),
                            lambda b, j, pt: (layer_page0 + pt[b, j * CMP_PAGES + r], 0, 0, 0))

    out_spec = pl.BlockSpec((None, 2, CMP_SLOT_ROWS, HEAD_DIM), lambda b, j, pt: (b, 0, j, 0))
    out_shape = jax.ShapeDtypeStruct((nb, 2, n_groups * CMP_SLOT_ROWS, HEAD_DIM), F32)
    grid_spec = pltpu.PrefetchScalarGridSpec(
        num_scalar_prefetch=1,
        grid=(nb, n_groups),
        in_specs=[page_spec(r) for r in range(CMP_PAGES)]
        + [pl.BlockSpec(w1_pairs.shape, lambda b, j, pt: (0, 0, 0, 0))],
        out_specs=[out_spec, out_spec],
    )
    return pl.pallas_call(
        _cmp_pages_kernel,
        grid_spec=grid_spec,
        out_shape=[out_shape, out_shape],
        compiler_params=_params(("arbitrary", "arbitrary")),
        name="compress_pages_partial",
    )(page_table, *([pool4] * CMP_PAGES), w1_pairs)


def _cmp_finish_kernel(a_ref, b_ref, w2_ref, pet_ref, o_ref, scr_ref):
    rows = a_ref.shape[0]
    b_next = pltpu.roll(b_ref[...], rows - NSA_KV_HEADS, 0)
    h = jax.nn.gelu(a_ref[...] + b_next + pet_ref[0:1, :])
    res = jnp.dot(h.astype(BF16), w2_ref[...].astype(BF16), preferred_element_type=F32)
    row = lax.broadcasted_iota(jnp.int32, (rows, HEAD_DIM), 0)
    scr_ref[...] = jnp.where(row >= rows - NSA_KV_HEADS, 0.0, res)
    for head in range(NSA_KV_HEADS):
        o_ref[head] = scr_ref[pl.ds(head, rows // NSA_KV_HEADS, stride=NSA_KV_HEADS), :]


def compress_pages_finish(a, b, w2, pet, layer):
    nb, _, rows, _ = a.shape
    blk = (None, None, rows, HEAD_DIM)
    return pl.pallas_call(
        _cmp_finish_kernel,
        grid=(nb, 2),
        in_specs=[pl.BlockSpec(blk, lambda b_, s: (b_, s, 0, 0)),
                  pl.BlockSpec(blk, lambda b_, s: (b_, s, 0, 0)),
                  pl.BlockSpec((None, None, HEAD_DIM, HEAD_DIM), lambda b_, s: (layer, s, 0, 0)),
                  pl.BlockSpec((None, 8, HEAD_DIM), lambda b_, s: (s, 0, 0))],
        out_specs=pl.BlockSpec((None, None, NSA_KV_HEADS, rows // NSA_KV_HEADS, HEAD_DIM),
                               lambda b_, s: (b_, s, 0, 0, 0)),
        out_shape=jax.ShapeDtypeStruct((nb, 2, NSA_KV_HEADS, rows // NSA_KV_HEADS, HEAD_DIM), F32),
        scratch_shapes=[pltpu.VMEM((rows, HEAD_DIM), F32)],
        compiler_params=_params(("arbitrary", "arbitrary")),
        name="compress_pages_finish",
    )(a, b, w2, pet)


def _split3(x):
    hi = x.astype(BF16)
    r1 = x - hi.astype(F32)
    mid = r1.astype(BF16)
    lo = (r1 - mid.astype(F32)).astype(BF16)
    return hi, mid, lo


def _cmp_attn_kernel(slopes_ref, q_ref, kc_ref, vc_ref, ov_ref, o_ref, sel_ref, *, tb, nc, n_sel, pos0, emit_idx):
    kvh = pl.program_id(1)
    i = pl.program_id(2)
    g_count = N_HEADS // NSA_KV_HEADS
    ncp = kc_ref.shape[0]
    qs = _stack_heads(q_ref[...], g_count).astype(BF16)
    kc = kc_ref[...].astype(BF16)
    vc = vc_ref[...].astype(BF16)
    s = lax.dot_general(qs, kc, (((1,), (1,)), ((), ())), preferred_element_type=F32) * SCALE
    t = pos0 + i * tb + lax.broadcasted_iota(jnp.int32, (tb, ncp), 0)
    c = lax.broadcasted_iota(jnp.int32, (tb, ncp), 1)
    valid = (c * CMP_STRIDE + (CMP_LEN - 1) <= t) & (c < nc)
    rel = t.astype(F32) - ((c * CMP_STRIDE).astype(F32) + (CMP_LEN - 1) / 2)
    psum = jnp.zeros((tb, ncp), F32)
    outs = []
    for g in range(g_count):
        sg = s[g * tb:(g + 1) * tb] - slopes_ref[kvh * g_count + g] * rel
        sg = jnp.where(valid, sg, 2 * NEG_FILL)
        m = jnp.maximum(jnp.max(sg, axis=-1, keepdims=True), NEG_FILL)
        p = jnp.exp(sg - m)
        p = p / jnp.maximum(jnp.sum(p, axis=-1, keepdims=True), 1e-30)
        psum = psum + p
        outs.append(jnp.dot(p.astype(BF16), vc, preferred_element_type=F32))
    o_ref[...] = jnp.concatenate(outs, axis=1)

    ov = ov_ref[...]
    if emit_idx:
        shape, t_dim, j_dim = (tb, ov.shape[1]), 0, 1
        imp = sum(jnp.dot(part, ov, preferred_element_type=F32) for part in _split3(psum))
    else:
        shape, t_dim, j_dim = (ov.shape[0], tb), 1, 0
        imp = sum(lax.dot_general(ov, part, (((1,), (1,)), ((), ())), preferred_element_type=F32)
                  for part in _split3(psum))
    tq = pos0 + i * tb + lax.broadcasted_iota(jnp.int32, shape, t_dim)
    j = lax.broadcasted_iota(jnp.int32, shape, j_dim)
    cur = tq // SEL_BLOCK
    forced = (j == 0) | (j == cur) | (j == cur - 1)
    visible = j * SEL_BLOCK <= tq
    rank = jnp.where(forced, 1e9, jnp.where(visible, imp, -1.0))
    rank = jnp.where(j < n_sel, rank, -2.0)
    cnt = jnp.zeros(shape, jnp.int32)
    for jp in range(n_sel):
        one = rank[:, jp:jp + 1] if emit_idx else rank[jp:jp + 1, :]
        beats = (one > rank) | ((one == rank) & (j > jp))
        cnt = cnt + beats.astype(jnp.int32)
    if emit_idx:
        lane = lax.broadcasted_iota(jnp.int32, (tb, LANE), 1)
        idx = jnp.zeros((tb, LANE), jnp.int32)
        jf = j.astype(F32)
        for r in range(SEL_TOPN):
            val = jnp.sum(jnp.where(cnt == r, jf, 0.0), axis=1, keepdims=True).astype(jnp.int32)
            idx = jnp.where(lane == r, val, idx)
        sel_ref[...] = idx
    else:
        chosen = (cnt < SEL_TOPN).astype(F32)
        chosen = jnp.concatenate([chosen, jnp.zeros((LANE - shape[0], tb), F32)], axis=0)
        sel_ref[...] = jnp.transpose(chosen)


def _overlap_matrix(ncp, n_sel, nsp):
    c_start = jnp.arange(ncp, dtype=jnp.int32)[:, None] * CMP_STRIDE
    j = jnp.arange(nsp, dtype=jnp.int32)[None, :]
    ov = jnp.clip(jnp.minimum(c_start + CMP_LEN, (j + 1) * SEL_BLOCK) - jnp.maximum(c_start, j * SEL_BLOCK), 0, None)
    ov = jnp.where(j < n_sel, ov, 0)
    return (ov.astype(F32) / CMP_LEN).astype(BF16)


def cmp_attention_prompt(proj, cmp_kv, slopes, *, batch, seq):
    g_count = N_HEADS // NSA_KV_HEADS
    tb = BAND_BLOCK
    nb = seq // tb
    ncp = cmp_kv.shape[3]
    n_sel = -(-seq // SEL_BLOCK)
    nsp = LANE
    assert tb == LANE and n_sel % 8 == 0
    ov_t = jnp.transpose(_overlap_matrix(ncp, n_sel, nsp))[:n_sel]
    smem = pl.BlockSpec(memory_space=pltpu.SMEM)
    kv_block = (None, None, None, ncp, HEAD_DIM)
    return pl.pallas_call(
        functools.partial(_cmp_attn_kernel, tb=tb, nc=ncp - 1, n_sel=n_sel, pos0=0, emit_idx=False),
        grid=(batch, NSA_KV_HEADS, nb),
        in_specs=[smem,
                  pl.BlockSpec((tb, g_count * HEAD_DIM), lambda b, h, i: (b * nb + i, h)),
                  pl.BlockSpec(kv_block, lambda b, h, i: (b, 0, h, 0, 0)),
                  pl.BlockSpec(kv_block, lambda b, h, i: (b, 1, h, 0, 0)),
                  pl.BlockSpec((n_sel, ncp), lambda b, h, i: (0, 0))],
        out_specs=[pl.BlockSpec((tb, g_count * HEAD_DIM), lambda b, h, i: (b * nb + i, h)),
                   pl.BlockSpec((None, None, tb, nsp), lambda b, h, i: (b, h, i, 0))],
        out_shape=[jax.ShapeDtypeStruct((batch * seq, N_HEADS * HEAD_DIM), F32),
                   jax.ShapeDtypeStruct((batch, NSA_KV_HEADS, seq, nsp), F32)],
        compiler_params=_params(("arbitrary", "arbitrary", "arbitrary")),
        name="cmp_attention_prompt",
    )(slopes, proj, cmp_kv, cmp_kv, ov_t)


def cmp_attention_sample(qkv8, cmp_kv, slopes, *, lk):
    g_count = N_HEADS // NSA_KV_HEADS
    nb = qkv8.shape[0]
    ncp = cmp_kv.shape[3]
    n_sel = -(-lk // SEL_BLOCK)
    nsp = -(-n_sel // LANE) * LANE
    ov = _overlap_matrix(ncp, n_sel, nsp)
    smem = pl.BlockSpec(memory_space=pltpu.SMEM)
    kv_block = (None, None, None, ncp, HEAD_DIM)
    return pl.pallas_call(
        functools.partial(_cmp_attn_kernel, tb=DEC_PAD, nc=ncp - 1, n_sel=n_sel, pos0=PAST_LEN, emit_idx=True),
        grid=(nb, NSA_KV_HEADS, 1),
        in_specs=[smem,
                  pl.BlockSpec((None, DEC_PAD, g_count * HEAD_DIM), lambda b, h, i: (b, 0, h)),
                  pl.BlockSpec(kv_block, lambda b, h, i: (b, 0, h, 0, 0)),
                  pl.BlockSpec(kv_block, lambda b, h, i: (b, 1, h, 0, 0)),
                  pl.BlockSpec((ncp, nsp), lambda b, h, i: (0, 0))],
        out_specs=[pl.BlockSpec((None, DEC_PAD, g_count * HEAD_DIM), lambda b, h, i: (b, 0, h)),
                   pl.BlockSpec((None, None, DEC_PAD, LANE), lambda b, h, i: (b, h, 0, 0))],
        out_shape=[jax.ShapeDtypeStruct((nb, DEC_PAD, N_HEADS * HEAD_DIM), F32),
                   jax.ShapeDtypeStruct((nb, NSA_KV_HEADS, DEC_PAD, LANE), jnp.int32)],
        compiler_params=_params(("arbitrary", "arbitrary", "arbitrary")),
        name="cmp_attention_sample",
    )(slopes, qkv8, cmp_kv, cmp_kv, ov)


SEL_KEY_TILE = 256

def _sel_prompt_kernel(slopes_ref, q_ref, k_ref, v_ref, sel_ref, ex_ref, oc_ref, ow_ref, gl_ref, gb_ref, o_ref,
                       m_ref, l_ref, acc_ref, chosen_ref):
    kvh = pl.program_id(1)
    i = pl.program_id(2)
    g_count = N_HEADS // NSA_KV_HEADS
    tb = BAND_BLOCK
    qs = _stack_heads(q_ref[...], g_count).astype(BF16)
    sel = sel_ref[...].astype(BF16)
    slope_col = jnp.concatenate(
        [jnp.full((tb, 1), slopes_ref[kvh * g_count + g], F32) for g in range(g_count)], axis=0)
    kt_w = SEL_KEY_TILE
    m_ref[...] = jnp.full(m_ref.shape, NEG_FILL, F32)
    l_ref[...] = jnp.zeros(l_ref.shape, F32)
    acc_ref[...] = jnp.zeros(acc_ref.shape, F32)
    for kt in range(chosen_ref.shape[0]):
        chosen_ref[kt] = jnp.dot(sel, ex_ref[:, kt * kt_w:(kt + 1) * kt_w], preferred_element_type=F32)
    tq = i * tb + lax.broadcasted_iota(jnp.int32, (tb, kt_w), 0)
    kk = lax.broadcasted_iota(jnp.int32, (tb, kt_w), 1)

    def attend(k0, dist):
        ks = k_ref[pl.ds(k0, kt_w), :].astype(BF16)
        vs = v_ref[pl.ds(k0, kt_w), :].astype(BF16)
        s = lax.dot_general(qs, ks, (((1,), (1,)), ((), ())), preferred_element_type=F32) * SCALE
        distf = jnp.concatenate([dist] * g_count, axis=0)
        s = jnp.where(distf >= 0.0, s - slope_col * distf, 2 * NEG_FILL)
        m_old = m_ref[...]
        m_new = jnp.maximum(m_old, jnp.max(s, axis=-1, keepdims=True))
        alpha = jnp.exp(m_old - m_new)
        p = jnp.exp(s - jnp.concatenate([m_new] * (kt_w // LANE), axis=1))
        l_ref[...] = alpha * l_ref[...] + jnp.sum(p, axis=-1, keepdims=True)
        acc_ref[...] = alpha * acc_ref[...] + jnp.dot(p.astype(BF16), vs, preferred_element_type=F32)
        m_ref[...] = m_new

    def body(kt, carry):
        k0 = pl.multiple_of(kt * kt_w, kt_w)
        dist = (tq - (k0 + kk)).astype(F32)
        dist = jnp.where(chosen_ref[kt] > 0.5, dist, -1.0)

        @pl.when(jnp.max(dist) >= 0.0)
        def _():
            attend(k0, dist)

        return carry

    n_tiles = (i * tb + tb + kt_w - 1) // kt_w
    lax.fori_loop(0, n_tiles, body, 0)
    o_sel = acc_ref[...] / jnp.maximum(l_ref[...], 1e-30)
    gate = jax.nn.sigmoid(gl_ref[...] + gb_ref[...])
    gate = pltpu.roll(gate, jnp.where(kvh == 0, 0, LANE - kvh * g_count), 1)
    for g in range(g_count):
        sl = slice(g * HEAD_DIM, (g + 1) * HEAD_DIM)
        og = (gate[:, g:g + 1] * oc_ref[:, sl] + gate[:, N_HEADS + g:N_HEADS + g + 1] * o_sel[g * tb:(g + 1) * tb]
              + gate[:, 2 * N_HEADS + g:2 * N_HEADS + g + 1] * ow_ref[:, sl])
        o_ref[:, sl] = og.astype(o_ref.dtype)


def sel_attention_prompt(proj, sel_mask, slopes, o_cmp, o_win, gate_logits, gate_bias, *, batch, seq, k_col0, v_col0):
    g_count = N_HEADS // NSA_KV_HEADS
    tb = BAND_BLOCK
    nb = seq // tb
    kb0, vb0 = k_col0 // HEAD_DIM, v_col0 // HEAD_DIM
    smem = pl.BlockSpec(memory_space=pltpu.SMEM)
    block_of_key = jnp.arange(seq, dtype=jnp.int32)[None, :] // SEL_BLOCK
    expand = (jnp.arange(LANE, dtype=jnp.int32)[:, None] == block_of_key).astype(BF16)
    group_tile = pl.BlockSpec((tb, g_count * HEAD_DIM), lambda b, h, i: (b * nb + i, h))
    return pl.pallas_call(
        _sel_prompt_kernel,
        grid=(batch, NSA_KV_HEADS, nb),
        in_specs=[smem,
                  group_tile,
                  pl.BlockSpec((seq, HEAD_DIM), lambda b, h, i: (b, kb0 + h)),
                  pl.BlockSpec((seq, HEAD_DIM), lambda b, h, i: (b, vb0 + h)),
                  pl.BlockSpec((None, None, tb, LANE), lambda b, h, i: (b, h, i, 0)),
                  pl.BlockSpec((LANE, seq), lambda b, h, i: (0, 0)),
                  group_tile,
                  group_tile,
                  pl.BlockSpec((tb, LANE), lambda b, h, i: (b * nb + i, 0)),
                  pl.BlockSpec((1, LANE), lambda b, h, i: (0, 0))],
        out_specs=group_tile,
        out_shape=jax.ShapeDtypeStruct((batch * seq, N_HEADS * HEAD_DIM), BF16),
        scratch_shapes=[pltpu.VMEM((g_count * tb, HEAD_DIM), F32)] * 3
        + [pltpu.VMEM((seq // SEL_KEY_TILE, tb, SEL_KEY_TILE), F32)],
        compiler_params=_params(("arbitrary", "arbitrary", "arbitrary")),
        name="sel_attention_prompt",
    )(slopes, proj, proj, proj, sel_mask, expand, o_cmp, o_win, gate_logits, gate_bias)


def _sel_sample_kernel(idx_ref, pt_ref, q_ref, *refs, n_past_blocks):
    blocks = refs[:SEL_TOPN]
    kn_ref, vn_ref, slope_ref, o_ref = refs[SEL_TOPN:]
    b, kvh, t = pl.program_id(0), pl.program_id(1), pl.program_id(2)
    base = ((b * NSA_KV_HEADS + kvh) * DEC_SEQ + t) * SEL_TOPN
    g_count = N_HEADS // NSA_KV_HEADS
    pad = jnp.zeros((LANE - DEC_PAD, HEAD_DIM), F32)
    kall = jnp.concatenate([r[:, kvh, :] for r in blocks] + [kn_ref[...], pad], axis=0).astype(BF16)
    vall = jnp.concatenate([r[:, NSA_KV_HEADS + kvh, :] for r in blocks] + [vn_ref[...], pad], axis=0).astype(BF16)
    q = q_ref[...].astype(BF16)
    s = lax.dot_general(q, kall, (((1,), (1,)), ((), ())), preferred_element_type=F32) * SCALE
    lane = lax.broadcasted_iota(jnp.int32, (g_count, LANE), 1)
    low = lane < SEL_BLOCK
    qpos = PAST_LEN + t
    pos_parts, ok_parts = [], []
    for c in range(SEL_TOPN // 2):
        b0 = idx_ref[base + 2 * c]
        b1 = idx_ref[base + 2 * c + 1]
        p0 = jnp.where(b0 < n_past_blocks, b0 * SEL_BLOCK, PAST_LEN + DEC_SEQ)
        p1 = jnp.where(b1 < n_past_blocks, b1 * SEL_BLOCK, PAST_LEN + DEC_SEQ)
        pos_parts.append(jnp.where(low, p0 + lane, p1 + (lane - SEL_BLOCK)))
    pos_parts.append(PAST_LEN + lane)
    dist = qpos - jnp.concatenate(pos_parts, axis=1)
    valid = dist >= 0
    slope = jnp.concatenate([slope_ref[...]] * (SEL_TOPN // 2 + 1), axis=1)
    o_ref[...] = _softmax_pv(s - slope * dist.astype(F32), valid, vall)


def sel_attention_sample(q_s, pool_halves, page_table, sel_idx, new_kv8, slope_rows, layer_half0):
    nb = q_s.shape[0]
    g_count = N_HEADS // NSA_KV_HEADS
    n_past_blocks = PAST_LEN // SEL_BLOCK
    per_page = PAGE_SIZE // SEL_BLOCK
    page_shift = per_page.bit_length() - 1
    assert per_page == 1 << page_shift

    def blk_spec(r):
        def index(b, h, t, idx, pt):
            blk = jnp.minimum(idx[((b * NSA_KV_HEADS + h) * DEC_SEQ + t) * SEL_TOPN + r], n_past_blocks - 1)
            page = pt[b, lax.shift_right_logical(blk, page_shift)]
            return (layer_half0 + page * per_page + jnp.bitwise_and(blk, per_page - 1), 0, 1, 0)
        return pl.BlockSpec((None, SEL_BLOCK, CMP_HEADS, HEAD_DIM), index)

    q_block = (None, None, None, g_count, HEAD_DIM)
    new_block = (None, None, None, DEC_PAD, HEAD_DIM)
    grid_spec = pltpu.PrefetchScalarGridSpec(
        num_scalar_prefetch=2,
        grid=(nb, NSA_KV_HEADS, DEC_SEQ),
        in_specs=[pl.BlockSpec(q_block, lambda b, h, t, idx, pt: (b, t, h, 0, 0))]
        + [blk_spec(r) for r in range(SEL_TOPN)]
        + [pl.BlockSpec(new_block, lambda b, h, t, idx, pt: (b, 2, h, 0, 0)),
           pl.BlockSpec(new_block, lambda b, h, t, idx, pt: (b, 3, h, 0, 0)),
           pl.BlockSpec((None, g_count, LANE), lambda b, h, t, idx, pt: (h, 0, 0))],
        out_specs=pl.BlockSpec(q_block, lambda b, h, t, idx, pt: (b, t, h, 0, 0)),
    )
    return pl.pallas_call(
        functools.partial(_sel_sample_kernel, n_past_blocks=n_past_blocks),
        grid_spec=grid_spec,
        out_shape=jax.ShapeDtypeStruct(q_s.shape, F32),
        compiler_params=_params(("arbitrary", "arbitrary", "arbitrary")),
        name="sel_attention_sample",
    )(sel_idx, page_table, q_s, *([pool_halves] * SEL_TOPN), new_kv8, new_kv8, slope_rows)


def _combine_kernel(oc_ref, os_ref, ow_ref, gl_ref, gb_ref, o_ref):
    gate = jax.nn.sigmoid(gl_ref[...] + gb_ref[...])
    for h in range(N_HEADS):
        sl = slice(h * HEAD_DIM, (h + 1) * HEAD_DIM)
        o = (gate[:, h:h + 1] * oc_ref[:, sl] + gate[:, N_HEADS + h:N_HEADS + h + 1] * os_ref[:, sl]
             + gate[:, 2 * N_HEADS + h:2 * N_HEADS + h + 1] * ow_ref[:, sl])
        o_ref[:, sl] = o.astype(o_ref.dtype)


def gate_combine(o_c, o_s, o_w, gate_logits, gate_bias, tm, row_block0=0):
    m, n = o_c.shape
    big = pl.BlockSpec((tm, n), lambda i: (i, 0))
    return pl.pallas_call(
        _combine_kernel,
        grid=(m // tm,),
        in_specs=[big, big, big,
                  pl.BlockSpec((tm, LANE), lambda i: (row_block0 + i, 0)),
                  pl.BlockSpec((1, LANE), lambda i: (0, 0))],
        out_specs=big,
        out_shape=jax.ShapeDtypeStruct((m, n), BF16),
        compiler_params=_params(("arbitrary",)),
        name="gate_combine",
    )(o_c, o_s, o_w, gate_logits, gate_bias)


TM_DENSE = 2080
TN_PROJ = 256
TF_FFN = 256
TM_FFN_OUT = 1040
TN_FFN_OUT = 256
TM_NORM = 320
TM_PROMPT = 256
NQ = N_HEADS * HEAD_DIM


def _ffn_block(x, g, w_in, w_out_bf, layer, sub):
    xn = rms_norm(x, g, BF16, TM_NORM)
    h = ffn_in(xn, w_in, layer, sub, TM_DENSE, TF_FFN)
    return ffn_out(h, w_out_bf, layer, sub, x, TM_FFN_OUT, TN_FFN_OUT)


def _sample_rows8(proj):
    s = proj[M_PROMPT:].reshape(DEC_BATCH, DEC_SEQ, proj.shape[1])
    return jnp.pad(s, ((0, 0), (0, DEC_PAD - DEC_SEQ), (0, 0)))


def _swa_layer(x, g, w_in, w_out, sinks_all, cache_all, li, slopes):
    xn = rms_norm(x, g, BF16, TM_NORM)
    kv_cols = 2 * SWA_KV_HEADS * HEAD_DIM
    proj = matmul_wres(xn, w_in, (li,), NQ + kv_cols, TM_DENSE, TN_PROJ)
    sinks = sinks_all[li].astype(F32)
    o_p = band_attention(proj, slopes, sinks, batch=BATCH, seq=SEQ, kv_heads=SWA_KV_HEADS, k_col0=NQ,
                         v_col0=NQ + SWA_KV_HEADS * HEAD_DIM, window=SWA_WINDOW, use_sink=True, out_dtype=BF16)
    cache = cache_all[li]
    lc = cache.shape[1]
    o_s = decode_attention(_sample_rows8(proj), cache.reshape(DEC_BATCH, lc * 2 * SWA_KV_HEADS, HEAD_DIM), slopes, sinks,
                           kv_heads=SWA_KV_HEADS, k_col0=NQ, v_col0=NQ + SWA_KV_HEADS * HEAD_DIM,
                           window=SWA_WINDOW, use_sink=True, out_dtype=BF16)
    attn = jnp.concatenate([o_p, o_s[:, :DEC_SEQ].reshape(M_SAMPLE, NQ)], axis=0)
    x = matmul_wres(attn, w_out, (li,), D_MODEL, TM_DENSE, TN_PROJ, res=x)
    kv_p = proj[:M_PROMPT, NQ:].reshape(BATCH, SEQ, 2, SWA_KV_HEADS, HEAD_DIM)
    kv_s = proj[M_PROMPT:, NQ:].reshape(DEC_BATCH, DEC_SEQ, 2, SWA_KV_HEADS, HEAD_DIM)
    buf_p = kv_p[:, SEQ - min(SWA_WINDOW, SEQ):]
    buf_s = jnp.concatenate([cache, kv_s], axis=1)[:, DEC_SEQ:]
    return x, buf_p, buf_s


def _nsa_layer(x, g, w_in, gate_b, cmp_pe, cmp_w1, cmp_w2, w_out, win_cache_all, pool_all, page_table, li, slopes):
    xn = rms_norm(x, g, BF16, TM_NORM)
    kvw = NSA_KV_HEADS * HEAD_DIM
    main_cols = NQ + 6 * kvw
    proj = matmul_wres(xn, w_in, (li,), main_cols, TM_DENSE, TN_PROJ)
    n_gate = 3 * N_HEADS
    w_gate = jnp.pad(w_in[li][:, main_cols:], ((0, 0), (0, LANE - n_gate)))
    gate_logits = matmul_wres(xn, w_gate, (), LANE, TM_DENSE, LANE)
    gate_bias = jnp.pad(gate_b[li], (0, LANE - n_gate)).reshape(1, LANE)
    zeros = jnp.zeros((N_HEADS,), F32)

    pet = pe_term(cmp_pe, cmp_w1, li)
    w1 = cmp_w1[li].astype(BF16)
    top = jnp.concatenate([w1[:, 0:CMP_STRIDE:2], w1[:, CMP_STRIDE::2]], axis=-1)
    bot = jnp.concatenate([w1[:, 1:CMP_STRIDE:2], w1[:, CMP_STRIDE + 1::2]], axis=-1)
    w1_pairs = jnp.concatenate([top, bot], axis=-2)

    o_w_p = band_attention(proj, slopes, zeros, batch=BATCH, seq=SEQ, kv_heads=NSA_KV_HEADS, k_col0=NQ + 4 * kvw,
                           v_col0=NQ + 5 * kvw, window=NSA_WINDOW, use_sink=False, out_dtype=F32)
    cmp_p = compress_prompt(proj, cmp_w1, cmp_w2, pet, li, batch=BATCH, seq=SEQ, col0=NQ)
    o_c_p, sel_mask = cmp_attention_prompt(proj, cmp_p, slopes, batch=BATCH, seq=SEQ)
    attn_p = sel_attention_prompt(proj, sel_mask, slopes, o_c_p, o_w_p, gate_logits, gate_bias, batch=BATCH, seq=SEQ,
                                  k_col0=NQ + 2 * kvw, v_col0=NQ + 3 * kvw)

    qkv8 = _sample_rows8(proj)
    win_cache = win_cache_all[li]
    lc = win_cache.shape[1]
    o_w_s = decode_attention(qkv8, win_cache.reshape(DEC_BATCH, lc * 2 * NSA_KV_HEADS, HEAD_DIM), slopes, zeros,
                             kv_heads=NSA_KV_HEADS, k_col0=NQ + 4 * kvw, v_col0=NQ + 5 * kvw, window=NSA_WINDOW,
                             use_sink=False, out_dtype=F32)
    n_pool = pool_all.shape[1]
    pool = pool_all.reshape(pool_all.shape[0] * n_pool, PAGE_SIZE, 4 * kvw)
    pool4 = pool.reshape(pool.shape[0], PAGE_SIZE, 2 * CMP_HEADS, HEAD_DIM)
    part_a, part_b = compress_pages_partial(pool4, page_table, w1_pairs, li * n_pool)
    cmp_s = compress_pages_finish(part_a, part_b, cmp_w2, pet, li)
    o_c_s, sel_idx = cmp_attention_sample(qkv8, cmp_s, slopes, lk=PAST_LEN + DEC_SEQ)
    sel_flat = sel_idx[:, :, :DEC_SEQ, :SEL_TOPN].reshape(-1)
    rows_s = proj[M_PROMPT:, NQ:NQ + 4 * kvw].reshape(DEC_BATCH, DEC_SEQ, 4, NSA_KV_HEADS, HEAD_DIM)
    new_kv8 = jnp.pad(rows_s.transpose(0, 2, 3, 1, 4), ((0, 0), (0, 0), (0, 0), (0, DEC_PAD - DEC_SEQ), (0, 0)))
    g_count = N_HEADS // NSA_KV_HEADS
    q_s = proj[M_PROMPT:, :NQ].reshape(DEC_BATCH, DEC_SEQ, NSA_KV_HEADS, g_count, HEAD_DIM)
    slope_rows = jnp.broadcast_to(slopes.reshape(NSA_KV_HEADS, g_count, 1), (NSA_KV_HEADS, g_count, LANE))
    per_page = PAGE_SIZE // SEL_BLOCK
    pool_halves = pool.reshape(pool.shape[0] * per_page, SEL_BLOCK, 2 * CMP_HEADS, HEAD_DIM)
    o_s_s = sel_attention_sample(q_s, pool_halves, page_table, sel_flat, new_kv8, slope_rows, li * n_pool * per_page)
    attn_s = gate_combine(o_c_s[:, :DEC_SEQ].reshape(M_SAMPLE, NQ), o_s_s.reshape(M_SAMPLE, NQ),
                          o_w_s[:, :DEC_SEQ].reshape(M_SAMPLE, NQ), gate_logits, gate_bias, M_SAMPLE,
                          row_block0=M_PROMPT // M_SAMPLE)

    attn = jnp.concatenate([attn_p, attn_s], axis=0)
    x = matmul_wres(attn, w_out, (li,), D_MODEL, TM_DENSE, TN_PROJ, res=x)

    kv_p = proj[:M_PROMPT, NQ:main_cols].reshape(BATCH, SEQ, 6, NSA_KV_HEADS, HEAD_DIM)
    kv_s = proj[M_PROMPT:, NQ:main_cols].reshape(DEC_BATCH, DEC_SEQ, 6, NSA_KV_HEADS, HEAD_DIM)
    win_p = kv_p[:, SEQ - min(NSA_WINDOW, SEQ):, 4:]
    win_s = jnp.concatenate([win_cache, kv_s[:, :, 4:]], axis=1)[:, DEC_SEQ:]
    return x, kv_p[:, :, :4], kv_s[:, :, :4], win_p, win_s


def kernel(x_prompt, x_sample, cache_swa_kv, cache_nsa_win_kv, cache_nsa_kv, page_table, norm_g, final_norm_g,
           ffn_w_in, ffn_w_out, swa_w_in, swa_w_out, swa_sinks, nsa_w_in, nsa_gate_b, nsa_cmp_pe, nsa_cmp_w1,
           nsa_cmp_w2, nsa_w_out):
    x = jnp.concatenate([x_prompt.reshape(M_PROMPT, D_MODEL), x_sample.reshape(M_SAMPLE, D_MODEL)], axis=0)
    slopes = jnp.exp2(-8.0 * jnp.arange(1, N_HEADS + 1, dtype=F32) / N_HEADS)
    w_out_bf = ffn_w_out.astype(BF16)
    swa_p, swa_s, win_p, win_s, kv_p, kv_s = [], [], [], [], [], []
    for i in range(DEPTH):
        x = _ffn_block(x, norm_g[i, 0], ffn_w_in, w_out_bf, i, 0)
        li = i // N_MIXERS
        if i % N_MIXERS == 0:
            x, bp, bs = _swa_layer(x, norm_g[i, 1], swa_w_in, swa_w_out, swa_sinks, cache_swa_kv, li, slopes)
            swa_p.append(bp)
            swa_s.append(bs)
        else:
            x, rp, rs, wp, ws = _nsa_layer(x, norm_g[i, 1], nsa_w_in, nsa_gate_b, nsa_cmp_pe, nsa_cmp_w1,
                                           nsa_cmp_w2, nsa_w_out, cache_nsa_win_kv, cache_nsa_kv, page_table, li,
                                           slopes)
            kv_p.append(rp)
            kv_s.append(rs)
            win_p.append(wp)
            win_s.append(ws)
        x = _ffn_block(x, norm_g[i, 2], ffn_w_in, w_out_bf, i, 1)
    y_p = rms_norm(x, final_norm_g, F32, TM_PROMPT, rows=M_PROMPT).reshape(BATCH, SEQ, D_MODEL)
    y_s = rms_norm(x, final_norm_g, F32, M_SAMPLE, row_block0=M_PROMPT // M_SAMPLE, rows=M_SAMPLE)
    return (y_p, y_s.reshape(DEC_BATCH, DEC_SEQ, D_MODEL), jnp.stack(swa_p), jnp.stack(swa_s), jnp.stack(win_p),
            jnp.stack(win_s), jnp.stack(kv_p), jnp.stack(kv_s))
```

```python
import functools
import math

import jax
import jax.numpy as jnp
from jax import lax
from jax.experimental import pallas as pl
from jax.experimental.pallas import tpu as pltpu

D_MODEL = 4096
BATCH = 4
SEQ = 2048
DEPTH = 2
DEC_BATCH = 32
DEC_SEQ = 4
PAST_LEN = 16384
PAGE_SIZE = 128
N_HEADS = 32
HEAD_DIM = 128
SWA_KV_HEADS = 8
SWA_WINDOW = 128
NSA_KV_HEADS = 4
CMP_STRIDE = 16
CMP_LEN = 32
SEL_BLOCK = 64
SEL_TOPN = 16
NSA_WINDOW = 512
D_FF = 11008
BAND_BLOCK = 128
N_MIXERS = 2
RMS_EPS = 1e-6
NEG_FILL = -1e30
SCALE = HEAD_DIM ** -0.5

M_PROMPT = BATCH * SEQ
M_SAMPLE = DEC_BATCH * DEC_SEQ
M_ALL = M_PROMPT + M_SAMPLE
DEC_PAD = 8
LANE = 128
F32 = jnp.float32
BF16 = jnp.bfloat16
VMEM_LIMIT = 60 * 1024 * 1024


def _params(sem):
    return pltpu.CompilerParams(dimension_semantics=sem, vmem_limit_bytes=VMEM_LIMIT)


def _rms_kernel(x_ref, g_ref, o_ref):
    x = x_ref[...]
    y = x * lax.rsqrt(jnp.mean(x * x, axis=-1, keepdims=True) + RMS_EPS)
    o_ref[...] = (y * g_ref[...]).astype(o_ref.dtype)


def rms_norm(x, g, out_dtype, tm, row_block0=0, rows=None):
    m, d = x.shape
    rows = m if rows is None else rows
    return pl.pallas_call(
        _rms_kernel,
        grid=(rows // tm,),
        in_specs=[pl.BlockSpec((tm, d), lambda i: (row_block0 + i, 0)),
                  pl.BlockSpec((1, d), lambda i: (0, 0))],
        out_specs=pl.BlockSpec((tm, d), lambda i: (i, 0)),
        out_shape=jax.ShapeDtypeStruct((rows, d), out_dtype),
        compiler_params=_params(("arbitrary",)),
        name="rms_norm",
    )(x, g.reshape(1, d))


def _row_tile_spec(tm, k):
    return pl.BlockSpec((tm, k), lambda i, j: (i, 0))


def _mm_kernel(x_ref, w_ref, *rest, scale, has_res):
    acc = jnp.dot(x_ref[...], w_ref[...].astype(BF16), preferred_element_type=F32)
    if has_res:
        res_ref, o_ref = rest
        o_ref[...] = res_ref[...] + (acc if scale == 1.0 else scale * acc)
    else:
        (o_ref,) = rest
        o_ref[...] = acc.astype(o_ref.dtype)


def matmul_wres(x, w, w_prefix, n_cols, tm, tn, res=None, scale=1.0, col_block0=0):
    m, k = x.shape
    npre = len(w_prefix)
    w_block = (None,) * npre + (k, tn)
    in_specs = [_row_tile_spec(tm, k),
                pl.BlockSpec(w_block, lambda i, j: tuple(w_prefix) + (0, col_block0 + j))]
    args = [x, w]
    if res is not None:
        in_specs.append(pl.BlockSpec((tm, tn), lambda i, j: (i, j)))
        args.append(res)
    return pl.pallas_call(
        functools.partial(_mm_kernel, scale=scale, has_res=res is not None),
        grid=(m // tm, n_cols // tn),
        in_specs=in_specs,
        out_specs=pl.BlockSpec((tm, tn), lambda i, j: (i, j)),
        out_shape=jax.ShapeDtypeStruct((m, n_cols), F32),
        compiler_params=_params(("arbitrary", "arbitrary")),
        name="matmul_wres",
    )(*args)


def _ffn_in_kernel(x_ref, wg_ref, wu_ref, o_ref):
    x = x_ref[...]
    g = jnp.dot(x, wg_ref[...].astype(BF16), preferred_element_type=F32)
    u = jnp.dot(x, wu_ref[...].astype(BF16), preferred_element_type=F32)
    o_ref[...] = (jax.nn.silu(g) * u).astype(o_ref.dtype)


def ffn_in(xn, w_in, layer, sub, tm, tf):
    m, k = xn.shape
    nf = D_FF // tf
    w_block = (None, None, k, tf)
    return pl.pallas_call(
        _ffn_in_kernel,
        grid=(m // tm, nf),
        in_specs=[_row_tile_spec(tm, k),
                  pl.BlockSpec(w_block, lambda i, j: (layer, sub, 0, j)),
                  pl.BlockSpec(w_block, lambda i, j: (layer, sub, 0, nf + j))],
        out_specs=pl.BlockSpec((tm, tf), lambda i, j: (i, j)),
        out_shape=jax.ShapeDtypeStruct((m, D_FF), BF16),
        compiler_params=_params(("arbitrary", "arbitrary")),
        name="ffn_in",
    )(xn, w_in, w_in)


def _ffn_out_kernel(h_ref, w_ref, res_ref, o_ref):
    acc = jnp.dot(h_ref[...], w_ref[...], preferred_element_type=F32)
    o_ref[...] = res_ref[...] + 0.5 * acc


def ffn_out(h, w_out_bf, layer, sub, res, tm, tn):
    m, k = h.shape
    n = res.shape[1]
    return pl.pallas_call(
        _ffn_out_kernel,
        grid=(m // tm, n // tn),
        in_specs=[_row_tile_spec(tm, k),
                  pl.BlockSpec((None, None, k, tn), lambda i, j: (layer, sub, 0, j)),
                  pl.BlockSpec((tm, tn), lambda i, j: (i, j))],
        out_specs=pl.BlockSpec((tm, tn), lambda i, j: (i, j)),
        out_shape=jax.ShapeDtypeStruct((m, n), F32),
        compiler_params=_params(("arbitrary", "arbitrary")),
        name="ffn_out",
    )(h, w_out_bf, res)


def _stack_heads(q, g_count):
    return jnp.concatenate([q[:, g * HEAD_DIM:(g + 1) * HEAD_DIM] for g in range(g_count)], axis=0)


def _softmax_pv(s, valid, v_bf, sink=None):
    s = jnp.where(valid, s, 2 * NEG_FILL)
    m = jnp.maximum(jnp.max(s, axis=-1, keepdims=True), NEG_FILL)
    if sink is not None:
        m = jnp.maximum(m, sink)
    p = jnp.exp(s - m)
    den = jnp.sum(p, axis=-1, keepdims=True)
    if sink is not None:
        den = den + jnp.exp(sink - m)
    o = jnp.dot(p.astype(BF16), v_bf, preferred_element_type=F32)
    return o / jnp.maximum(den, 1e-30)


def _band_kernel(slopes_ref, sinks_ref, q_ref, k_ref, v_ref, o_ref, *, g_count, nprev, window, use_sink):
    kvh = pl.program_id(1)
    i = pl.program_id(2)
    width = (nprev + 1) * BAND_BLOCK
    start = pl.multiple_of(jnp.maximum(i - nprev, 0) * BAND_BLOCK, BAND_BLOCK)
    k = k_ref[pl.ds(start, width), :].astype(BF16)
    v = v_ref[pl.ds(start, width), :].astype(BF16)
    qs = _stack_heads(q_ref[...], g_count).astype(BF16)
    s = lax.dot_general(qs, k, (((1,), (1,)), ((), ())), preferred_element_type=F32) * SCALE
    tq = i * BAND_BLOCK + lax.broadcasted_iota(jnp.int32, (BAND_BLOCK, width), 0)
    kp = start + lax.broadcasted_iota(jnp.int32, (BAND_BLOCK, width), 1)
    dist = tq - kp
    valid = (dist >= 0) & (dist <= window)
    distf = dist.astype(F32)
    outs = []
    for g in range(g_count):
        h = kvh * g_count + g
        sg = s[g * BAND_BLOCK:(g + 1) * BAND_BLOCK] - slopes_ref[h] * distf
        outs.append(_softmax_pv(sg, valid, v, sinks_ref[h] if use_sink else None))
    o_ref[...] = jnp.concatenate(outs, axis=1).astype(o_ref.dtype)


def band_attention(proj, slopes, sinks, *, batch, seq, kv_heads, k_col0, v_col0, window, use_sink, out_dtype):
    g_count = N_HEADS // kv_heads
    nb = seq // BAND_BLOCK
    nprev = -(-window // BAND_BLOCK)
    kb0, vb0 = k_col0 // HEAD_DIM, v_col0 // HEAD_DIM
    smem = pl.BlockSpec(memory_space=pltpu.SMEM)
    return pl.pallas_call(
        functools.partial(_band_kernel, g_count=g_count, nprev=nprev, window=window, use_sink=use_sink),
        grid=(batch, kv_heads, nb),
        in_specs=[smem, smem,
                  pl.BlockSpec((BAND_BLOCK, g_count * HEAD_DIM), lambda b, h, i: (b * nb + i, h)),
                  pl.BlockSpec((seq, HEAD_DIM), lambda b, h, i: (b, kb0 + h)),
                  pl.BlockSpec((seq, HEAD_DIM), lambda b, h, i: (b, vb0 + h))],
        out_specs=pl.BlockSpec((BAND_BLOCK, g_count * HEAD_DIM), lambda b, h, i: (b * nb + i, h)),
        out_shape=jax.ShapeDtypeStruct((batch * seq, N_HEADS * HEAD_DIM), out_dtype),
        compiler_params=_params(("arbitrary", "arbitrary", "arbitrary")),
        name="band_attention",
    )(slopes, sinks, proj, proj, proj)


def _dec_kernel(slopes_ref, sinks_ref, qkv_ref, c_ref, o_ref, *, kv_heads, g_count, lc, k_col0, v_col0,
                window, use_sink):
    rows = g_count * DEC_PAD
    width = lc + LANE
    t = lax.broadcasted_iota(jnp.int32, (DEC_PAD, width), 0)
    col = lax.broadcasted_iota(jnp.int32, (DEC_PAD, width), 1)
    dist = jnp.where(col < lc, t + (lc - col), t - (col - lc))
    valid = (dist >= 0) & (dist <= window) & (col < lc + DEC_SEQ)
    distf = dist.astype(F32)
    pad = jnp.zeros((LANE - DEC_PAD, HEAD_DIM), F32)
    for kvh in range(kv_heads):
        q0 = kvh * g_count * HEAD_DIM
        qs = _stack_heads(qkv_ref[:, q0:q0 + g_count * HEAD_DIM], g_count).astype(BF16)
        kn = qkv_ref[:, k_col0 + kvh * HEAD_DIM:k_col0 + (kvh + 1) * HEAD_DIM]
        vn = qkv_ref[:, v_col0 + kvh * HEAD_DIM:v_col0 + (kvh + 1) * HEAD_DIM]
        kc = c_ref[pl.ds(kvh, lc, stride=2 * kv_heads), :]
        vc = c_ref[pl.ds(kv_heads + kvh, lc, stride=2 * kv_heads), :]
        kall = jnp.concatenate([kc, kn, pad], axis=0).astype(BF16)
        vall = jnp.concatenate([vc, vn, pad], axis=0).astype(BF16)
        s = lax.dot_general(qs, kall, (((1,), (1,)), ((), ())), preferred_element_type=F32) * SCALE
        assert s.shape == (rows, width)
        for g in range(g_count):
            h = kvh * g_count + g
            sg = s[g * DEC_PAD:(g + 1) * DEC_PAD] - slopes_ref[h] * distf
            o = _softmax_pv(sg, valid, vall, sinks_ref[h] if use_sink else None)
            o_ref[:, h * HEAD_DIM:(h + 1) * HEAD_DIM] = o.astype(o_ref.dtype)


def decode_attention(qkv8, cache, slopes, sinks, *, kv_heads, k_col0, v_col0, window, use_sink, out_dtype):
    nb, _, ncols = qkv8.shape
    lc = cache.shape[1] // (2 * kv_heads)
    g_count = N_HEADS // kv_heads
    smem = pl.BlockSpec(memory_space=pltpu.SMEM)
    return pl.pallas_call(
        functools.partial(_dec_kernel, kv_heads=kv_heads, g_count=g_count, lc=lc, k_col0=k_col0, v_col0=v_col0,
                          window=window, use_sink=use_sink),
        grid=(nb,),
        in_specs=[smem, smem,
                  pl.BlockSpec((None, DEC_PAD, ncols), lambda b: (b, 0, 0)),
                  pl.BlockSpec((None, cache.shape[1], HEAD_DIM), lambda b: (b, 0, 0))],
        out_specs=pl.BlockSpec((None, DEC_PAD, N_HEADS * HEAD_DIM), lambda b: (b, 0, 0)),
        out_shape=jax.ShapeDtypeStruct((nb, DEC_PAD, N_HEADS * HEAD_DIM), out_dtype),
        compiler_params=_params(("arbitrary",)),
        name="decode_attention",
    )(slopes, sinks, qkv8, cache)


def _pe_term_kernel(pe_ref, w1_ref, o_ref):
    acc = jnp.zeros((8, HEAD_DIM), F32)
    for l in range(CMP_LEN):
        row = jnp.broadcast_to(pe_ref[l:l + 1, :], (8, HEAD_DIM)).astype(BF16)
        acc = acc + jnp.dot(row, w1_ref[l].astype(BF16), preferred_element_type=F32)
    o_ref[...] = acc


def pe_term(pe, w1, layer):
    return pl.pallas_call(
        _pe_term_kernel,
        grid=(2,),
        in_specs=[pl.BlockSpec((None, None, CMP_LEN, HEAD_DIM), lambda s: (layer, s, 0, 0)),
                  pl.BlockSpec((None, None, CMP_LEN, HEAD_DIM, HEAD_DIM), lambda s: (layer, s, 0, 0, 0))],
        out_specs=pl.BlockSpec((None, 8, HEAD_DIM), lambda s: (s, 0, 0)),
        out_shape=jax.ShapeDtypeStruct((2, 8, HEAD_DIM), F32),
        compiler_params=_params(("arbitrary",)),
        name="pe_term",
    )(pe, w1)


def _compress_finish(acc_a, acc_b, pet_row, w2_bf):
    n = acc_a.shape[0]
    pre = acc_a + pltpu.roll(acc_b, n - 1, 0) + pet_row
    return jnp.dot(jax.nn.gelu(pre).astype(BF16), w2_bf, preferred_element_type=F32)


def _cmp_prompt_kernel(x_ref, w1_ref, w2_ref, pet_ref, o_ref, *, nch):
    acc_a = jnp.zeros((nch, HEAD_DIM), F32)
    acc_b = jnp.zeros((nch, HEAD_DIM), F32)
    for l in range(CMP_STRIDE):
        xl = x_ref[pl.ds(l, nch, stride=CMP_STRIDE), :].astype(BF16)
        acc_a = acc_a + jnp.dot(xl, w1_ref[l].astype(BF16), preferred_element_type=F32)
        acc_b = acc_b + jnp.dot(xl, w1_ref[CMP_STRIDE + l].astype(BF16), preferred_element_type=F32)
    o_ref[...] = _compress_finish(acc_a, acc_b, pet_ref[0:1, :], w2_ref[...].astype(BF16))


def compress_prompt(proj, w1, w2, pet, layer, *, batch, seq, col0):
    nch = seq // CMP_STRIDE
    cb0 = col0 // HEAD_DIM
    return pl.pallas_call(
        functools.partial(_cmp_prompt_kernel, nch=nch),
        grid=(batch, 2, NSA_KV_HEADS),
        in_specs=[pl.BlockSpec((seq, HEAD_DIM), lambda b, s, h: (b, cb0 + s * NSA_KV_HEADS + h)),
                  pl.BlockSpec((None, None, CMP_LEN, HEAD_DIM, HEAD_DIM), lambda b, s, h: (layer, s, 0, 0, 0)),
                  pl.BlockSpec((None, None, HEAD_DIM, HEAD_DIM), lambda b, s, h: (layer, s, 0, 0)),
                  pl.BlockSpec((None, 8, HEAD_DIM), lambda b, s, h: (s, 0, 0))],
        out_specs=pl.BlockSpec((None, None, None, nch, HEAD_DIM), lambda b, s, h: (b, s, h, 0, 0)),
        out_shape=jax.ShapeDtypeStruct((batch, 2, NSA_KV_HEADS, nch, HEAD_DIM), F32),
        compiler_params=_params(("arbitrary", "arbitrary", "arbitrary")),
        name="compress_prompt",
    )(proj, w1, w2, pet)


CMP_PAGES = 16
CHUNKS_PER_PAGE = PAGE_SIZE // CMP_STRIDE
CMP_HEADS = 2 * NSA_KV_HEADS
CMP_SLOT_ROWS = CMP_PAGES * CHUNKS_PER_PAGE * NSA_KV_HEADS


def _cmp_pages_kernel(pt_ref, *refs):
    pages = refs[:CMP_PAGES]
    w_ref, a_ref, b_ref = refs[CMP_PAGES:]
    low = lax.broadcasted_iota(jnp.int32, (CMP_HEADS, HEAD_DIM), 0) < NSA_KV_HEADS
    acc = [jnp.zeros((CMP_SLOT_ROWS, 2 * HEAD_DIM), F32) for _ in range(2)]
    for lp in range(CMP_STRIDE // 2):
        halves = ([], [])
        for l in (2 * lp, 2 * lp + 1):
            tiles = ([], [])
            for r in range(CMP_PAGES):
                for c in range(0, CHUNKS_PER_PAGE, 2):
                    even = pages[r][l + CMP_STRIDE * c]
                    odd = pages[r][l + CMP_STRIDE * (c + 1)]
                    tiles[0].append(jnp.where(low, even, pltpu.roll(odd, NSA_KV_HEADS, 0)))
                    tiles[1].append(jnp.where(low, pltpu.roll(even, NSA_KV_HEADS, 0), odd))
            for s in range(2):
                halves[s].append(jnp.concatenate(tiles[s], axis=0))
        for s in range(2):
            lhs = jnp.concatenate(halves[s], axis=1).astype(BF16)
            acc[s] = acc[s] + jnp.dot(lhs, w_ref[s, lp], preferred_element_type=F32)
    for s in range(2):
        a_ref[s] = acc[s][:, :HEAD_DIM]
        b_ref[s] = acc[s][:, HEAD_DIM:]


def compress_pages_partial(pool4, page_table, w1_pairs, layer_page0):
    nb, n_pages = page_table.shape
    n_groups = n_pages // CMP_PAGES

    def page_spec(r):
        return pl.BlockSpec((None, PAGE_SIZE, CMP_HEADS, HEAD_DIM),
                            lambda b, j, pt: (layer_page0 + pt[b, j * CMP_PAGES + r], 0, 0, 0))

    out_spec = pl.BlockSpec((None, 2, CMP_SLOT_ROWS, HEAD_DIM), lambda b, j, pt: (b, 0, j, 0))
    out_shape = jax.ShapeDtypeStruct((nb, 2, n_groups * CMP_SLOT_ROWS, HEAD_DIM), F32)
    grid_spec = pltpu.PrefetchScalarGridSpec(
        num_scalar_prefetch=1,
        grid=(nb, n_groups),
        in_specs=[page_spec(r) for r in range(CMP_PAGES)]
        + [pl.BlockSpec(w1_pairs.shape, lambda b, j, pt: (0, 0, 0, 0))],
        out_specs=[out_spec, out_spec],
    )
    return pl.pallas_call(
        _cmp_pages_kernel,
        grid_spec=grid_spec,
        out_shape=[out_shape, out_shape],
        compiler_params=_params(("arbitrary", "arbitrary")),
        name="compress_pages_partial",
    )(page_table, *([pool4] * CMP_PAGES), w1_pairs)


def _cmp_finish_kernel(a_ref, b_ref, w2_ref, pet_ref, o_ref, scr_ref):
    rows = a_ref.shape[0]
    b_next = pltpu.roll(b_ref[...], rows - NSA_KV_HEADS, 0)
    h = jax.nn.gelu(a_ref[...] + b_next + pet_ref[0:1, :])
    res = jnp.dot(h.astype(BF16), w2_ref[...].astype(BF16), preferred_element_type=F32)
    row = lax.broadcasted_iota(jnp.int32, (rows, HEAD_DIM), 0)
    scr_ref[...] = jnp.where(row >= rows - NSA_KV_HEADS, 0.0, res)
    for head in range(NSA_KV_HEADS):
        o_ref[head] = scr_ref[pl.ds(head, rows // NSA_KV_HEADS, stride=NSA_KV_HEADS), :]


def compress_pages_finish(a, b, w2, pet, layer):
    nb, _, rows, _ = a.shape
    blk = (None, None, rows, HEAD_DIM)
    return pl.pallas_call(
        _cmp_finish_kernel,
        grid=(nb, 2),
        in_specs=[pl.BlockSpec(blk, lambda b_, s: (b_, s, 0, 0)),
                  pl.BlockSpec(blk, lambda b_, s: (b_, s, 0, 0)),
                  pl.BlockSpec((None, None, HEAD_DIM, HEAD_DIM), lambda b_, s: (layer, s, 0, 0)),
                  pl.BlockSpec((None, 8, HEAD_DIM), lambda b_, s: (s, 0, 0))],
        out_specs=pl.BlockSpec((None, None, NSA_KV_HEADS, rows // NSA_KV_HEADS, HEAD_DIM),
                               lambda b_, s: (b_, s, 0, 0, 0)),
        out_shape=jax.ShapeDtypeStruct((nb, 2, NSA_KV_HEADS, rows // NSA_KV_HEADS, HEAD_DIM), F32),
        scratch_shapes=[pltpu.VMEM((rows, HEAD_DIM), F32)],
        compiler_params=_params(("arbitrary", "arbitrary")),
        name="compress_pages_finish",
    )(a, b, w2, pet)


def _split3(x):
    hi = x.astype(BF16)
    r1 = x - hi.astype(F32)
    mid = r1.astype(BF16)
    lo = (r1 - mid.astype(F32)).astype(BF16)
    return hi, mid, lo


def _cmp_attn_kernel(slopes_ref, q_ref, kc_ref, vc_ref, ov_ref, o_ref, sel_ref, *, tb, nc, n_sel, pos0, emit_idx):
    kvh = pl.program_id(1)
    i = pl.program_id(2)
    g_count = N_HEADS // NSA_KV_HEADS
    ncp = kc_ref.shape[0]
    qs = _stack_heads(q_ref[...], g_count).astype(BF16)
    kc = kc_ref[...].astype(BF16)
    vc = vc_ref[...].astype(BF16)
    s = lax.dot_general(qs, kc, (((1,), (1,)), ((), ())), preferred_element_type=F32) * SCALE
    t = pos0 + i * tb + lax.broadcasted_iota(jnp.int32, (tb, ncp), 0)
    c = lax.broadcasted_iota(jnp.int32, (tb, ncp), 1)
    valid = (c * CMP_STRIDE + (CMP_LEN - 1) <= t) & (c < nc)
    rel = t.astype(F32) - ((c * CMP_STRIDE).astype(F32) + (CMP_LEN - 1) / 2)
    psum = jnp.zeros((tb, ncp), F32)
    outs = []
    for g in range(g_count):
        sg = s[g * tb:(g + 1) * tb] - slopes_ref[kvh * g_count + g] * rel
        sg = jnp.where(valid, sg, 2 * NEG_FILL)
        m = jnp.maximum(jnp.max(sg, axis=-1, keepdims=True), NEG_FILL)
        p = jnp.exp(sg - m)
        p = p / jnp.maximum(jnp.sum(p, axis=-1, keepdims=True), 1e-30)
        psum = psum + p
        outs.append(jnp.dot(p.astype(BF16), vc, preferred_element_type=F32))
    o_ref[...] = jnp.concatenate(outs, axis=1)

    ov = ov_ref[...]
    if emit_idx:
        shape, t_dim, j_dim = (tb, ov.shape[1]), 0, 1
        imp = sum(jnp.dot(part, ov, preferred_element_type=F32) for part in _split3(psum))
    else:
        shape, t_dim, j_dim = (ov.shape[0], tb), 1, 0
        imp = sum(lax.dot_general(ov, part, (((1,), (1,)), ((), ())), preferred_element_type=F32)
                  for part in _split3(psum))
    tq = pos0 + i * tb + lax.broadcasted_iota(jnp.int32, shape, t_dim)
    j = lax.broadcasted_iota(jnp.int32, shape, j_dim)
    cur = tq // SEL_BLOCK
    forced = (j == 0) | (j == cur) | (j == cur - 1)
    visible = j * SEL_BLOCK <= tq
    rank = jnp.where(forced, 1e9, jnp.where(visible, imp, -1.0))
    rank = jnp.where(j < n_sel, rank, -2.0)
    cnt = jnp.zeros(shape, jnp.int32)
    for jp in range(n_sel):
        one = rank[:, jp:jp + 1] if emit_idx else rank[jp:jp + 1, :]
        beats = (one > rank) | ((one == rank) & (j > jp))
        cnt = cnt + beats.astype(jnp.int32)
    if emit_idx:
        lane = lax.broadcasted_iota(jnp.int32, (tb, LANE), 1)
        idx = jnp.zeros((tb, LANE), jnp.int32)
        jf = j.astype(F32)
        for r in range(SEL_TOPN):
            val = jnp.sum(jnp.where(cnt == r, jf, 0.0), axis=1, keepdims=True).astype(jnp.int32)
            idx = jnp.where(lane == r, val, idx)
        sel_ref[...] = idx
    else:
        chosen = (cnt < SEL_TOPN).astype(F32)
        chosen = jnp.concatenate([chosen, jnp.zeros((LANE - shape[0], tb), F32)], axis=0)
        sel_ref[...] = jnp.transpose(chosen)


def _overlap_matrix(ncp, n_sel, nsp):
    c_start = jnp.arange(ncp, dtype=jnp.int32)[:, None] * CMP_STRIDE
    j = jnp.arange(nsp, dtype=jnp.int32)[None, :]
    ov = jnp.clip(jnp.minimum(c_start + CMP_LEN, (j + 1) * SEL_BLOCK) - jnp.maximum(c_start, j * SEL_BLOCK), 0, None)
    ov = jnp.where(j < n_sel, ov, 0)
    return (ov.astype(F32) / CMP_LEN).astype(BF16)


def cmp_attention_prompt(proj, cmp_kv, slopes, *, batch, seq):
    g_count = N_HEADS // NSA_KV_HEADS
    tb = BAND_BLOCK
    nb = seq // tb
    ncp = cmp_kv.shape[3]
    n_sel = -(-seq // SEL_BLOCK)
    nsp = LANE
    assert tb == LANE and n_sel % 8 == 0
    ov_t = jnp.transpose(_overlap_matrix(ncp, n_sel, nsp))[:n_sel]
    smem = pl.BlockSpec(memory_space=pltpu.SMEM)
    kv_block = (None, None, None, ncp, HEAD_DIM)
    return pl.pallas_call(
        functools.partial(_cmp_attn_kernel, tb=tb, nc=ncp - 1, n_sel=n_sel, pos0=0, emit_idx=False),
        grid=(batch, NSA_KV_HEADS, nb),
        in_specs=[smem,
                  pl.BlockSpec((tb, g_count * HEAD_DIM), lambda b, h, i: (b * nb + i, h)),
                  pl.BlockSpec(kv_block, lambda b, h, i: (b, 0, h, 0, 0)),
                  pl.BlockSpec(kv_block, lambda b, h, i: (b, 1, h, 0, 0)),
                  pl.BlockSpec((n_sel, ncp), lambda b, h, i: (0, 0))],
        out_specs=[pl.BlockSpec((tb, g_count * HEAD_DIM), lambda b, h, i: (b * nb + i, h)),
                   pl.BlockSpec((None, None, tb, nsp), lambda b, h, i: (b, h, i, 0))],
        out_shape=[jax.ShapeDtypeStruct((batch * seq, N_HEADS * HEAD_DIM), F32),
                   jax.ShapeDtypeStruct((batch, NSA_KV_HEADS, seq, nsp), F32)],
        compiler_params=_params(("arbitrary", "arbitrary", "arbitrary")),
        name="cmp_attention_prompt",
    )(slopes, proj, cmp_kv, cmp_kv, ov_t)


def cmp_attention_sample(qkv8, cmp_kv, slopes, *, lk):
    g_count = N_HEADS // NSA_KV_HEADS
    nb = qkv8.shape[0]
    ncp = cmp_kv.shape[3]
    n_sel = -(-lk // SEL_BLOCK)
    nsp = -(-n_sel // LANE) * LANE
    ov = _overlap_matrix(ncp, n_sel, nsp)
    smem = pl.BlockSpec(memory_space=pltpu.SMEM)
    kv_block = (None, None, None, ncp, HEAD_DIM)
    return pl.pallas_call(
        functools.partial(_cmp_attn_kernel, tb=DEC_PAD, nc=ncp - 1, n_sel=n_sel, pos0=PAST_LEN, emit_idx=True),
        grid=(nb, NSA_KV_HEADS, 1),
        in_specs=[smem,
                  pl.BlockSpec((None, DEC_PAD, g_count * HEAD_DIM), lambda b, h, i: (b, 0, h)),
                  pl.BlockSpec(kv_block, lambda b, h, i: (b, 0, h, 0, 0)),
                  pl.BlockSpec(kv_block, lambda b, h, i: (b, 1, h, 0, 0)),
                  pl.BlockSpec((ncp, nsp), lambda b, h, i: (0, 0))],
        out_specs=[pl.BlockSpec((None, DEC_PAD, g_count * HEAD_DIM), lambda b, h, i: (b, 0, h)),
                   pl.BlockSpec((None, None, DEC_PAD, LANE), lambda b, h, i: (b, h, 0, 0))],
        out_shape=[jax.ShapeDtypeStruct((nb, DEC_PAD, N_HEADS * HEAD_DIM), F32),
                   jax.ShapeDtypeStruct((nb, NSA_KV_HEADS, DEC_PAD, LANE), jnp.int32)],
        compiler_params=_params(("arbitrary", "arbitrary", "arbitrary")),
        name="cmp_attention_sample",
    )(slopes, qkv8, cmp_kv, cmp_kv, ov)


SEL_KEY_TILE = 256

def _sel_prompt_kernel(slopes_ref, q_ref, k_ref, v_ref, sel_ref, ex_ref, oc_ref, ow_ref, gl_ref, gb_ref, o_ref,
                       m_ref, l_ref, acc_ref, chosen_ref):
    kvh = pl.program_id(1)
    i = pl.program_id(2)
    g_count = N_HEADS // NSA_KV_HEADS
    tb = BAND_BLOCK
    qs = _stack_heads(q_ref[...], g_count).astype(BF16)
    sel = sel_ref[...].astype(BF16)
    slope_col = jnp.concatenate(
        [jnp.full((tb, 1), slopes_ref[kvh * g_count + g], F32) for g in range(g_count)], axis=0)
    kt_w = SEL_KEY_TILE
    m_ref[...] = jnp.full(m_ref.shape, NEG_FILL, F32)
    l_ref[...] = jnp.zeros(l_ref.shape, F32)
    acc_ref[...] = jnp.zeros(acc_ref.shape, F32)
    for kt in range(chosen_ref.shape[0]):
        chosen_ref[kt] = jnp.dot(sel, ex_ref[:, kt * kt_w:(kt + 1) * kt_w], preferred_element_type=F32)
    tq = i * tb + lax.broadcasted_iota(jnp.int32, (tb, kt_w), 0)
    kk = lax.broadcasted_iota(jnp.int32, (tb, kt_w), 1)

    def attend(k0, dist):
        ks = k_ref[pl.ds(k0, kt_w), :].astype(BF16)
        vs = v_ref[pl.ds(k0, kt_w), :].astype(BF16)
        s = lax.dot_general(qs, ks, (((1,), (1,)), ((), ())), preferred_element_type=F32) * SCALE
        distf = jnp.concatenate([dist] * g_count, axis=0)
        s = jnp.where(distf >= 0.0, s - slope_col * distf, 2 * NEG_FILL)
        m_old = m_ref[...]
        m_new = jnp.maximum(m_old, jnp.max(s, axis=-1, keepdims=True))
        alpha = jnp.exp(m_old - m_new)
        p = jnp.exp(s - jnp.concatenate([m_new] * (kt_w // LANE), axis=1))
        l_ref[...] = alpha * l_ref[...] + jnp.sum(p, axis=-1, keepdims=True)
        acc_ref[...] = alpha * acc_ref[...] + jnp.dot(p.astype(BF16), vs, preferred_element_type=F32)
        m_ref[...] = m_new

    def body(kt, carry):
        k0 = pl.multiple_of(kt * kt_w, kt_w)
        dist = (tq - (k0 + kk)).astype(F32)
        dist = jnp.where(chosen_ref[kt] > 0.5, dist, -1.0)

        @pl.when(jnp.max(dist) >= 0.0)
        def _():
            attend(k0, dist)

        return carry

    n_tiles = (i * tb + tb + kt_w - 1) // kt_w
    lax.fori_loop(0, n_tiles, body, 0)
    o_sel = acc_ref[...] / jnp.maximum(l_ref[...], 1e-30)
    gate = jax.nn.sigmoid(gl_ref[...] + gb_ref[...])
    gate = pltpu.roll(gate, jnp.where(kvh == 0, 0, LANE - kvh * g_count), 1)
    for g in range(g_count):
        sl = slice(g * HEAD_DIM, (g + 1) * HEAD_DIM)
        og = (gate[:, g:g + 1] * oc_ref[:, sl] + gate[:, N_HEADS + g:N_HEADS + g + 1] * o_sel[g * tb:(g + 1) * tb]
              + gate[:, 2 * N_HEADS + g:2 * N_HEADS + g + 1] * ow_ref[:, sl])
        o_ref[:, sl] = og.astype(o_ref.dtype)


def sel_attention_prompt(proj, sel_mask, slopes, o_cmp, o_win, gate_logits, gate_bias, *, batch, seq, k_col0, v_col0):
    g_count = N_HEADS // NSA_KV_HEADS
    tb = BAND_BLOCK
    nb = seq // tb
    kb0, vb0 = k_col0 // HEAD_DIM, v_col0 // HEAD_DIM
    smem = pl.BlockSpec(memory_space=pltpu.SMEM)
    block_of_key = jnp.arange(seq, dtype=jnp.int32)[None, :] // SEL_BLOCK
    expand = (jnp.arange(LANE, dtype=jnp.int32)[:, None] == block_of_key).astype(BF16)
    group_tile = pl.BlockSpec((tb, g_count * HEAD_DIM), lambda b, h, i: (b * nb + i, h))
    return pl.pallas_call(
        _sel_prompt_kernel,
        grid=(batch, NSA_KV_HEADS, nb),
        in_specs=[smem,
                  group_tile,
                  pl.BlockSpec((seq, HEAD_DIM), lambda b, h, i: (b, kb0 + h)),
                  pl.BlockSpec((seq, HEAD_DIM), lambda b, h, i: (b, vb0 + h)),
                  pl.BlockSpec((None, None, tb, LANE), lambda b, h, i: (b, h, i, 0)),
                  pl.BlockSpec((LANE, seq), lambda b, h, i: (0, 0)),
                  group_tile,
                  group_tile,
                  pl.BlockSpec((tb, LANE), lambda b, h, i: (b * nb + i, 0)),
                  pl.BlockSpec((1, LANE), lambda b, h, i: (0, 0))],
        out_specs=group_tile,
        out_shape=jax.ShapeDtypeStruct((batch * seq, N_HEADS * HEAD_DIM), BF16),
        scratch_shapes=[pltpu.VMEM((g_count * tb, HEAD_DIM), F32)] * 3
        + [pltpu.VMEM((seq // SEL_KEY_TILE, tb, SEL_KEY_TILE), F32)],
        compiler_params=_params(("arbitrary", "arbitrary", "arbitrary")),
        name="sel_attention_prompt",
    )(slopes, proj, proj, proj, sel_mask, expand, o_cmp, o_win, gate_logits, gate_bias)


def _sel_sample_kernel(idx_ref, pt_ref, q_ref, *refs, n_past_blocks):
    blocks = refs[:SEL_TOPN]
    kn_ref, vn_ref, slope_ref, o_ref = refs[SEL_TOPN:]
    b, kvh, t = pl.program_id(0), pl.program_id(1), pl.program_id(2)
    base = ((b * NSA_KV_HEADS + kvh) * DEC_SEQ + t) * SEL_TOPN
    g_count = N_HEADS // NSA_KV_HEADS
    pad = jnp.zeros((LANE - DEC_PAD, HEAD_DIM), F32)
    kall = jnp.concatenate([r[:, kvh, :] for r in blocks] + [kn_ref[...], pad], axis=0).astype(BF16)
    vall = jnp.concatenate([r[:, NSA_KV_HEADS + kvh, :] for r in blocks] + [vn_ref[...], pad], axis=0).astype(BF16)
    q = q_ref[...].astype(BF16)
    s = lax.dot_general(q, kall, (((1,), (1,)), ((), ())), preferred_element_type=F32) * SCALE
    lane = lax.broadcasted_iota(jnp.int32, (g_count, LANE), 1)
    low = lane < SEL_BLOCK
    qpos = PAST_LEN + t
    pos_parts, ok_parts = [], []
    for c in range(SEL_TOPN // 2):
        b0 = idx_ref[base + 2 * c]
        b1 = idx_ref[base + 2 * c + 1]
        p0 = jnp.where(b0 < n_past_blocks, b0 * SEL_BLOCK, PAST_LEN + DEC_SEQ)
        p1 = jnp.where(b1 < n_past_blocks, b1 * SEL_BLOCK, PAST_LEN + DEC_SEQ)
        pos_parts.append(jnp.where(low, p0 + lane, p1 + (lane - SEL_BLOCK)))
    pos_parts.append(PAST_LEN + lane)
    dist = qpos - jnp.concatenate(pos_parts, axis=1)
    valid = dist >= 0
    slope = jnp.concatenate([slope_ref[...]] * (SEL_TOPN // 2 + 1), axis=1)
    o_ref[...] = _softmax_pv(s - slope * dist.astype(F32), valid, vall)


def sel_attention_sample(q_s, pool_halves, page_table, sel_idx, new_kv8, slope_rows, layer_half0):
    nb = q_s.shape[0]
    g_count = N_HEADS // NSA_KV_HEADS
    n_past_blocks = PAST_LEN // SEL_BLOCK
    per_page = PAGE_SIZE // SEL_BLOCK
    page_shift = per_page.bit_length() - 1
    assert per_page == 1 << page_shift

    def blk_spec(r):
        def index(b, h, t, idx, pt):
            blk = jnp.minimum(idx[((b * NSA_KV_HEADS + h) * DEC_SEQ + t) * SEL_TOPN + r], n_past_blocks - 1)
            page = pt[b, lax.shift_right_logical(blk, page_shift)]
            return (layer_half0 + page * per_page + jnp.bitwise_and(blk, per_page - 1), 0, 1, 0)
        return pl.BlockSpec((None, SEL_BLOCK, CMP_HEADS, HEAD_DIM), index)

    q_block = (None, None, None, g_count, HEAD_DIM)
    new_block = (None, None, None, DEC_PAD, HEAD_DIM)
    grid_spec = pltpu.PrefetchScalarGridSpec(
        num_scalar_prefetch=2,
        grid=(nb, NSA_KV_HEADS, DEC_SEQ),
        in_specs=[pl.BlockSpec(q_block, lambda b, h, t, idx, pt: (b, t, h, 0, 0))]
        + [blk_spec(r) for r in range(SEL_TOPN)]
        + [pl.BlockSpec(new_block, lambda b, h, t, idx, pt: (b, 2, h, 0, 0)),
           pl.BlockSpec(new_block, lambda b, h, t, idx, pt: (b, 3, h, 0, 0)),
           pl.BlockSpec((None, g_count, LANE), lambda b, h, t, idx, pt: (h, 0, 0))],
        out_specs=pl.BlockSpec(q_block, lambda b, h, t, idx, pt: (b, t, h, 0, 0)),
    )
    return pl.pallas_call(
        functools.partial(_sel_sample_kernel, n_past_blocks=n_past_blocks),
        grid_spec=grid_spec,
        out_shape=jax.ShapeDtypeStruct(q_s.shape, F32),
        compiler_params=_params(("arbitrary", "arbitrary", "arbitrary")),
        name="sel_attention_sample",
    )(sel_idx, page_table, q_s, *([pool_halves] * SEL_TOPN), new_kv8, new_kv8, slope_rows)


def _combine_kernel(oc_ref, os_ref, ow_ref, gl_ref, gb_ref, o_ref):
    gate = jax.nn.sigmoid(gl_ref[...] + gb_ref[...])
    for h in range(N_HEADS):
        sl = slice(h * HEAD_DIM, (h + 1) * HEAD_DIM)
        o = (gate[:, h:h + 1] * oc_ref[:, sl] + gate[:, N_HEADS + h:N_HEADS + h + 1] * os_ref[:, sl]
             + gate[:, 2 * N_HEADS + h:2 * N_HEADS + h + 1] * ow_ref[:, sl])
        o_ref[:, sl] = o.astype(o_ref.dtype)


def gate_combine(o_c, o_s, o_w, gate_logits, gate_bias, tm, row_block0=0):
    m, n = o_c.shape
    big = pl.BlockSpec((tm, n), lambda i: (i, 0))
    return pl.pallas_call(
        _combine_kernel,
        grid=(m // tm,),
        in_specs=[big, big, big,
                  pl.BlockSpec((tm, LANE), lambda i: (row_block0 + i, 0)),
                  pl.BlockSpec((1, LANE), lambda i: (0, 0))],
        out_specs=big,
        out_shape=jax.ShapeDtypeStruct((m, n), BF16),
        compiler_params=_params(("arbitrary",)),
        name="gate_combine",
    )(o_c, o_s, o_w, gate_logits, gate_bias)


TM_DENSE = 1664
TN_PROJ = 256
TF_FFN = 256
TM_FFN_OUT = 832
TN_FFN_OUT = 256
TM_NORM = 320
TM_PROMPT = 256
NQ = N_HEADS * HEAD_DIM


def _ffn_block(x, g, w_in, w_out_bf, layer, sub):
    xn = rms_norm(x, g, BF16, TM_NORM)
    h = ffn_in(xn, w_in, layer, sub, TM_DENSE, TF_FFN)
    return ffn_out(h, w_out_bf, layer, sub, x, TM_FFN_OUT, TN_FFN_OUT)


def _sample_rows8(proj):
    s = proj[M_PROMPT:].reshape(DEC_BATCH, DEC_SEQ, proj.shape[1])
    return jnp.pad(s, ((0, 0), (0, DEC_PAD - DEC_SEQ), (0, 0)))


def _swa_layer(x, g, w_in, w_out, sinks_all, cache_all, li, slopes):
    xn = rms_norm(x, g, BF16, TM_NORM)
    kv_cols = 2 * SWA_KV_HEADS * HEAD_DIM
    proj = matmul_wres(xn, w_in, (li,), NQ + kv_cols, TM_DENSE, TN_PROJ)
    sinks = sinks_all[li].astype(F32)
    o_p = band_attention(proj, slopes, sinks, batch=BATCH, seq=SEQ, kv_heads=SWA_KV_HEADS, k_col0=NQ,
                         v_col0=NQ + SWA_KV_HEADS * HEAD_DIM, window=SWA_WINDOW, use_sink=True, out_dtype=BF16)
    cache = cache_all[li]
    lc = cache.shape[1]
    o_s = decode_attention(_sample_rows8(proj), cache.reshape(DEC_BATCH, lc * 2 * SWA_KV_HEADS, HEAD_DIM), slopes, sinks,
                           kv_heads=SWA_KV_HEADS, k_col0=NQ, v_col0=NQ + SWA_KV_HEADS * HEAD_DIM,
                           window=SWA_WINDOW, use_sink=True, out_dtype=BF16)
    attn = jnp.concatenate([o_p, o_s[:, :DEC_SEQ].reshape(M_SAMPLE, NQ)], axis=0)
    x = matmul_wres(attn, w_out, (li,), D_MODEL, TM_DENSE, TN_PROJ, res=x)
    kv_p = proj[:M_PROMPT, NQ:].reshape(BATCH, SEQ, 2, SWA_KV_HEADS, HEAD_DIM)
    kv_s = proj[M_PROMPT:, NQ:].reshape(DEC_BATCH, DEC_SEQ, 2, SWA_KV_HEADS, HEAD_DIM)
    buf_p = kv_p[:, SEQ - min(SWA_WINDOW, SEQ):]
    buf_s = jnp.concatenate([cache, kv_s], axis=1)[:, DEC_SEQ:]
    return x, buf_p, buf_s


def _nsa_layer(x, g, w_in, gate_b, cmp_pe, cmp_w1, cmp_w2, w_out, win_cache_all, pool_all, page_table, li, slopes):
    xn = rms_norm(x, g, BF16, TM_NORM)
    kvw = NSA_KV_HEADS * HEAD_DIM
    main_cols = NQ + 6 * kvw
    proj = matmul_wres(xn, w_in, (li,), main_cols, TM_DENSE, TN_PROJ)
    n_gate = 3 * N_HEADS
    w_gate = jnp.pad(w_in[li][:, main_cols:], ((0, 0), (0, LANE - n_gate)))
    gate_logits = matmul_wres(xn, w_gate, (), LANE, TM_DENSE, LANE)
    gate_bias = jnp.pad(gate_b[li], (0, LANE - n_gate)).reshape(1, LANE)
    zeros = jnp.zeros((N_HEADS,), F32)

    pet = pe_term(cmp_pe, cmp_w1, li)
    w1 = cmp_w1[li].astype(BF16)
    top = jnp.concatenate([w1[:, 0:CMP_STRIDE:2], w1[:, CMP_STRIDE::2]], axis=-1)
    bot = jnp.concatenate([w1[:, 1:CMP_STRIDE:2], w1[:, CMP_STRIDE + 1::2]], axis=-1)
    w1_pairs = jnp.concatenate([top, bot], axis=-2)

    o_w_p = band_attention(proj, slopes, zeros, batch=BATCH, seq=SEQ, kv_heads=NSA_KV_HEADS, k_col0=NQ + 4 * kvw,
                           v_col0=NQ + 5 * kvw, window=NSA_WINDOW, use_sink=False, out_dtype=F32)
    cmp_p = compress_prompt(proj, cmp_w1, cmp_w2, pet, li, batch=BATCH, seq=SEQ, col0=NQ)
    o_c_p, sel_mask = cmp_attention_prompt(proj, cmp_p, slopes, batch=BATCH, seq=SEQ)
    attn_p = sel_attention_prompt(proj, sel_mask, slopes, o_c_p, o_w_p, gate_logits, gate_bias, batch=BATCH, seq=SEQ,
                                  k_col0=NQ + 2 * kvw, v_col0=NQ + 3 * kvw)

    qkv8 = _sample_rows8(proj)
    win_cache = win_cache_all[li]
    lc = win_cache.shape[1]
    o_w_s = decode_attention(qkv8, win_cache.reshape(DEC_BATCH, lc * 2 * NSA_KV_HEADS, HEAD_DIM), slopes, zeros,
                             kv_heads=NSA_KV_HEADS, k_col0=NQ + 4 * kvw, v_col0=NQ + 5 * kvw, window=NSA_WINDOW,
                             use_sink=False, out_dtype=F32)
    n_pool = pool_all.shape[1]
    pool = pool_all.reshape(pool_all.shape[0] * n_pool, PAGE_SIZE, 4 * kvw)
    pool4 = pool.reshape(pool.shape[0], PAGE_SIZE, 2 * CMP_HEADS, HEAD_DIM)
    part_a, part_b = compress_pages_partial(pool4, page_table, w1_pairs, li * n_pool)
    cmp_s = compress_pages_finish(part_a, part_b, cmp_w2, pet, li)
    o_c_s, sel_idx = cmp_attention_sample(qkv8, cmp_s, slopes, lk=PAST_LEN + DEC_SEQ)
    sel_flat = sel_idx[:, :, :DEC_SEQ, :SEL_TOPN].reshape(-1)
    rows_s = proj[M_PROMPT:, NQ:NQ + 4 * kvw].reshape(DEC_BATCH, DEC_SEQ, 4, NSA_KV_HEADS, HEAD_DIM)
    new_kv8 = jnp.pad(rows_s.transpose(0, 2, 3, 1, 4), ((0, 0), (0, 0), (0, 0), (0, DEC_PAD - DEC_SEQ), (0, 0)))
    g_count = N_HEADS // NSA_KV_HEADS
    q_s = proj[M_PROMPT:, :NQ].reshape(DEC_BATCH, DEC_SEQ, NSA_KV_HEADS, g_count, HEAD_DIM)
    slope_rows = jnp.broadcast_to(slopes.reshape(NSA_KV_HEADS, g_count, 1), (NSA_KV_HEADS, g_count, LANE))
    per_page = PAGE_SIZE // SEL_BLOCK
    pool_halves = pool.reshape(pool.shape[0] * per_page, SEL_BLOCK, 2 * CMP_HEADS, HEAD_DIM)
    o_s_s = sel_attention_sample(q_s, pool_halves, page_table, sel_flat, new_kv8, slope_rows, li * n_pool * per_page)
    attn_s = gate_combine(o_c_s[:, :DEC_SEQ].reshape(M_SAMPLE, NQ), o_s_s.reshape(M_SAMPLE, NQ),
                          o_w_s[:, :DEC_SEQ].reshape(M_SAMPLE, NQ), gate_logits, gate_bias, M_SAMPLE,
                          row_block0=M_PROMPT // M_SAMPLE)

    attn = jnp.concatenate([attn_p, attn_s], axis=0)
    x = matmul_wres(attn, w_out, (li,), D_MODEL, TM_DENSE, TN_PROJ, res=x)

    kv_p = proj[:M_PROMPT, NQ:main_cols].reshape(BATCH, SEQ, 6, NSA_KV_HEADS, HEAD_DIM)
    kv_s = proj[M_PROMPT:, NQ:main_cols].reshape(DEC_BATCH, DEC_SEQ, 6, NSA_KV_HEADS, HEAD_DIM)
    win_p = kv_p[:, SEQ - min(NSA_WINDOW, SEQ):, 4:]
    win_s = jnp.concatenate([win_cache, kv_s[:, :, 4:]], axis=1)[:, DEC_SEQ:]
    return x, kv_p[:, :, :4], kv_s[:, :, :4], win_p, win_s


def kernel(x_prompt, x_sample, cache_swa_kv, cache_nsa_win_kv, cache_nsa_kv, page_table, norm_g, final_norm_g,
           ffn_w_in, ffn_w_out, swa_w_in, swa_w_out, swa_sinks, nsa_w_in, nsa_gate_b, nsa_cmp_pe, nsa_cmp_w1,
           nsa_cmp_w2, nsa_w_out):
    x = jnp.concatenate([x_prompt.reshape(M_PROMPT, D_MODEL), x_sample.reshape(M_SAMPLE, D_MODEL)], axis=0)
    slopes = jnp.exp2(-8.0 * jnp.arange(1, N_HEADS + 1, dtype=F32) / N_HEADS)
    w_out_bf = ffn_w_out.astype(BF16)
    swa_p, swa_s, win_p, win_s, kv_p, kv_s = [], [], [], [], [], []
    for i in range(DEPTH):
        x = _ffn_block(x, norm_g[i, 0], ffn_w_in, w_out_bf, i, 0)
        li = i // N_MIXERS
        if i % N_MIXERS == 0:
            x, bp, bs = _swa_layer(x, norm_g[i, 1], swa_w_in, swa_w_out, swa_sinks, cache_swa_kv, li, slopes)
            swa_p.append(bp)
            swa_s.append(bs)
        else:
            x, rp, rs, wp, ws = _nsa_layer(x, norm_g[i, 1], nsa_w_in, nsa_gate_b, nsa_cmp_pe, nsa_cmp_w1,
                                           nsa_cmp_w2, nsa_w_out, cache_nsa_win_kv, cache_nsa_kv, page_table, li,
                                           slopes)
            kv_p.append(rp)
            kv_s.append(rs)
            win_p.append(wp)
            win_s.append(ws)
        x = _ffn_block(x, norm_g[i, 2], ffn_w_in, w_out_bf, i, 1)
    y_p = rms_norm(x, final_norm_g, F32, TM_PROMPT, rows=M_PROMPT).reshape(BATCH, SEQ, D_MODEL)
    y_s = rms_norm(x, final_norm_g, F32, M_SAMPLE, row_block0=M_PROMPT // M_SAMPLE, rows=M_SAMPLE)
    return (y_p, y_s.reshape(DEC_BATCH, DEC_SEQ, D_MODEL), jnp.stack(swa_p), jnp.stack(swa_s), jnp.stack(win_p),
            jnp.stack(win_s), jnp.stack(kv_p), jnp.stack(kv_s))
```

```python
import functools
import math

import jax
import jax.numpy as jnp
from jax import lax
from jax.experimental import pallas as pl
from jax.experimental.pallas import tpu as pltpu

D_MODEL = 4096
BATCH = 4
SEQ = 2048
DEPTH = 2
DEC_BATCH = 32
DEC_SEQ = 4
PAST_LEN = 16384
PAGE_SIZE = 128
N_HEADS = 32
HEAD_DIM = 128
SWA_KV_HEADS = 8
SWA_WINDOW = 128
NSA_KV_HEADS = 4
CMP_STRIDE = 16
CMP_LEN = 32
SEL_BLOCK = 64
SEL_TOPN = 16
NSA_WINDOW = 512
D_FF = 11008
BAND_BLOCK = 128
N_MIXERS = 2
RMS_EPS = 1e-6
NEG_FILL = -1e30
SCALE = HEAD_DIM ** -0.5

M_PROMPT = BATCH * SEQ
M_SAMPLE = DEC_BATCH * DEC_SEQ
M_ALL = M_PROMPT + M_SAMPLE
DEC_PAD = 8
LANE = 128
F32 = jnp.float32
BF16 = jnp.bfloat16
VMEM_LIMIT = 60 * 1024 * 1024


def _params(sem):
    return pltpu.CompilerParams(dimension_semantics=sem, vmem_limit_bytes=VMEM_LIMIT)


def _rms_kernel(x_ref, g_ref, o_ref):
    x = x_ref[...]
    y = x * lax.rsqrt(jnp.mean(x * x, axis=-1, keepdims=True) + RMS_EPS)
    o_ref[...] = (y * g_ref[...]).astype(o_ref.dtype)


def rms_norm(x, g, out_dtype, tm, row_block0=0, rows=None):
    m, d = x.shape
    rows = m if rows is None else rows
    return pl.pallas_call(
        _rms_kernel,
        grid=(rows // tm,),
        in_specs=[pl.BlockSpec((tm, d), lambda i: (row_block0 + i, 0)),
                  pl.BlockSpec((1, d), lambda i: (0, 0))],
        out_specs=pl.BlockSpec((tm, d), lambda i: (i, 0)),
        out_shape=jax.ShapeDtypeStruct((rows, d), out_dtype),
        compiler_params=_params(("arbitrary",)),
        name="rms_norm",
    )(x, g.reshape(1, d))


def _row_tile_spec(tm, k):
    return pl.BlockSpec((tm, k), lambda i, j: (i, 0))


def _mm_kernel(x_ref, w_ref, *rest, scale, has_res):
    acc = jnp.dot(x_ref[...], w_ref[...].astype(BF16), preferred_element_type=F32)
    if has_res:
        res_ref, o_ref = rest
        o_ref[...] = res_ref[...] + (acc if scale == 1.0 else scale * acc)
    else:
        (o_ref,) = rest
        o_ref[...] = acc.astype(o_ref.dtype)


def matmul_wres(x, w, w_prefix, n_cols, tm, tn, res=None, scale=1.0, col_block0=0):
    m, k = x.shape
    npre = len(w_prefix)
    w_block = (None,) * npre + (k, tn)
    in_specs = [_row_tile_spec(tm, k),
                pl.BlockSpec(w_block, lambda i, j: tuple(w_prefix) + (0, col_block0 + j))]
    args = [x, w]
    if res is not None:
        in_specs.append(pl.BlockSpec((tm, tn), lambda i, j: (i, j)))
        args.append(res)
    return pl.pallas_call(
        functools.partial(_mm_kernel, scale=scale, has_res=res is not None),
        grid=(m // tm, n_cols // tn),
        in_specs=in_specs,
        out_specs=pl.BlockSpec((tm, tn), lambda i, j: (i, j)),
        out_shape=jax.ShapeDtypeStruct((m, n_cols), F32),
        compiler_params=_params(("arbitrary", "arbitrary")),
        name="matmul_wres",
    )(*args)


def _ffn_in_kernel(x_ref, wg_ref, wu_ref, o_ref):
    x = x_ref[...]
    g = jnp.dot(x, wg_ref[...].astype(BF16), preferred_element_type=F32)
    u = jnp.dot(x, wu_ref[...].astype(BF16), preferred_element_type=F32)
    o_ref[...] = (jax.nn.silu(g) * u).astype(o_ref.dtype)


def ffn_in(xn, w_in, layer, sub, tm, tf):
    m, k = xn.shape
    nf = D_FF // tf
    w_block = (None, None, k, tf)
    return pl.pallas_call(
        _ffn_in_kernel,
        grid=(m // tm, nf),
        in_specs=[_row_tile_spec(tm, k),
                  pl.BlockSpec(w_block, lambda i, j: (layer, sub, 0, j)),
                  pl.BlockSpec(w_block, lambda i, j: (layer, sub, 0, nf + j))],
        out_specs=pl.BlockSpec((tm, tf), lambda i, j: (i, j)),
        out_shape=jax.ShapeDtypeStruct((m, D_FF), BF16),
        compiler_params=_params(("arbitrary", "arbitrary")),
        name="ffn_in",
    )(xn, w_in, w_in)


def _ffn_out_kernel(h_ref, w_ref, res_ref, o_ref):
    acc = jnp.dot(h_ref[...], w_ref[...], preferred_element_type=F32)
    o_ref[...] = res_ref[...] + 0.5 * acc


def ffn_out(h, w_out_bf, layer, sub, res, tm, tn):
    m, k = h.shape
    n = res.shape[1]
    return pl.pallas_call(
        _ffn_out_kernel,
        grid=(m // tm, n // tn),
        in_specs=[_row_tile_spec(tm, k),
                  pl.BlockSpec((None, None, k, tn), lambda i, j: (layer, sub, 0, j)),
                  pl.BlockSpec((tm, tn), lambda i, j: (i, j))],
        out_specs=pl.BlockSpec((tm, tn), lambda i, j: (i, j)),
        out_shape=jax.ShapeDtypeStruct((m, n), F32),
        compiler_params=_params(("arbitrary", "arbitrary")),
        name="ffn_out",
    )(h, w_out_bf, res)


def _stack_heads(q, g_count):
    return jnp.concatenate([q[:, g * HEAD_DIM:(g + 1) * HEAD_DIM] for g in range(g_count)], axis=0)


def _softmax_pv(s, valid, v_bf, sink=None):
    s = jnp.where(valid, s, 2 * NEG_FILL)
    m = jnp.maximum(jnp.max(s, axis=-1, keepdims=True), NEG_FILL)
    if sink is not None:
        m = jnp.maximum(m, sink)
    p = jnp.exp(s - m)
    den = jnp.sum(p, axis=-1, keepdims=True)
    if sink is not None:
        den = den + jnp.exp(sink - m)
    o = jnp.dot(p.astype(BF16), v_bf, preferred_element_type=F32)
    return o / jnp.maximum(den, 1e-30)


def _band_kernel(slopes_ref, sinks_ref, q_ref, k_ref, v_ref, o_ref, *, g_count, nprev, window, use_sink, kv_per_step):
    i = pl.program_id(2)
    width = (nprev + 1) * BAND_BLOCK
    start = pl.multiple_of(jnp.maximum(i - nprev, 0) * BAND_BLOCK, BAND_BLOCK)
    tq = i * BAND_BLOCK + lax.broadcasted_iota(jnp.int32, (BAND_BLOCK, width), 0)
    kp = start + lax.broadcasted_iota(jnp.int32, (BAND_BLOCK, width), 1)
    dist = tq - kp
    valid = (dist >= 0) & (dist <= window)
    distf = dist.astype(F32)
    group = g_count * HEAD_DIM
    for r in range(kv_per_step):
        kvh = pl.program_id(1) * kv_per_step + r
        k = k_ref[pl.ds(start, width), r * HEAD_DIM:(r + 1) * HEAD_DIM].astype(BF16)
        v = v_ref[pl.ds(start, width), r * HEAD_DIM:(r + 1) * HEAD_DIM].astype(BF16)
        qs = _stack_heads(q_ref[:, r * group:(r + 1) * group], g_count).astype(BF16)
        s = lax.dot_general(qs, k, (((1,), (1,)), ((), ())), preferred_element_type=F32) * SCALE
        for g in range(g_count):
            h = kvh * g_count + g
            sg = s[g * BAND_BLOCK:(g + 1) * BAND_BLOCK] - slopes_ref[h] * distf
            o = _softmax_pv(sg, valid, v, sinks_ref[h] if use_sink else None)
            o_ref[:, r * group + g * HEAD_DIM:r * group + (g + 1) * HEAD_DIM] = o.astype(o_ref.dtype)


def band_attention(proj, slopes, sinks, *, batch, seq, kv_heads, k_col0, v_col0, window, use_sink, out_dtype):
    g_count = N_HEADS // kv_heads
    nb = seq // BAND_BLOCK
    nprev = -(-window // BAND_BLOCK)
    kv_per_step = max(1, 8 // g_count)
    cols = kv_per_step * HEAD_DIM
    assert kv_heads % kv_per_step == 0 and k_col0 % cols == 0 and v_col0 % cols == 0
    kb0, vb0 = k_col0 // cols, v_col0 // cols
    smem = pl.BlockSpec(memory_space=pltpu.SMEM)
    q_tile = pl.BlockSpec((BAND_BLOCK, kv_per_step * g_count * HEAD_DIM), lambda b, h, i: (b * nb + i, h))
    return pl.pallas_call(
        functools.partial(_band_kernel, g_count=g_count, nprev=nprev, window=window, use_sink=use_sink,
                          kv_per_step=kv_per_step),
        grid=(batch, kv_heads // kv_per_step, nb),
        in_specs=[smem, smem,
                  q_tile,
                  pl.BlockSpec((seq, cols), lambda b, h, i: (b, kb0 + h)),
                  pl.BlockSpec((seq, cols), lambda b, h, i: (b, vb0 + h))],
        out_specs=q_tile,
        out_shape=jax.ShapeDtypeStruct((batch * seq, N_HEADS * HEAD_DIM), out_dtype),
        compiler_params=_params(("arbitrary", "arbitrary", "arbitrary")),
        name="band_attention",
    )(slopes, sinks, proj, proj, proj)


def _dec_kernel(slopes_ref, sinks_ref, qkv_ref, c_ref, o_ref, *, kv_heads, g_count, lc, k_col0, v_col0,
                window, use_sink):
    rows = g_count * DEC_PAD
    width = lc + LANE
    t = lax.broadcasted_iota(jnp.int32, (DEC_PAD, width), 0)
    col = lax.broadcasted_iota(jnp.int32, (DEC_PAD, width), 1)
    dist = jnp.where(col < lc, t + (lc - col), t - (col - lc))
    valid = (dist >= 0) & (dist <= window) & (col < lc + DEC_SEQ)
    distf = dist.astype(F32)
    pad = jnp.zeros((LANE - DEC_PAD, HEAD_DIM), F32)
    for kvh in range(kv_heads):
        q0 = kvh * g_count * HEAD_DIM
        qs = _stack_heads(qkv_ref[:, q0:q0 + g_count * HEAD_DIM], g_count).astype(BF16)
        kn = qkv_ref[:, k_col0 + kvh * HEAD_DIM:k_col0 + (kvh + 1) * HEAD_DIM]
        vn = qkv_ref[:, v_col0 + kvh * HEAD_DIM:v_col0 + (kvh + 1) * HEAD_DIM]
        kc = c_ref[pl.ds(kvh, lc, stride=2 * kv_heads), :]
        vc = c_ref[pl.ds(kv_heads + kvh, lc, stride=2 * kv_heads), :]
        kall = jnp.concatenate([kc, kn, pad], axis=0).astype(BF16)
        vall = jnp.concatenate([vc, vn, pad], axis=0).astype(BF16)
        s = lax.dot_general(qs, kall, (((1,), (1,)), ((), ())), preferred_element_type=F32) * SCALE
        assert s.shape == (rows, width)
        for g in range(g_count):
            h = kvh * g_count + g
            sg = s[g * DEC_PAD:(g + 1) * DEC_PAD] - slopes_ref[h] * distf
            o = _softmax_pv(sg, valid, vall, sinks_ref[h] if use_sink else None)
            o_ref[:, h * HEAD_DIM:(h + 1) * HEAD_DIM] = o.astype(o_ref.dtype)


def decode_attention(qkv8, cache, slopes, sinks, *, kv_heads, k_col0, v_col0, window, use_sink, out_dtype):
    nb, _, ncols = qkv8.shape
    lc = cache.shape[1] // (2 * kv_heads)
    g_count = N_HEADS // kv_heads
    smem = pl.BlockSpec(memory_space=pltpu.SMEM)
    return pl.pallas_call(
        functools.partial(_dec_kernel, kv_heads=kv_heads, g_count=g_count, lc=lc, k_col0=k_col0, v_col0=v_col0,
                          window=window, use_sink=use_sink),
        grid=(nb,),
        in_specs=[smem, smem,
                  pl.BlockSpec((None, DEC_PAD, ncols), lambda b: (b, 0, 0)),
                  pl.BlockSpec((None, cache.shape[1], HEAD_DIM), lambda b: (b, 0, 0))],
        out_specs=pl.BlockSpec((None, DEC_PAD, N_HEADS * HEAD_DIM), lambda b: (b, 0, 0)),
        out_shape=jax.ShapeDtypeStruct((nb, DEC_PAD, N_HEADS * HEAD_DIM), out_dtype),
        compiler_params=_params(("arbitrary",)),
        name="decode_attention",
    )(slopes, sinks, qkv8, cache)


def _pe_term_kernel(pe_ref, w1_ref, o_ref):
    acc = jnp.zeros((8, HEAD_DIM), F32)
    for l in range(CMP_LEN):
        row = jnp.broadcast_to(pe_ref[l:l + 1, :], (8, HEAD_DIM)).astype(BF16)
        acc = acc + jnp.dot(row, w1_ref[l].astype(BF16), preferred_element_type=F32)
    o_ref[...] = acc


def pe_term(pe, w1, layer):
    return pl.pallas_call(
        _pe_term_kernel,
        grid=(2,),
        in_specs=[pl.BlockSpec((None, None, CMP_LEN, HEAD_DIM), lambda s: (layer, s, 0, 0)),
                  pl.BlockSpec((None, None, CMP_LEN, HEAD_DIM, HEAD_DIM), lambda s: (layer, s, 0, 0, 0))],
        out_specs=pl.BlockSpec((None, 8, HEAD_DIM), lambda s: (s, 0, 0)),
        out_shape=jax.ShapeDtypeStruct((2, 8, HEAD_DIM), F32),
        compiler_params=_params(("arbitrary",)),
        name="pe_term",
    )(pe, w1)


def _compress_finish(acc_a, acc_b, pet_row, w2_bf):
    n = acc_a.shape[0]
    pre = acc_a + pltpu.roll(acc_b, n - 1, 0) + pet_row
    return jnp.dot(jax.nn.gelu(pre).astype(BF16), w2_bf, preferred_element_type=F32)


def _cmp_prompt_kernel(x_ref, w1_ref, w2_ref, pet_ref, o_ref, *, nch):
    acc_a = jnp.zeros((nch, HEAD_DIM), F32)
    acc_b = jnp.zeros((nch, HEAD_DIM), F32)
    for l in range(CMP_STRIDE):
        xl = x_ref[pl.ds(l, nch, stride=CMP_STRIDE), :].astype(BF16)
        acc_a = acc_a + jnp.dot(xl, w1_ref[l].astype(BF16), preferred_element_type=F32)
        acc_b = acc_b + jnp.dot(xl, w1_ref[CMP_STRIDE + l].astype(BF16), preferred_element_type=F32)
    o_ref[...] = _compress_finish(acc_a, acc_b, pet_ref[0:1, :], w2_ref[...].astype(BF16))


def compress_prompt(proj, w1, w2, pet, layer, *, batch, seq, col0):
    nch = seq // CMP_STRIDE
    cb0 = col0 // HEAD_DIM
    return pl.pallas_call(
        functools.partial(_cmp_prompt_kernel, nch=nch),
        grid=(batch, 2, NSA_KV_HEADS),
        in_specs=[pl.BlockSpec((seq, HEAD_DIM), lambda b, s, h: (b, cb0 + s * NSA_KV_HEADS + h)),
                  pl.BlockSpec((None, None, CMP_LEN, HEAD_DIM, HEAD_DIM), lambda b, s, h: (layer, s, 0, 0, 0)),
                  pl.BlockSpec((None, None, HEAD_DIM, HEAD_DIM), lambda b, s, h: (layer, s, 0, 0)),
                  pl.BlockSpec((None, 8, HEAD_DIM), lambda b, s, h: (s, 0, 0))],
        out_specs=pl.BlockSpec((None, None, None, nch, HEAD_DIM), lambda b, s, h: (b, s, h, 0, 0)),
        out_shape=jax.ShapeDtypeStruct((batch, 2, NSA_KV_HEADS, nch, HEAD_DIM), F32),
        compiler_params=_params(("arbitrary", "arbitrary", "arbitrary")),
        name="compress_prompt",
    )(proj, w1, w2, pet)


CMP_PAGES = 16
CHUNKS_PER_PAGE = PAGE_SIZE // CMP_STRIDE
CMP_HEADS = 2 * NSA_KV_HEADS
CMP_SLOT_ROWS = CMP_PAGES * CHUNKS_PER_PAGE * NSA_KV_HEADS


def _cmp_pages_kernel(pt_ref, *refs):
    pages = refs[:CMP_PAGES]
    w_ref, a_ref, b_ref = refs[CMP_PAGES:]
    low = lax.broadcasted_iota(jnp.int32, (CMP_HEADS, HEAD_DIM), 0) < NSA_KV_HEADS
    acc = [jnp.zeros((CMP_SLOT_ROWS, 2 * HEAD_DIM), F32) for _ in range(2)]
    for lp in range(CMP_STRIDE // 2):
        halves = ([], [])
        for l in (2 * lp, 2 * lp + 1):
            tiles = ([], [])
            for r in range(CMP_PAGES):
                for c in range(0, CHUNKS_PER_PAGE, 2):
                    even = pages[r][l + CMP_STRIDE * c]
                    odd = pages[r][l + CMP_STRIDE * (c + 1)]
                    tiles[0].append(jnp.where(low, even, pltpu.roll(odd, NSA_KV_HEADS, 0)))
                    tiles[1].append(jnp.where(low, pltpu.roll(even, NSA_KV_HEADS, 0), odd))
            for s in range(2):
                halves[s].append(jnp.concatenate(tiles[s], axis=0))
        for s in range(2):
            lhs = jnp.concatenate(halves[s], axis=1).astype(BF16)
            acc[s] = acc[s] + jnp.dot(lhs, w_ref[s, lp], preferred_element_type=F32)
    for s in range(2):
        a_ref[s] = acc[s][:, :HEAD_DIM]
        b_ref[s] = acc[s][:, HEAD_DIM:]


def compress_pages_partial(pool4, page_table, w1_pairs, layer_page0):
    nb, n_pages = page_table.shape
    n_groups = n_pages // CMP_PAGES

    def page_spec(r):
        return pl.BlockSpec((None, PAGE_SIZE, CMP_HEADS, HEAD_DIM),
                            lambda b, j, pt: (layer_page0 + pt[b, j * CMP_PAGES + r], 0, 0, 0))

    out_spec = pl.BlockSpec((None, 2, CMP_SLOT_ROWS, HEAD_DIM), lambda b, j, pt: (b, 0, j, 0))
    out_shape = jax.ShapeDtypeStruct((nb, 2, n_groups * CMP_SLOT_ROWS, HEAD_DIM), F32)
    grid_spec = pltpu.PrefetchScalarGridSpec(
        num_scalar_prefetch=1,
        grid=(nb, n_groups),
        in_specs=[page_spec(r) for r in range(CMP_PAGES)]
        + [pl.BlockSpec(w1_pairs.shape, lambda b, j, pt: (0, 0, 0, 0))],
        out_specs=[out_spec, out_spec],
    )
    return pl.pallas_call(
        _cmp_pages_kernel,
        grid_spec=grid_spec,
        out_shape=[out_shape, out_shape],
        compiler_params=_params(("arbitrary", "arbitrary")),
        name="compress_pages_partial",
    )(page_table, *([pool4] * CMP_PAGES), w1_pairs)


def _cmp_finish_kernel(a_ref, b_ref, w2_ref, pet_ref, o_ref, scr_ref):
    rows = a_ref.shape[0]
    b_next = pltpu.roll(b_ref[...], rows - NSA_KV_HEADS, 0)
    h = jax.nn.gelu(a_ref[...] + b_next + pet_ref[0:1, :])
    res = jnp.dot(h.astype(BF16), w2_ref[...].astype(BF16), preferred_element_type=F32)
    row = lax.broadcasted_iota(jnp.int32, (rows, HEAD_DIM), 0)
    scr_ref[...] = jnp.where(row >= rows - NSA_KV_HEADS, 0.0, res)
    for head in range(NSA_KV_HEADS):
        o_ref[head] = scr_ref[pl.ds(head, rows // NSA_KV_HEADS, stride=NSA_KV_HEADS), :]


def compress_pages_finish(a, b, w2, pet, layer):
    nb, _, rows, _ = a.shape
    blk = (None, None, rows, HEAD_DIM)
    return pl.pallas_call(
        _cmp_finish_kernel,
        grid=(nb, 2),
        in_specs=[pl.BlockSpec(blk, lambda b_, s: (b_, s, 0, 0)),
                  pl.BlockSpec(blk, lambda b_, s: (b_, s, 0, 0)),
                  pl.BlockSpec((None, None, HEAD_DIM, HEAD_DIM), lambda b_, s: (layer, s, 0, 0)),
                  pl.BlockSpec((None, 8, HEAD_DIM), lambda b_, s: (s, 0, 0))],
        out_specs=pl.BlockSpec((None, None, NSA_KV_HEADS, rows // NSA_KV_HEADS, HEAD_DIM),
                               lambda b_, s: (b_, s, 0, 0, 0)),
        out_shape=jax.ShapeDtypeStruct((nb, 2, NSA_KV_HEADS, rows // NSA_KV_HEADS, HEAD_DIM), F32),
        scratch_shapes=[pltpu.VMEM((rows, HEAD_DIM), F32)],
        compiler_params=_params(("arbitrary", "arbitrary")),
        name="compress_pages_finish",
    )(a, b, w2, pet)


def _split3(x):
    hi = x.astype(BF16)
    r1 = x - hi.astype(F32)
    mid = r1.astype(BF16)
    lo = (r1 - mid.astype(F32)).astype(BF16)
    return hi, mid, lo


def _cmp_attn_kernel(slopes_ref, q_ref, kc_ref, vc_ref, ov_ref, o_ref, sel_ref, *, tb, nc, n_sel, pos0, emit_idx):
    kvh = pl.program_id(1)
    i = pl.program_id(2)
    g_count = N_HEADS // NSA_KV_HEADS
    ncp = kc_ref.shape[0]
    qs = _stack_heads(q_ref[...], g_count).astype(BF16)
    kc = kc_ref[...].astype(BF16)
    vc = vc_ref[...].astype(BF16)
    s = lax.dot_general(qs, kc, (((1,), (1,)), ((), ())), preferred_element_type=F32) * SCALE
    t = pos0 + i * tb + lax.broadcasted_iota(jnp.int32, (tb, ncp), 0)
    c = lax.broadcasted_iota(jnp.int32, (tb, ncp), 1)
    valid = (c * CMP_STRIDE + (CMP_LEN - 1) <= t) & (c < nc)
    rel = t.astype(F32) - ((c * CMP_STRIDE).astype(F32) + (CMP_LEN - 1) / 2)
    psum = jnp.zeros((tb, ncp), F32)
    outs = []
    for g in range(g_count):
        sg = s[g * tb:(g + 1) * tb] - slopes_ref[kvh * g_count + g] * rel
        sg = jnp.where(valid, sg, 2 * NEG_FILL)
        m = jnp.maximum(jnp.max(sg, axis=-1, keepdims=True), NEG_FILL)
        p = jnp.exp(sg - m)
        p = p / jnp.maximum(jnp.sum(p, axis=-1, keepdims=True), 1e-30)
        psum = psum + p
        outs.append(jnp.dot(p.astype(BF16), vc, preferred_element_type=F32))
    o_ref[...] = jnp.concatenate(outs, axis=1)

    ov = ov_ref[...]
    if emit_idx:
        shape, t_dim, j_dim = (tb, ov.shape[1]), 0, 1
        imp = sum(jnp.dot(part, ov, preferred_element_type=F32) for part in _split3(psum))
    else:
        shape, t_dim, j_dim = (ov.shape[0], tb), 1, 0
        imp = sum(lax.dot_general(ov, part, (((1,), (1,)), ((), ())), preferred_element_type=F32)
                  for part in _split3(psum))
    tq = pos0 + i * tb + lax.broadcasted_iota(jnp.int32, shape, t_dim)
    j = lax.broadcasted_iota(jnp.int32, shape, j_dim)
    cur = tq // SEL_BLOCK
    forced = (j == 0) | (j == cur) | (j == cur - 1)
    visible = j * SEL_BLOCK <= tq
    rank = jnp.where(forced, 1e9, jnp.where(visible, imp, -1.0))
    rank = jnp.where(j < n_sel, rank, -2.0)
    cnt = jnp.zeros(shape, jnp.int32)
    for jp in range(n_sel):
        one = rank[:, jp:jp + 1] if emit_idx else rank[jp:jp + 1, :]
        beats = (one > rank) | ((one == rank) & (j > jp))
        cnt = cnt + beats.astype(jnp.int32)
    if emit_idx:
        lane = lax.broadcasted_iota(jnp.int32, (tb, LANE), 1)
        idx = jnp.zeros((tb, LANE), jnp.int32)
        jf = j.astype(F32)
        for r in range(SEL_TOPN):
            val = jnp.sum(jnp.where(cnt == r, jf, 0.0), axis=1, keepdims=True).astype(jnp.int32)
            idx = jnp.where(lane == r, val, idx)
        sel_ref[...] = idx
    else:
        chosen = (cnt < SEL_TOPN).astype(F32)
        chosen = jnp.concatenate([chosen, jnp.zeros((LANE - shape[0], tb), F32)], axis=0)
        sel_ref[...] = jnp.transpose(chosen)


def _overlap_matrix(ncp, n_sel, nsp):
    c_start = jnp.arange(ncp, dtype=jnp.int32)[:, None] * CMP_STRIDE
    j = jnp.arange(nsp, dtype=jnp.int32)[None, :]
    ov = jnp.clip(jnp.minimum(c_start + CMP_LEN, (j + 1) * SEL_BLOCK) - jnp.maximum(c_start, j * SEL_BLOCK), 0, None)
    ov = jnp.where(j < n_sel, ov, 0)
    return (ov.astype(F32) / CMP_LEN).astype(BF16)


def cmp_attention_prompt(proj, cmp_kv, slopes, *, batch, seq):
    g_count = N_HEADS // NSA_KV_HEADS
    tb = BAND_BLOCK
    nb = seq // tb
    ncp = cmp_kv.shape[3]
    n_sel = -(-seq // SEL_BLOCK)
    nsp = LANE
    assert tb == LANE and n_sel % 8 == 0
    ov_t = jnp.transpose(_overlap_matrix(ncp, n_sel, nsp))[:n_sel]
    smem = pl.BlockSpec(memory_space=pltpu.SMEM)
    kv_block = (None, None, None, ncp, HEAD_DIM)
    return pl.pallas_call(
        functools.partial(_cmp_attn_kernel, tb=tb, nc=ncp - 1, n_sel=n_sel, pos0=0, emit_idx=False),
        grid=(batch, NSA_KV_HEADS, nb),
        in_specs=[smem,
                  pl.BlockSpec((tb, g_count * HEAD_DIM), lambda b, h, i: (b * nb + i, h)),
                  pl.BlockSpec(kv_block, lambda b, h, i: (b, 0, h, 0, 0)),
                  pl.BlockSpec(kv_block, lambda b, h, i: (b, 1, h, 0, 0)),
                  pl.BlockSpec((n_sel, ncp), lambda b, h, i: (0, 0))],
        out_specs=[pl.BlockSpec((tb, g_count * HEAD_DIM), lambda b, h, i: (b * nb + i, h)),
                   pl.BlockSpec((None, None, tb, nsp), lambda b, h, i: (b, h, i, 0))],
        out_shape=[jax.ShapeDtypeStruct((batch * seq, N_HEADS * HEAD_DIM), F32),
                   jax.ShapeDtypeStruct((batch, NSA_KV_HEADS, seq, nsp), F32)],
        compiler_params=_params(("arbitrary", "arbitrary", "arbitrary")),
        name="cmp_attention_prompt",
    )(slopes, proj, cmp_kv, cmp_kv, ov_t)


def cmp_attention_sample(qkv8, cmp_kv, slopes, *, lk):
    g_count = N_HEADS // NSA_KV_HEADS
    nb = qkv8.shape[0]
    ncp = cmp_kv.shape[3]
    n_sel = -(-lk // SEL_BLOCK)
    nsp = -(-n_sel // LANE) * LANE
    ov = _overlap_matrix(ncp, n_sel, nsp)
    smem = pl.BlockSpec(memory_space=pltpu.SMEM)
    kv_block = (None, None, None, ncp, HEAD_DIM)
    return pl.pallas_call(
        functools.partial(_cmp_attn_kernel, tb=DEC_PAD, nc=ncp - 1, n_sel=n_sel, pos0=PAST_LEN, emit_idx=True),
        grid=(nb, NSA_KV_HEADS, 1),
        in_specs=[smem,
                  pl.BlockSpec((None, DEC_PAD, g_count * HEAD_DIM), lambda b, h, i: (b, 0, h)),
                  pl.BlockSpec(kv_block, lambda b, h, i: (b, 0, h, 0, 0)),
                  pl.BlockSpec(kv_block, lambda b, h, i: (b, 1, h, 0, 0)),
                  pl.BlockSpec((ncp, nsp), lambda b, h, i: (0, 0))],
        out_specs=[pl.BlockSpec((None, DEC_PAD, g_count * HEAD_DIM), lambda b, h, i: (b, 0, h)),
                   pl.BlockSpec((None, None, DEC_PAD, LANE), lambda b, h, i: (b, h, 0, 0))],
        out_shape=[jax.ShapeDtypeStruct((nb, DEC_PAD, N_HEADS * HEAD_DIM), F32),
                   jax.ShapeDtypeStruct((nb, NSA_KV_HEADS, DEC_PAD, LANE), jnp.int32)],
        compiler_params=_params(("arbitrary", "arbitrary", "arbitrary")),
        name="cmp_attention_sample",
    )(slopes, qkv8, cmp_kv, cmp_kv, ov)


SEL_KEY_TILE = 256

def _sel_prompt_kernel(slopes_ref, q_ref, k_ref, v_ref, sel_ref, ex_ref, oc_ref, ow_ref, gl_ref, gb_ref, o_ref,
                       m_ref, l_ref, acc_ref, chosen_ref):
    kvh = pl.program_id(1)
    i = pl.program_id(2)
    g_count = N_HEADS // NSA_KV_HEADS
    tb = BAND_BLOCK
    qs = _stack_heads(q_ref[...], g_count).astype(BF16)
    sel = sel_ref[...].astype(BF16)
    slope_col = jnp.concatenate(
        [jnp.full((tb, 1), slopes_ref[kvh * g_count + g], F32) for g in range(g_count)], axis=0)
    kt_w = SEL_KEY_TILE
    m_ref[...] = jnp.full(m_ref.shape, NEG_FILL, F32)
    l_ref[...] = jnp.zeros(l_ref.shape, F32)
    acc_ref[...] = jnp.zeros(acc_ref.shape, F32)
    for kt in range(chosen_ref.shape[0]):
        chosen_ref[kt] = jnp.dot(sel, ex_ref[:, kt * kt_w:(kt + 1) * kt_w], preferred_element_type=F32)
    tq = i * tb + lax.broadcasted_iota(jnp.int32, (tb, kt_w), 0)
    kk = lax.broadcasted_iota(jnp.int32, (tb, kt_w), 1)

    def attend(k0, dist):
        ks = k_ref[pl.ds(k0, kt_w), :].astype(BF16)
        vs = v_ref[pl.ds(k0, kt_w), :].astype(BF16)
        s = lax.dot_general(qs, ks, (((1,), (1,)), ((), ())), preferred_element_type=F32) * SCALE
        distf = jnp.concatenate([dist] * g_count, axis=0)
        s = jnp.where(distf >= 0.0, s - slope_col * distf, 2 * NEG_FILL)
        m_old = m_ref[...]
        m_new = jnp.maximum(m_old, jnp.max(s, axis=-1, keepdims=True))
        alpha = jnp.exp(m_old - m_new)
        p = jnp.exp(s - jnp.concatenate([m_new] * (kt_w // LANE), axis=1))
        l_ref[...] = alpha * l_ref[...] + jnp.sum(p, axis=-1, keepdims=True)
        acc_ref[...] = alpha * acc_ref[...] + jnp.dot(p.astype(BF16), vs, preferred_element_type=F32)
        m_ref[...] = m_new

    def body(kt, carry):
        k0 = pl.multiple_of(kt * kt_w, kt_w)
        dist = (tq - (k0 + kk)).astype(F32)
        dist = jnp.where(chosen_ref[kt] > 0.5, dist, -1.0)

        @pl.when(jnp.max(dist) >= 0.0)
        def _():
            attend(k0, dist)

        return carry

    n_tiles = (i * tb + tb + kt_w - 1) // kt_w
    lax.fori_loop(0, n_tiles, body, 0)
    o_sel = acc_ref[...] / jnp.maximum(l_ref[...], 1e-30)
    gate = jax.nn.sigmoid(gl_ref[...] + gb_ref[...])
    gate = pltpu.roll(gate, jnp.where(kvh == 0, 0, LANE - kvh * g_count), 1)
    for g in range(g_count):
        sl = slice(g * HEAD_DIM, (g + 1) * HEAD_DIM)
        og = (gate[:, g:g + 1] * oc_ref[:, sl] + gate[:, N_HEADS + g:N_HEADS + g + 1] * o_sel[g * tb:(g + 1) * tb]
              + gate[:, 2 * N_HEADS + g:2 * N_HEADS + g + 1] * ow_ref[:, sl])
        o_ref[:, sl] = og.astype(o_ref.dtype)


def sel_attention_prompt(proj, sel_mask, slopes, o_cmp, o_win, gate_logits, gate_bias, *, batch, seq, k_col0, v_col0):
    g_count = N_HEADS // NSA_KV_HEADS
    tb = BAND_BLOCK
    nb = seq // tb
    kb0, vb0 = k_col0 // HEAD_DIM, v_col0 // HEAD_DIM
    smem = pl.BlockSpec(memory_space=pltpu.SMEM)
    block_of_key = jnp.arange(seq, dtype=jnp.int32)[None, :] // SEL_BLOCK
    expand = (jnp.arange(LANE, dtype=jnp.int32)[:, None] == block_of_key).astype(BF16)
    group_tile = pl.BlockSpec((tb, g_count * HEAD_DIM), lambda b, h, i: (b * nb + i, h))
    return pl.pallas_call(
        _sel_prompt_kernel,
        grid=(batch, NSA_KV_HEADS, nb),
        in_specs=[smem,
                  group_tile,
                  pl.BlockSpec((seq, HEAD_DIM), lambda b, h, i: (b, kb0 + h)),
                  pl.BlockSpec((seq, HEAD_DIM), lambda b, h, i: (b, vb0 + h)),
                  pl.BlockSpec((None, None, tb, LANE), lambda b, h, i: (b, h, i, 0)),
                  pl.BlockSpec((LANE, seq), lambda b, h, i: (0, 0)),
                  group_tile,
                  group_tile,
                  pl.BlockSpec((tb, LANE), lambda b, h, i: (b * nb + i, 0)),
                  pl.BlockSpec((1, LANE), lambda b, h, i: (0, 0))],
        out_specs=group_tile,
        out_shape=jax.ShapeDtypeStruct((batch * seq, N_HEADS * HEAD_DIM), BF16),
        scratch_shapes=[pltpu.VMEM((g_count * tb, HEAD_DIM), F32)] * 3
        + [pltpu.VMEM((seq // SEL_KEY_TILE, tb, SEL_KEY_TILE), F32)],
        compiler_params=_params(("arbitrary", "arbitrary", "arbitrary")),
        name="sel_attention_prompt",
    )(slopes, proj, proj, proj, sel_mask, expand, o_cmp, o_win, gate_logits, gate_bias)


def _sel_sample_kernel(idx_ref, pt_ref, q_ref, *refs, n_past_blocks):
    blocks = refs[:SEL_TOPN]
    kn_ref, vn_ref, slope_ref, o_ref = refs[SEL_TOPN:]
    b, kvh, t = pl.program_id(0), pl.program_id(1), pl.program_id(2)
    base = ((b * NSA_KV_HEADS + kvh) * DEC_SEQ + t) * SEL_TOPN
    g_count = N_HEADS // NSA_KV_HEADS
    pad = jnp.zeros((LANE - DEC_PAD, HEAD_DIM), F32)
    kall = jnp.concatenate([r[:, kvh, :] for r in blocks] + [kn_ref[...], pad], axis=0).astype(BF16)
    vall = jnp.concatenate([r[:, NSA_KV_HEADS + kvh, :] for r in blocks] + [vn_ref[...], pad], axis=0).astype(BF16)
    q = q_ref[...].astype(BF16)
    s = lax.dot_general(q, kall, (((1,), (1,)), ((), ())), preferred_element_type=F32) * SCALE
    lane = lax.broadcasted_iota(jnp.int32, (g_count, LANE), 1)
    low = lane < SEL_BLOCK
    qpos = PAST_LEN + t
    pos_parts, ok_parts = [], []
    for c in range(SEL_TOPN // 2):
        b0 = idx_ref[base + 2 * c]
        b1 = idx_ref[base + 2 * c + 1]
        p0 = jnp.where(b0 < n_past_blocks, b0 * SEL_BLOCK, PAST_LEN + DEC_SEQ)
        p1 = jnp.where(b1 < n_past_blocks, b1 * SEL_BLOCK, PAST_LEN + DEC_SEQ)
        pos_parts.append(jnp.where(low, p0 + lane, p1 + (lane - SEL_BLOCK)))
    pos_parts.append(PAST_LEN + lane)
    dist = qpos - jnp.concatenate(pos_parts, axis=1)
    valid = dist >= 0
    slope = jnp.concatenate([slope_ref[...]] * (SEL_TOPN // 2 + 1), axis=1)
    o_ref[...] = _softmax_pv(s - slope * dist.astype(F32), valid, vall)


def sel_attention_sample(q_s, pool_halves, page_table, sel_idx, new_kv8, slope_rows, layer_half0):
    nb = q_s.shape[0]
    g_count = N_HEADS // NSA_KV_HEADS
    n_past_blocks = PAST_LEN // SEL_BLOCK
    per_page = PAGE_SIZE // SEL_BLOCK
    page_shift = per_page.bit_length() - 1
    assert per_page == 1 << page_shift

    def blk_spec(r):
        def index(b, h, t, idx, pt):
            blk = jnp.minimum(idx[((b * NSA_KV_HEADS + h) * DEC_SEQ + t) * SEL_TOPN + r], n_past_blocks - 1)
            page = pt[b, lax.shift_right_logical(blk, page_shift)]
            return (layer_half0 + page * per_page + jnp.bitwise_and(blk, per_page - 1), 0, 1, 0)
        return pl.BlockSpec((None, SEL_BLOCK, CMP_HEADS, HEAD_DIM), index)

    q_block = (None, None, None, g_count, HEAD_DIM)
    new_block = (None, None, None, DEC_PAD, HEAD_DIM)
    grid_spec = pltpu.PrefetchScalarGridSpec(
        num_scalar_prefetch=2,
        grid=(nb, NSA_KV_HEADS, DEC_SEQ),
        in_specs=[pl.BlockSpec(q_block, lambda b, h, t, idx, pt: (b, t, h, 0, 0))]
        + [blk_spec(r) for r in range(SEL_TOPN)]
        + [pl.BlockSpec(new_block, lambda b, h, t, idx, pt: (b, 2, h, 0, 0)),
           pl.BlockSpec(new_block, lambda b, h, t, idx, pt: (b, 3, h, 0, 0)),
           pl.BlockSpec((None, g_count, LANE), lambda b, h, t, idx, pt: (h, 0, 0))],
        out_specs=pl.BlockSpec(q_block, lambda b, h, t, idx, pt: (b, t, h, 0, 0)),
    )
    return pl.pallas_call(
        functools.partial(_sel_sample_kernel, n_past_blocks=n_past_blocks),
        grid_spec=grid_spec,
        out_shape=jax.ShapeDtypeStruct(q_s.shape, F32),
        compiler_params=_params(("arbitrary", "arbitrary", "arbitrary")),
        name="sel_attention_sample",
    )(sel_idx, page_table, q_s, *([pool_halves] * SEL_TOPN), new_kv8, new_kv8, slope_rows)


def _combine_kernel(oc_ref, os_ref, ow_ref, gl_ref, gb_ref, o_ref):
    gate = jax.nn.sigmoid(gl_ref[...] + gb_ref[...])
    for h in range(N_HEADS):
        sl = slice(h * HEAD_DIM, (h + 1) * HEAD_DIM)
        o = (gate[:, h:h + 1] * oc_ref[:, sl] + gate[:, N_HEADS + h:N_HEADS + h + 1] * os_ref[:, sl]
             + gate[:, 2 * N_HEADS + h:2 * N_HEADS + h + 1] * ow_ref[:, sl])
        o_ref[:, sl] = o.astype(o_ref.dtype)


def gate_combine(o_c, o_s, o_w, gate_logits, gate_bias, tm, row_block0=0):
    m, n = o_c.shape
    big = pl.BlockSpec((tm, n), lambda i: (i, 0))
    return pl.pallas_call(
        _combine_kernel,
        grid=(m // tm,),
        in_specs=[big, big, big,
                  pl.BlockSpec((tm, LANE), lambda i: (row_block0 + i, 0)),
                  pl.BlockSpec((1, LANE), lambda i: (0, 0))],
        out_specs=big,
        out_shape=jax.ShapeDtypeStruct((m, n), BF16),
        compiler_params=_params(("arbitrary",)),
        name="gate_combine",
    )(o_c, o_s, o_w, gate_logits, gate_bias)


TM_DENSE = 1664
TN_PROJ = 256
TF_FFN = 256
TM_FFN_OUT = 832
TN_FFN_OUT = 256
TM_NORM = 320
TM_PROMPT = 256
NQ = N_HEADS * HEAD_DIM


def _ffn_block(x, g, w_in, w_out_bf, layer, sub):
    xn = rms_norm(x, g, BF16, TM_NORM)
    h = ffn_in(xn, w_in, layer, sub, TM_DENSE, TF_FFN)
    return ffn_out(h, w_out_bf, layer, sub, x, TM_FFN_OUT, TN_FFN_OUT)


def _sample_rows8(proj):
    s = proj[M_PROMPT:].reshape(DEC_BATCH, DEC_SEQ, proj.shape[1])
    return jnp.pad(s, ((0, 0), (0, DEC_PAD - DEC_SEQ), (0, 0)))


def _swa_layer(x, g, w_in, w_out, sinks_all, cache_all, li, slopes):
    xn = rms_norm(x, g, BF16, TM_NORM)
    kv_cols = 2 * SWA_KV_HEADS * HEAD_DIM
    proj = matmul_wres(xn, w_in, (li,), NQ + kv_cols, TM_DENSE, TN_PROJ)
    sinks = sinks_all[li].astype(F32)
    o_p = band_attention(proj, slopes, sinks, batch=BATCH, seq=SEQ, kv_heads=SWA_KV_HEADS, k_col0=NQ,
                         v_col0=NQ + SWA_KV_HEADS * HEAD_DIM, window=SWA_WINDOW, use_sink=True, out_dtype=BF16)
    cache = cache_all[li]
    lc = cache.shape[1]
    o_s = decode_attention(_sample_rows8(proj), cache.reshape(DEC_BATCH, lc * 2 * SWA_KV_HEADS, HEAD_DIM), slopes, sinks,
                           kv_heads=SWA_KV_HEADS, k_col0=NQ, v_col0=NQ + SWA_KV_HEADS * HEAD_DIM,
                           window=SWA_WINDOW, use_sink=True, out_dtype=BF16)
    attn = jnp.concatenate([o_p, o_s[:, :DEC_SEQ].reshape(M_SAMPLE, NQ)], axis=0)
    x = matmul_wres(attn, w_out, (li,), D_MODEL, TM_DENSE, TN_PROJ, res=x)
    kv_p = proj[:M_PROMPT, NQ:].reshape(BATCH, SEQ, 2, SWA_KV_HEADS, HEAD_DIM)
    kv_s = proj[M_PROMPT:, NQ:].reshape(DEC_BATCH, DEC_SEQ, 2, SWA_KV_HEADS, HEAD_DIM)
    buf_p = kv_p[:, SEQ - min(SWA_WINDOW, SEQ):]
    buf_s = jnp.concatenate([cache, kv_s], axis=1)[:, DEC_SEQ:]
    return x, buf_p, buf_s


def _nsa_layer(x, g, w_in, gate_b, cmp_pe, cmp_w1, cmp_w2, w_out, win_cache_all, pool_all, page_table, li, slopes):
    xn = rms_norm(x, g, BF16, TM_NORM)
    kvw = NSA_KV_HEADS * HEAD_DIM
    main_cols = NQ + 6 * kvw
    proj = matmul_wres(xn, w_in, (li,), main_cols, TM_DENSE, TN_PROJ)
    n_gate = 3 * N_HEADS
    w_gate = jnp.pad(w_in[li][:, main_cols:], ((0, 0), (0, LANE - n_gate)))
    gate_logits = matmul_wres(xn, w_gate, (), LANE, TM_DENSE, LANE)
    gate_bias = jnp.pad(gate_b[li], (0, LANE - n_gate)).reshape(1, LANE)
    zeros = jnp.zeros((N_HEADS,), F32)

    pet = pe_term(cmp_pe, cmp_w1, li)
    w1 = cmp_w1[li].astype(BF16)
    top = jnp.concatenate([w1[:, 0:CMP_STRIDE:2], w1[:, CMP_STRIDE::2]], axis=-1)
    bot = jnp.concatenate([w1[:, 1:CMP_STRIDE:2], w1[:, CMP_STRIDE + 1::2]], axis=-1)
    w1_pairs = jnp.concatenate([top, bot], axis=-2)

    o_w_p = band_attention(proj, slopes, zeros, batch=BATCH, seq=SEQ, kv_heads=NSA_KV_HEADS, k_col0=NQ + 4 * kvw,
                           v_col0=NQ + 5 * kvw, window=NSA_WINDOW, use_sink=False, out_dtype=F32)
    cmp_p = compress_prompt(proj, cmp_w1, cmp_w2, pet, li, batch=BATCH, seq=SEQ, col0=NQ)
    o_c_p, sel_mask = cmp_attention_prompt(proj, cmp_p, slopes, batch=BATCH, seq=SEQ)
    attn_p = sel_attention_prompt(proj, sel_mask, slopes, o_c_p, o_w_p, gate_logits, gate_bias, batch=BATCH, seq=SEQ,
                                  k_col0=NQ + 2 * kvw, v_col0=NQ + 3 * kvw)

    qkv8 = _sample_rows8(proj)
    win_cache = win_cache_all[li]
    lc = win_cache.shape[1]
    o_w_s = decode_attention(qkv8, win_cache.reshape(DEC_BATCH, lc * 2 * NSA_KV_HEADS, HEAD_DIM), slopes, zeros,
                             kv_heads=NSA_KV_HEADS, k_col0=NQ + 4 * kvw, v_col0=NQ + 5 * kvw, window=NSA_WINDOW,
                             use_sink=False, out_dtype=F32)
    n_pool = pool_all.shape[1]
    pool = pool_all.reshape(pool_all.shape[0] * n_pool, PAGE_SIZE, 4 * kvw)
    pool4 = pool.reshape(pool.shape[0], PAGE_SIZE, 2 * CMP_HEADS, HEAD_DIM)
    part_a, part_b = compress_pages_partial(pool4, page_table, w1_pairs, li * n_pool)
    cmp_s = compress_pages_finish(part_a, part_b, cmp_w2, pet, li)
    o_c_s, sel_idx = cmp_attention_sample(qkv8, cmp_s, slopes, lk=PAST_LEN + DEC_SEQ)
    sel_flat = sel_idx[:, :, :DEC_SEQ, :SEL_TOPN].reshape(-1)
    rows_s = proj[M_PROMPT:, NQ:NQ + 4 * kvw].reshape(DEC_BATCH, DEC_SEQ, 4, NSA_KV_HEADS, HEAD_DIM)
    new_kv8 = jnp.pad(rows_s.transpose(0, 2, 3, 1, 4), ((0, 0), (0, 0), (0, 0), (0, DEC_PAD - DEC_SEQ), (0, 0)))
    g_count = N_HEADS // NSA_KV_HEADS
    q_s = proj[M_PROMPT:, :NQ].reshape(DEC_BATCH, DEC_SEQ, NSA_KV_HEADS, g_count, HEAD_DIM)
    slope_rows = jnp.broadcast_to(slopes.reshape(NSA_KV_HEADS, g_count, 1), (NSA_KV_HEADS, g_count, LANE))
    per_page = PAGE_SIZE // SEL_BLOCK
    pool_halves = pool.reshape(pool.shape[0] * per_page, SEL_BLOCK, 2 * CMP_HEADS, HEAD_DIM)
    o_s_s = sel_attention_sample(q_s, pool_halves, page_table, sel_flat, new_kv8, slope_rows, li * n_pool * per_page)
    attn_s = gate_combine(o_c_s[:, :DEC_SEQ].reshape(M_SAMPLE, NQ), o_s_s.reshape(M_SAMPLE, NQ),
                          o_w_s[:, :DEC_SEQ].reshape(M_SAMPLE, NQ), gate_logits, gate_bias, M_SAMPLE,
                          row_block0=M_PROMPT // M_SAMPLE)

    attn = jnp.concatenate([attn_p, attn_s], axis=0)
    x = matmul_wres(attn, w_out, (li,), D_MODEL, TM_DENSE, TN_PROJ, res=x)

    kv_p = proj[:M_PROMPT, NQ:main_cols].reshape(BATCH, SEQ, 6, NSA_KV_HEADS, HEAD_DIM)
    kv_s = proj[M_PROMPT:, NQ:main_cols].reshape(DEC_BATCH, DEC_SEQ, 6, NSA_KV_HEADS, HEAD_DIM)
    win_p = kv_p[:, SEQ - min(NSA_WINDOW, SEQ):, 4:]
    win_s = jnp.concatenate([win_cache, kv_s[:, :, 4:]], axis=1)[:, DEC_SEQ:]
    return x, kv_p[:, :, :4], kv_s[:, :, :4], win_p, win_s


def kernel(x_prompt, x_sample, cache_swa_kv, cache_nsa_win_kv, cache_nsa_kv, page_table, norm_g, final_norm_g,
           ffn_w_in, ffn_w_out, swa_w_in, swa_w_out, swa_sinks, nsa_w_in, nsa_gate_b, nsa_cmp_pe, nsa_cmp_w1,
           nsa_cmp_w2, nsa_w_out):
    x = jnp.concatenate([x_prompt.reshape(M_PROMPT, D_MODEL), x_sample.reshape(M_SAMPLE, D_MODEL)], axis=0)
    slopes = jnp.exp2(-8.0 * jnp.arange(1, N_HEADS + 1, dtype=F32) / N_HEADS)
    w_out_bf = ffn_w_out.astype(BF16)
    swa_p, swa_s, win_p, win_s, kv_p, kv_s = [], [], [], [], [], []
    for i in range(DEPTH):
        x = _ffn_block(x, norm_g[i, 0], ffn_w_in, w_out_bf, i, 0)
        li = i // N_MIXERS
        if i % N_MIXERS == 0:
            x, bp, bs = _swa_layer(x, norm_g[i, 1], swa_w_in, swa_w_out, swa_sinks, cache_swa_kv, li, slopes)
            swa_p.append(bp)
            swa_s.append(bs)
        else:
            x, rp, rs, wp, ws = _nsa_layer(x, norm_g[i, 1], nsa_w_in, nsa_gate_b, nsa_cmp_pe, nsa_cmp_w1,
                                           nsa_cmp_w2, nsa_w_out, cache_nsa_win_kv, cache_nsa_kv, page_table, li,
                                           slopes)
            kv_p.append(rp)
            kv_s.append(rs)
            win_p.append(wp)
            win_s.append(ws)
        x = _ffn_block(x, norm_g[i, 2], ffn_w_in, w_out_bf, i, 1)
    y_p = rms_norm(x, final_norm_g, F32, TM_PROMPT, rows=M_PROMPT).reshape(BATCH, SEQ, D_MODEL)
    y_s = rms_norm(x, final_norm_g, F32, M_SAMPLE, row_block0=M_PROMPT // M_SAMPLE, rows=M_SAMPLE)
    return (y_p, y_s.reshape(DEC_BATCH, DEC_SEQ, D_MODEL), jnp.stack(swa_p), jnp.stack(swa_s), jnp.stack(win_p),
            jnp.stack(win_s), jnp.stack(kv_p), jnp.stack(kv_s))
```

```python
import functools
import math

import jax
import jax.numpy as jnp
from jax import lax
from jax.experimental import pallas as pl
from jax.experimental.pallas import tpu as pltpu

D_MODEL = 4096
BATCH = 4
SEQ = 2048
DEPTH = 2
DEC_BATCH = 32
DEC_SEQ = 4
PAST_LEN = 16384
PAGE_SIZE = 128
N_HEADS = 32
HEAD_DIM = 128
SWA_KV_HEADS = 8
SWA_WINDOW = 128
NSA_KV_HEADS = 4
CMP_STRIDE = 16
CMP_LEN = 32
SEL_BLOCK = 64
SEL_TOPN = 16
NSA_WINDOW = 512
D_FF = 11008
BAND_BLOCK = 128
N_MIXERS = 2
RMS_EPS = 1e-6
NEG_FILL = -1e30
SCALE = HEAD_DIM ** -0.5

M_PROMPT = BATCH * SEQ
M_SAMPLE = DEC_BATCH * DEC_SEQ
M_ALL = M_PROMPT + M_SAMPLE
DEC_PAD = 8
LANE = 128
F32 = jnp.float32
BF16 = jnp.bfloat16
VMEM_LIMIT = 60 * 1024 * 1024


def _params(sem):
    return pltpu.CompilerParams(dimension_semantics=sem, vmem_limit_bytes=VMEM_LIMIT)


def _rms_kernel(x_ref, g_ref, o_ref):
    x = x_ref[...]
    y = x * lax.rsqrt(jnp.mean(x * x, axis=-1, keepdims=True) + RMS_EPS)
    o_ref[...] = (y * g_ref[...]).astype(o_ref.dtype)


def rms_norm(x, g, out_dtype, tm, row_block0=0, rows=None):
    m, d = x.shape
    rows = m if rows is None else rows
    return pl.pallas_call(
        _rms_kernel,
        grid=(rows // tm,),
        in_specs=[pl.BlockSpec((tm, d), lambda i: (row_block0 + i, 0)),
                  pl.BlockSpec((1, d), lambda i: (0, 0))],
        out_specs=pl.BlockSpec((tm, d), lambda i: (i, 0)),
        out_shape=jax.ShapeDtypeStruct((rows, d), out_dtype),
        compiler_params=_params(("arbitrary",)),
        name="rms_norm",
    )(x, g.reshape(1, d))


def _row_tile_spec(tm, k):
    return pl.BlockSpec((tm, k), lambda i, j: (i, 0))


def _mm_kernel(x_ref, w_ref, *rest, scale, has_res):
    acc = jnp.dot(x_ref[...], w_ref[...].astype(BF16), preferred_element_type=F32)
    if has_res:
        res_ref, o_ref = rest
        o_ref[...] = res_ref[...] + (acc if scale == 1.0 else scale * acc)
    else:
        (o_ref,) = rest
        o_ref[...] = acc.astype(o_ref.dtype)


def matmul_wres(x, w, w_prefix, n_cols, tm, tn, res=None, scale=1.0, col_block0=0):
    m, k = x.shape
    npre = len(w_prefix)
    w_block = (None,) * npre + (k, tn)
    in_specs = [_row_tile_spec(tm, k),
                pl.BlockSpec(w_block, lambda i, j: tuple(w_prefix) + (0, col_block0 + j))]
    args = [x, w]
    if res is not None:
        in_specs.append(pl.BlockSpec((tm, tn), lambda i, j: (i, j)))
        args.append(res)
    return pl.pallas_call(
        functools.partial(_mm_kernel, scale=scale, has_res=res is not None),
        grid=(m // tm, n_cols // tn),
        in_specs=in_specs,
        out_specs=pl.BlockSpec((tm, tn), lambda i, j: (i, j)),
        out_shape=jax.ShapeDtypeStruct((m, n_cols), F32),
        compiler_params=_params(("arbitrary", "arbitrary")),
        name="matmul_wres",
    )(*args)


def _ffn_in_kernel(x_ref, wg_ref, wu_ref, o_ref):
    x = x_ref[...]
    g = jnp.dot(x, wg_ref[...].astype(BF16), preferred_element_type=F32)
    u = jnp.dot(x, wu_ref[...].astype(BF16), preferred_element_type=F32)
    o_ref[...] = (jax.nn.silu(g) * u).astype(o_ref.dtype)


def ffn_in(xn, w_in, layer, sub, tm, tf):
    m, k = xn.shape
    nf = D_FF // tf
    w_block = (None, None, k, tf)
    return pl.pallas_call(
        _ffn_in_kernel,
        grid=(m // tm, nf),
        in_specs=[_row_tile_spec(tm, k),
                  pl.BlockSpec(w_block, lambda i, j: (layer, sub, 0, j)),
                  pl.BlockSpec(w_block, lambda i, j: (layer, sub, 0, nf + j))],
        out_specs=pl.BlockSpec((tm, tf), lambda i, j: (i, j)),
        out_shape=jax.ShapeDtypeStruct((m, D_FF), BF16),
        compiler_params=_params(("arbitrary", "arbitrary")),
        name="ffn_in",
    )(xn, w_in, w_in)


def _ffn_out_kernel(h_ref, w_ref, res_ref, o_ref):
    acc = jnp.dot(h_ref[...], w_ref[...], preferred_element_type=F32)
    o_ref[...] = res_ref[...] + 0.5 * acc


def ffn_out(h, w_out_bf, layer, sub, res, tm, tn):
    m, k = h.shape
    n = res.shape[1]
    return pl.pallas_call(
        _ffn_out_kernel,
        grid=(m // tm, n // tn),
        in_specs=[_row_tile_spec(tm, k),
                  pl.BlockSpec((None, None, k, tn), lambda i, j: (layer, sub, 0, j)),
                  pl.BlockSpec((tm, tn), lambda i, j: (i, j))],
        out_specs=pl.BlockSpec((tm, tn), lambda i, j: (i, j)),
        out_shape=jax.ShapeDtypeStruct((m, n), F32),
        compiler_params=_params(("arbitrary", "arbitrary")),
        name="ffn_out",
    )(h, w_out_bf, res)


def _stack_heads(q, g_count):
    return jnp.concatenate([q[:, g * HEAD_DIM:(g + 1) * HEAD_DIM] for g in range(g_count)], axis=0)


def _softmax_pv(s, valid, v_bf, sink=None):
    s = jnp.where(valid, s, 2 * NEG_FILL)
    m = jnp.maximum(jnp.max(s, axis=-1, keepdims=True), NEG_FILL)
    if sink is not None:
        m = jnp.maximum(m, sink)
    p = jnp.exp(s - m)
    den = jnp.sum(p, axis=-1, keepdims=True)
    if sink is not None:
        den = den + jnp.exp(sink - m)
    o = jnp.dot(p.astype(BF16), v_bf, preferred_element_type=F32)
    return o / jnp.maximum(den, 1e-30)


def _band_kernel(slopes_ref, sinks_ref, q_ref, k_ref, v_ref, o_ref, *, g_count, nprev, window, use_sink, kv_per_step):
    i = pl.program_id(2)
    width = (nprev + 1) * BAND_BLOCK
    start = pl.multiple_of(jnp.maximum(i - nprev, 0) * BAND_BLOCK, BAND_BLOCK)
    tq = i * BAND_BLOCK + lax.broadcasted_iota(jnp.int32, (BAND_BLOCK, width), 0)
    kp = start + lax.broadcasted_iota(jnp.int32, (BAND_BLOCK, width), 1)
    dist = tq - kp
    valid = (dist >= 0) & (dist <= window)
    distf = dist.astype(F32)
    group = g_count * HEAD_DIM
    for r in range(kv_per_step):
        kvh = pl.program_id(1) * kv_per_step + r
        k = k_ref[pl.ds(start, width), r * HEAD_DIM:(r + 1) * HEAD_DIM].astype(BF16)
        v = v_ref[pl.ds(start, width), r * HEAD_DIM:(r + 1) * HEAD_DIM].astype(BF16)
        qs = _stack_heads(q_ref[:, r * group:(r + 1) * group], g_count).astype(BF16)
        s = lax.dot_general(qs, k, (((1,), (1,)), ((), ())), preferred_element_type=F32) * SCALE
        for g in range(g_count):
            h = kvh * g_count + g
            sg = s[g * BAND_BLOCK:(g + 1) * BAND_BLOCK] - slopes_ref[h] * distf
            o = _softmax_pv(sg, valid, v, sinks_ref[h] if use_sink else None)
            o_ref[:, r * group + g * HEAD_DIM:r * group + (g + 1) * HEAD_DIM] = o.astype(o_ref.dtype)


def band_attention(proj, slopes, sinks, *, batch, seq, kv_heads, k_col0, v_col0, window, use_sink, out_dtype):
    g_count = N_HEADS // kv_heads
    nb = seq // BAND_BLOCK
    nprev = -(-window // BAND_BLOCK)
    kv_per_step = max(1, 8 // g_count)
    cols = kv_per_step * HEAD_DIM
    assert kv_heads % kv_per_step == 0 and k_col0 % cols == 0 and v_col0 % cols == 0
    kb0, vb0 = k_col0 // cols, v_col0 // cols
    smem = pl.BlockSpec(memory_space=pltpu.SMEM)
    q_tile = pl.BlockSpec((BAND_BLOCK, kv_per_step * g_count * HEAD_DIM), lambda b, h, i: (b * nb + i, h))
    return pl.pallas_call(
        functools.partial(_band_kernel, g_count=g_count, nprev=nprev, window=window, use_sink=use_sink,
                          kv_per_step=kv_per_step),
        grid=(batch, kv_heads // kv_per_step, nb),
        in_specs=[smem, smem,
                  q_tile,
                  pl.BlockSpec((seq, cols), lambda b, h, i: (b, kb0 + h)),
                  pl.BlockSpec((seq, cols), lambda b, h, i: (b, vb0 + h))],
        out_specs=q_tile,
        out_shape=jax.ShapeDtypeStruct((batch * seq, N_HEADS * HEAD_DIM), out_dtype),
        compiler_params=_params(("arbitrary", "arbitrary", "arbitrary")),
        name="band_attention",
    )(slopes, sinks, proj, proj, proj)


def _dec_kernel(slopes_ref, sinks_ref, qkv_ref, c_ref, o_ref, *, kv_heads, g_count, lc, k_col0, v_col0,
                window, use_sink):
    rows = g_count * DEC_PAD
    width = lc + LANE
    t = lax.broadcasted_iota(jnp.int32, (DEC_PAD, width), 0)
    col = lax.broadcasted_iota(jnp.int32, (DEC_PAD, width), 1)
    dist = jnp.where(col < lc, t + (lc - col), t - (col - lc))
    valid = (dist >= 0) & (dist <= window) & (col < lc + DEC_SEQ)
    distf = dist.astype(F32)
    pad = jnp.zeros((LANE - DEC_PAD, HEAD_DIM), F32)
    for kvh in range(kv_heads):
        q0 = kvh * g_count * HEAD_DIM
        qs = _stack_heads(qkv_ref[:, q0:q0 + g_count * HEAD_DIM], g_count).astype(BF16)
        kn = qkv_ref[:, k_col0 + kvh * HEAD_DIM:k_col0 + (kvh + 1) * HEAD_DIM]
        vn = qkv_ref[:, v_col0 + kvh * HEAD_DIM:v_col0 + (kvh + 1) * HEAD_DIM]
        kc = c_ref[pl.ds(kvh, lc, stride=2 * kv_heads), :]
        vc = c_ref[pl.ds(kv_heads + kvh, lc, stride=2 * kv_heads), :]
        kall = jnp.concatenate([kc, kn, pad], axis=0).astype(BF16)
        vall = jnp.concatenate([vc, vn, pad], axis=0).astype(BF16)
        s = lax.dot_general(qs, kall, (((1,), (1,)), ((), ())), preferred_element_type=F32) * SCALE
        assert s.shape == (rows, width)
        for g in range(g_count):
            h = kvh * g_count + g
            sg = s[g * DEC_PAD:(g + 1) * DEC_PAD] - slopes_ref[h] * distf
            o = _softmax_pv(sg, valid, vall, sinks_ref[h] if use_sink else None)
            o_ref[:, h * HEAD_DIM:(h + 1) * HEAD_DIM] = o.astype(o_ref.dtype)


def decode_attention(qkv8, cache, slopes, sinks, *, kv_heads, k_col0, v_col0, window, use_sink, out_dtype):
    nb, _, ncols = qkv8.shape
    lc = cache.shape[1] // (2 * kv_heads)
    g_count = N_HEADS // kv_heads
    smem = pl.BlockSpec(memory_space=pltpu.SMEM)
    return pl.pallas_call(
        functools.partial(_dec_kernel, kv_heads=kv_heads, g_count=g_count, lc=lc, k_col0=k_col0, v_col0=v_col0,
                          window=window, use_sink=use_sink),
        grid=(nb,),
        in_specs=[smem, smem,
                  pl.BlockSpec((None, DEC_PAD, ncols), lambda b: (b, 0, 0)),
                  pl.BlockSpec((None, cache.shape[1], HEAD_DIM), lambda b: (b, 0, 0))],
        out_specs=pl.BlockSpec((None, DEC_PAD, N_HEADS * HEAD_DIM), lambda b: (b, 0, 0)),
        out_shape=jax.ShapeDtypeStruct((nb, DEC_PAD, N_HEADS * HEAD_DIM), out_dtype),
        compiler_params=_params(("arbitrary",)),
        name="decode_attention",
    )(slopes, sinks, qkv8, cache)


def _pe_term_kernel(pe_ref, w1_ref, o_ref):
    acc = jnp.zeros((8, HEAD_DIM), F32)
    for l in range(CMP_LEN):
        row = jnp.broadcast_to(pe_ref[l:l + 1, :], (8, HEAD_DIM)).astype(BF16)
        acc = acc + jnp.dot(row, w1_ref[l].astype(BF16), preferred_element_type=F32)
    o_ref[...] = acc


def pe_term(pe, w1, layer):
    return pl.pallas_call(
        _pe_term_kernel,
        grid=(2,),
        in_specs=[pl.BlockSpec((None, None, CMP_LEN, HEAD_DIM), lambda s: (layer, s, 0, 0)),
                  pl.BlockSpec((None, None, CMP_LEN, HEAD_DIM, HEAD_DIM), lambda s: (layer, s, 0, 0, 0))],
        out_specs=pl.BlockSpec((None, 8, HEAD_DIM), lambda s: (s, 0, 0)),
        out_shape=jax.ShapeDtypeStruct((2, 8, HEAD_DIM), F32),
        compiler_params=_params(("arbitrary",)),
        name="pe_term",
    )(pe, w1)


def _compress_finish(acc_a, acc_b, pet_row, w2_bf):
    n = acc_a.shape[0]
    pre = acc_a + pltpu.roll(acc_b, n - 1, 0) + pet_row
    return jnp.dot(jax.nn.gelu(pre).astype(BF16), w2_bf, preferred_element_type=F32)


def _cmp_prompt_kernel(x_ref, w1_ref, w2_ref, pet_ref, o_ref, *, nch):
    acc_a = jnp.zeros((nch, HEAD_DIM), F32)
    acc_b = jnp.zeros((nch, HEAD_DIM), F32)
    for l in range(CMP_STRIDE):
        xl = x_ref[pl.ds(l, nch, stride=CMP_STRIDE), :].astype(BF16)
        acc_a = acc_a + jnp.dot(xl, w1_ref[l].astype(BF16), preferred_element_type=F32)
        acc_b = acc_b + jnp.dot(xl, w1_ref[CMP_STRIDE + l].astype(BF16), preferred_element_type=F32)
    o_ref[...] = _compress_finish(acc_a, acc_b, pet_ref[0:1, :], w2_ref[...].astype(BF16))


def compress_prompt(proj, w1, w2, pet, layer, *, batch, seq, col0):
    nch = seq // CMP_STRIDE
    cb0 = col0 // HEAD_DIM
    return pl.pallas_call(
        functools.partial(_cmp_prompt_kernel, nch=nch),
        grid=(batch, 2, NSA_KV_HEADS),
        in_specs=[pl.BlockSpec((seq, HEAD_DIM), lambda b, s, h: (b, cb0 + s * NSA_KV_HEADS + h)),
                  pl.BlockSpec((None, None, CMP_LEN, HEAD_DIM, HEAD_DIM), lambda b, s, h: (layer, s, 0, 0, 0)),
                  pl.BlockSpec((None, None, HEAD_DIM, HEAD_DIM), lambda b, s, h: (layer, s, 0, 0)),
                  pl.BlockSpec((None, 8, HEAD_DIM), lambda b, s, h: (s, 0, 0))],
        out_specs=pl.BlockSpec((None, None, None, nch, HEAD_DIM), lambda b, s, h: (b, s, h, 0, 0)),
        out_shape=jax.ShapeDtypeStruct((batch, 2, NSA_KV_HEADS, nch, HEAD_DIM), F32),
        compiler_params=_params(("arbitrary", "arbitrary", "arbitrary")),
        name="compress_prompt",
    )(proj, w1, w2, pet)


CMP_PAGES = 32
CHUNKS_PER_PAGE = PAGE_SIZE // CMP_STRIDE
CMP_HEADS = 2 * NSA_KV_HEADS
CMP_SLOT_ROWS = CMP_PAGES * CHUNKS_PER_PAGE * NSA_KV_HEADS


def _cmp_pages_kernel(pt_ref, *refs):
    pages = refs[:CMP_PAGES]
    w_ref, a_ref, b_ref = refs[CMP_PAGES:]
    low = lax.broadcasted_iota(jnp.int32, (CMP_HEADS, HEAD_DIM), 0) < NSA_KV_HEADS
    acc = [jnp.zeros((CMP_SLOT_ROWS, 2 * HEAD_DIM), F32) for _ in range(2)]
    for lp in range(CMP_STRIDE // 2):
        halves = ([], [])
        for l in (2 * lp, 2 * lp + 1):
            tiles = ([], [])
            for r in range(CMP_PAGES):
                for c in range(0, CHUNKS_PER_PAGE, 2):
                    even = pages[r][l + CMP_STRIDE * c]
                    odd = pages[r][l + CMP_STRIDE * (c + 1)]
                    tiles[0].append(jnp.where(low, even, pltpu.roll(odd, NSA_KV_HEADS, 0)))
                    tiles[1].append(jnp.where(low, pltpu.roll(even, NSA_KV_HEADS, 0), odd))
            for s in range(2):
                halves[s].append(jnp.concatenate(tiles[s], axis=0))
        for s in range(2):
            lhs = jnp.concatenate(halves[s], axis=1).astype(BF16)
            acc[s] = acc[s] + jnp.dot(lhs, w_ref[s, lp], preferred_element_type=F32)
    for s in range(2):
        a_ref[s] = acc[s][:, :HEAD_DIM]
        b_ref[s] = acc[s][:, HEAD_DIM:]


def compress_pages_partial(pool4, page_table, w1_pairs, layer_page0):
    nb, n_pages = page_table.shape
    n_groups = n_pages // CMP_PAGES

    def page_spec(r):
        return pl.BlockSpec((None, PAGE_SIZE, CMP_HEADS, HEAD_DIM),
                            lambda b, j, pt: (layer_page0 + pt[b, j * CMP_PAGES + r], 0, 0, 0))

    out_spec = pl.BlockSpec((None, 2, CMP_SLOT_ROWS, HEAD_DIM), lambda b, j, pt: (b, 0, j, 0))
    out_shape = jax.ShapeDtypeStruct((nb, 2, n_groups * CMP_SLOT_ROWS, HEAD_DIM), F32)
    grid_spec = pltpu.PrefetchScalarGridSpec(
        num_scalar_prefetch=1,
        grid=(nb, n_groups),
        in_specs=[page_spec(r) for r in range(CMP_PAGES)]
        + [pl.BlockSpec(w1_pairs.shape, lambda b, j, pt: (0, 0, 0, 0))],
        out_specs=[out_spec, out_spec],
    )
    return pl.pallas_call(
        _cmp_pages_kernel,
        grid_spec=grid_spec,
        out_shape=[out_shape, out_shape],
        compiler_params=_params(("arbitrary", "arbitrary")),
        name="compress_pages_partial",
    )(page_table, *([pool4] * CMP_PAGES), w1_pairs)


def _cmp_finish_kernel(a_ref, b_ref, w2_ref, pet_ref, o_ref, scr_ref):
    rows = a_ref.shape[0]
    b_next = pltpu.roll(b_ref[...], rows - NSA_KV_HEADS, 0)
    h = jax.nn.gelu(a_ref[...] + b_next + pet_ref[0:1, :])
    res = jnp.dot(h.astype(BF16), w2_ref[...].astype(BF16), preferred_element_type=F32)
    row = lax.broadcasted_iota(jnp.int32, (rows, HEAD_DIM), 0)
    scr_ref[...] = jnp.where(row >= rows - NSA_KV_HEADS, 0.0, res)
    for head in range(NSA_KV_HEADS):
        o_ref[head] = scr_ref[pl.ds(head, rows // NSA_KV_HEADS, stride=NSA_KV_HEADS), :]


def compress_pages_finish(a, b, w2, pet, layer):
    nb, _, rows, _ = a.shape
    blk = (None, None, rows, HEAD_DIM)
    return pl.pallas_call(
        _cmp_finish_kernel,
        grid=(nb, 2),
        in_specs=[pl.BlockSpec(blk, lambda b_, s: (b_, s, 0, 0)),
                  pl.BlockSpec(blk, lambda b_, s: (b_, s, 0, 0)),
                  pl.BlockSpec((None, None, HEAD_DIM, HEAD_DIM), lambda b_, s: (layer, s, 0, 0)),
                  pl.BlockSpec((None, 8, HEAD_DIM), lambda b_, s: (s, 0, 0))],
        out_specs=pl.BlockSpec((None, None, NSA_KV_HEADS, rows // NSA_KV_HEADS, HEAD_DIM),
                               lambda b_, s: (b_, s, 0, 0, 0)),
        out_shape=jax.ShapeDtypeStruct((nb, 2, NSA_KV_HEADS, rows // NSA_KV_HEADS, HEAD_DIM), F32),
        scratch_shapes=[pltpu.VMEM((rows, HEAD_DIM), F32)],
        compiler_params=_params(("arbitrary", "arbitrary")),
        name="compress_pages_finish",
    )(a, b, w2, pet)


def _split3(x):
    hi = x.astype(BF16)
    r1 = x - hi.astype(F32)
    mid = r1.astype(BF16)
    lo = (r1 - mid.astype(F32)).astype(BF16)
    return hi, mid, lo


def _cmp_attn_kernel(slopes_ref, q_ref, kc_ref, vc_ref, ov_ref, o_ref, sel_ref, *, tb, nc, n_sel, pos0, emit_idx):
    kvh = pl.program_id(1)
    i = pl.program_id(2)
    g_count = N_HEADS // NSA_KV_HEADS
    ncp = kc_ref.shape[0]
    qs = _stack_heads(q_ref[...], g_count).astype(BF16)
    kc = kc_ref[...].astype(BF16)
    vc = vc_ref[...].astype(BF16)
    s = lax.dot_general(qs, kc, (((1,), (1,)), ((), ())), preferred_element_type=F32) * SCALE
    t = pos0 + i * tb + lax.broadcasted_iota(jnp.int32, (tb, ncp), 0)
    c = lax.broadcasted_iota(jnp.int32, (tb, ncp), 1)
    valid = (c * CMP_STRIDE + (CMP_LEN - 1) <= t) & (c < nc)
    rel = t.astype(F32) - ((c * CMP_STRIDE).astype(F32) + (CMP_LEN - 1) / 2)
    psum = jnp.zeros((tb, ncp), F32)
    outs = []
    for g in range(g_count):
        sg = s[g * tb:(g + 1) * tb] - slopes_ref[kvh * g_count + g] * rel
        sg = jnp.where(valid, sg, 2 * NEG_FILL)
        m = jnp.maximum(jnp.max(sg, axis=-1, keepdims=True), NEG_FILL)
        p = jnp.exp(sg - m)
        p = p / jnp.maximum(jnp.sum(p, axis=-1, keepdims=True), 1e-30)
        psum = psum + p
        outs.append(jnp.dot(p.astype(BF16), vc, preferred_element_type=F32))
    o_ref[...] = jnp.concatenate(outs, axis=1)

    ov = ov_ref[...]
    if emit_idx:
        shape, t_dim, j_dim = (tb, ov.shape[1]), 0, 1
        imp = sum(jnp.dot(part, ov, preferred_element_type=F32) for part in _split3(psum))
    else:
        shape, t_dim, j_dim = (ov.shape[0], tb), 1, 0
        imp = sum(lax.dot_general(ov, part, (((1,), (1,)), ((), ())), preferred_element_type=F32)
                  for part in _split3(psum))
    tq = pos0 + i * tb + lax.broadcasted_iota(jnp.int32, shape, t_dim)
    j = lax.broadcasted_iota(jnp.int32, shape, j_dim)
    cur = tq // SEL_BLOCK
    forced = (j == 0) | (j == cur) | (j == cur - 1)
    visible = j * SEL_BLOCK <= tq
    rank = jnp.where(forced, 1e9, jnp.where(visible, imp, -1.0))
    rank = jnp.where(j < n_sel, rank, -2.0)
    cnt = jnp.zeros(shape, jnp.int32)
    for jp in range(n_sel):
        one = rank[:, jp:jp + 1] if emit_idx else rank[jp:jp + 1, :]
        beats = (one > rank) | ((one == rank) & (j > jp))
        cnt = cnt + beats.astype(jnp.int32)
    if emit_idx:
        lane = lax.broadcasted_iota(jnp.int32, (tb, LANE), 1)
        idx = jnp.zeros((tb, LANE), jnp.int32)
        jf = j.astype(F32)
        for r in range(SEL_TOPN):
            val = jnp.sum(jnp.where(cnt == r, jf, 0.0), axis=1, keepdims=True).astype(jnp.int32)
            idx = jnp.where(lane == r, val, idx)
        sel_ref[...] = idx
    else:
        chosen = (cnt < SEL_TOPN).astype(F32)
        chosen = jnp.concatenate([chosen, jnp.zeros((LANE - shape[0], tb), F32)], axis=0)
        sel_ref[...] = jnp.transpose(chosen)


def _overlap_matrix(ncp, n_sel, nsp):
    c_start = jnp.arange(ncp, dtype=jnp.int32)[:, None] * CMP_STRIDE
    j = jnp.arange(nsp, dtype=jnp.int32)[None, :]
    ov = jnp.clip(jnp.minimum(c_start + CMP_LEN, (j + 1) * SEL_BLOCK) - jnp.maximum(c_start, j * SEL_BLOCK), 0, None)
    ov = jnp.where(j < n_sel, ov, 0)
    return (ov.astype(F32) / CMP_LEN).astype(BF16)


def cmp_attention_prompt(proj, cmp_kv, slopes, *, batch, seq):
    g_count = N_HEADS // NSA_KV_HEADS
    tb = BAND_BLOCK
    nb = seq // tb
    ncp = cmp_kv.shape[3]
    n_sel = -(-seq // SEL_BLOCK)
    nsp = LANE
    assert tb == LANE and n_sel % 8 == 0
    ov_t = jnp.transpose(_overlap_matrix(ncp, n_sel, nsp))[:n_sel]
    smem = pl.BlockSpec(memory_space=pltpu.SMEM)
    kv_block = (None, None, None, ncp, HEAD_DIM)
    return pl.pallas_call(
        functools.partial(_cmp_attn_kernel, tb=tb, nc=ncp - 1, n_sel=n_sel, pos0=0, emit_idx=False),
        grid=(batch, NSA_KV_HEADS, nb),
        in_specs=[smem,
                  pl.BlockSpec((tb, g_count * HEAD_DIM), lambda b, h, i: (b * nb + i, h)),
                  pl.BlockSpec(kv_block, lambda b, h, i: (b, 0, h, 0, 0)),
                  pl.BlockSpec(kv_block, lambda b, h, i: (b, 1, h, 0, 0)),
                  pl.BlockSpec((n_sel, ncp), lambda b, h, i: (0, 0))],
        out_specs=[pl.BlockSpec((tb, g_count * HEAD_DIM), lambda b, h, i: (b * nb + i, h)),
                   pl.BlockSpec((None, None, tb, nsp), lambda b, h, i: (b, h, i, 0))],
        out_shape=[jax.ShapeDtypeStruct((batch * seq, N_HEADS * HEAD_DIM), F32),
                   jax.ShapeDtypeStruct((batch, NSA_KV_HEADS, seq, nsp), F32)],
        compiler_params=_params(("arbitrary", "arbitrary", "arbitrary")),
        name="cmp_attention_prompt",
    )(slopes, proj, cmp_kv, cmp_kv, ov_t)


def cmp_attention_sample(qkv8, cmp_kv, slopes, *, lk):
    g_count = N_HEADS // NSA_KV_HEADS
    nb = qkv8.shape[0]
    ncp = cmp_kv.shape[3]
    n_sel = -(-lk // SEL_BLOCK)
    nsp = -(-n_sel // LANE) * LANE
    ov = _overlap_matrix(ncp, n_sel, nsp)
    smem = pl.BlockSpec(memory_space=pltpu.SMEM)
    kv_block = (None, None, None, ncp, HEAD_DIM)
    return pl.pallas_call(
        functools.partial(_cmp_attn_kernel, tb=DEC_PAD, nc=ncp - 1, n_sel=n_sel, pos0=PAST_LEN, emit_idx=True),
        grid=(nb, NSA_KV_HEADS, 1),
        in_specs=[smem,
                  pl.BlockSpec((None, DEC_PAD, g_count * HEAD_DIM), lambda b, h, i: (b, 0, h)),
                  pl.BlockSpec(kv_block, lambda b, h, i: (b, 0, h, 0, 0)),
                  pl.BlockSpec(kv_block, lambda b, h, i: (b, 1, h, 0, 0)),
                  pl.BlockSpec((ncp, nsp), lambda b, h, i: (0, 0))],
        out_specs=[pl.BlockSpec((None, DEC_PAD, g_count * HEAD_DIM), lambda b, h, i: (b, 0, h)),
                   pl.BlockSpec((None, None, DEC_PAD, LANE), lambda b, h, i: (b, h, 0, 0))],
        out_shape=[jax.ShapeDtypeStruct((nb, DEC_PAD, N_HEADS * HEAD_DIM), F32),
                   jax.ShapeDtypeStruct((nb, NSA_KV_HEADS, DEC_PAD, LANE), jnp.int32)],
        compiler_params=_params(("arbitrary", "arbitrary", "arbitrary")),
        name="cmp_attention_sample",
    )(slopes, qkv8, cmp_kv, cmp_kv, ov)


SEL_KEY_TILE = 256

def _sel_prompt_kernel(slopes_ref, q_ref, k_ref, v_ref, sel_ref, ex_ref, oc_ref, ow_ref, gl_ref, gb_ref, o_ref,
                       m_ref, l_ref, acc_ref, chosen_ref):
    kvh = pl.program_id(1)
    i = pl.program_id(2)
    g_count = N_HEADS // NSA_KV_HEADS
    tb = BAND_BLOCK
    qs = _stack_heads(q_ref[...], g_count).astype(BF16)
    sel = sel_ref[...].astype(BF16)
    slope_col = jnp.concatenate(
        [jnp.full((tb, 1), slopes_ref[kvh * g_count + g], F32) for g in range(g_count)], axis=0)
    kt_w = SEL_KEY_TILE
    m_ref[...] = jnp.full(m_ref.shape, NEG_FILL, F32)
    l_ref[...] = jnp.zeros(l_ref.shape, F32)
    acc_ref[...] = jnp.zeros(acc_ref.shape, F32)
    for kt in range(chosen_ref.shape[0]):
        chosen_ref[kt] = jnp.dot(sel, ex_ref[:, kt * kt_w:(kt + 1) * kt_w], preferred_element_type=F32)
    tq = i * tb + lax.broadcasted_iota(jnp.int32, (tb, kt_w), 0)
    kk = lax.broadcasted_iota(jnp.int32, (tb, kt_w), 1)

    def attend(k0, dist):
        ks = k_ref[pl.ds(k0, kt_w), :].astype(BF16)
        vs = v_ref[pl.ds(k0, kt_w), :].astype(BF16)
        s = lax.dot_general(qs, ks, (((1,), (1,)), ((), ())), preferred_element_type=F32) * SCALE
        distf = jnp.concatenate([dist] * g_count, axis=0)
        s = jnp.where(distf >= 0.0, s - slope_col * distf, 2 * NEG_FILL)
        m_old = m_ref[...]
        m_new = jnp.maximum(m_old, jnp.max(s, axis=-1, keepdims=True))
        alpha = jnp.exp(m_old - m_new)
        p = jnp.exp(s - jnp.concatenate([m_new] * (kt_w // LANE), axis=1))
        l_ref[...] = alpha * l_ref[...] + jnp.sum(p, axis=-1, keepdims=True)
        acc_ref[...] = alpha * acc_ref[...] + jnp.dot(p.astype(BF16), vs, preferred_element_type=F32)
        m_ref[...] = m_new

    def body(kt, carry):
        k0 = pl.multiple_of(kt * kt_w, kt_w)
        dist = (tq - (k0 + kk)).astype(F32)
        dist = jnp.where(chosen_ref[kt] > 0.5, dist, -1.0)

        @pl.when(jnp.max(dist) >= 0.0)
        def _():
            attend(k0, dist)

        return carry

    n_tiles = (i * tb + tb + kt_w - 1) // kt_w
    lax.fori_loop(0, n_tiles, body, 0)
    o_sel = acc_ref[...] / jnp.maximum(l_ref[...], 1e-30)
    gate = jax.nn.sigmoid(gl_ref[...] + gb_ref[...])
    gate = pltpu.roll(gate, jnp.where(kvh == 0, 0, LANE - kvh * g_count), 1)
    for g in range(g_count):
        sl = slice(g * HEAD_DIM, (g + 1) * HEAD_DIM)
        og = (gate[:, g:g + 1] * oc_ref[:, sl] + gate[:, N_HEADS + g:N_HEADS + g + 1] * o_sel[g * tb:(g + 1) * tb]
              + gate[:, 2 * N_HEADS + g:2 * N_HEADS + g + 1] * ow_ref[:, sl])
        o_ref[:, sl] = og.astype(o_ref.dtype)


def sel_attention_prompt(proj, sel_mask, slopes, o_cmp, o_win, gate_logits, gate_bias, *, batch, seq, k_col0, v_col0):
    g_count = N_HEADS // NSA_KV_HEADS
    tb = BAND_BLOCK
    nb = seq // tb
    kb0, vb0 = k_col0 // HEAD_DIM, v_col0 // HEAD_DIM
    smem = pl.BlockSpec(memory_space=pltpu.SMEM)
    block_of_key = jnp.arange(seq, dtype=jnp.int32)[None, :] // SEL_BLOCK
    expand = (jnp.arange(LANE, dtype=jnp.int32)[:, None] == block_of_key).astype(BF16)
    group_tile = pl.BlockSpec((tb, g_count * HEAD_DIM), lambda b, h, i: (b * nb + i, h))
    return pl.pallas_call(
        _sel_prompt_kernel,
        grid=(batch, NSA_KV_HEADS, nb),
        in_specs=[smem,
                  group_tile,
                  pl.BlockSpec((seq, HEAD_DIM), lambda b, h, i: (b, kb0 + h)),
                  pl.BlockSpec((seq, HEAD_DIM), lambda b, h, i: (b, vb0 + h)),
                  pl.BlockSpec((None, None, tb, LANE), lambda b, h, i: (b, h, i, 0)),
                  pl.BlockSpec((LANE, seq), lambda b, h, i: (0, 0)),
                  group_tile,
                  group_tile,
                  pl.BlockSpec((tb, LANE), lambda b, h, i: (b * nb + i, 0)),
                  pl.BlockSpec((1, LANE), lambda b, h, i: (0, 0))],
        out_specs=group_tile,
        out_shape=jax.ShapeDtypeStruct((batch * seq, N_HEADS * HEAD_DIM), BF16),
        scratch_shapes=[pltpu.VMEM((g_count * tb, HEAD_DIM), F32)] * 3
        + [pltpu.VMEM((seq // SEL_KEY_TILE, tb, SEL_KEY_TILE), F32)],
        compiler_params=_params(("arbitrary", "arbitrary", "arbitrary")),
        name="sel_attention_prompt",
    )(slopes, proj, proj, proj, sel_mask, expand, o_cmp, o_win, gate_logits, gate_bias)


def _sel_sample_kernel(idx_ref, pt_ref, q_ref, *refs, n_past_blocks):
    blocks = refs[:SEL_TOPN]
    kn_ref, vn_ref, slope_ref, o_ref = refs[SEL_TOPN:]
    b, kvh, t = pl.program_id(0), pl.program_id(1), pl.program_id(2)
    base = ((b * NSA_KV_HEADS + kvh) * DEC_SEQ + t) * SEL_TOPN
    g_count = N_HEADS // NSA_KV_HEADS
    pad = jnp.zeros((LANE - DEC_PAD, HEAD_DIM), F32)
    kall = jnp.concatenate([r[:, kvh, :] for r in blocks] + [kn_ref[...], pad], axis=0).astype(BF16)
    vall = jnp.concatenate([r[:, NSA_KV_HEADS + kvh, :] for r in blocks] + [vn_ref[...], pad], axis=0).astype(BF16)
    q = q_ref[...].astype(BF16)
    s = lax.dot_general(q, kall, (((1,), (1,)), ((), ())), preferred_element_type=F32) * SCALE
    lane = lax.broadcasted_iota(jnp.int32, (g_count, LANE), 1)
    low = lane < SEL_BLOCK
    qpos = PAST_LEN + t
    pos_parts, ok_parts = [], []
    for c in range(SEL_TOPN // 2):
        b0 = idx_ref[base + 2 * c]
        b1 = idx_ref[base + 2 * c + 1]
        p0 = jnp.where(b0 < n_past_blocks, b0 * SEL_BLOCK, PAST_LEN + DEC_SEQ)
        p1 = jnp.where(b1 < n_past_blocks, b1 * SEL_BLOCK, PAST_LEN + DEC_SEQ)
        pos_parts.append(jnp.where(low, p0 + lane, p1 + (lane - SEL_BLOCK)))
    pos_parts.append(PAST_LEN + lane)
    dist = qpos - jnp.concatenate(pos_parts, axis=1)
    valid = dist >= 0
    slope = jnp.concatenate([slope_ref[...]] * (SEL_TOPN // 2 + 1), axis=1)
    o_ref[...] = _softmax_pv(s - slope * dist.astype(F32), valid, vall)


def sel_attention_sample(q_s, pool_halves, page_table, sel_idx, new_kv8, slope_rows, layer_half0):
    nb = q_s.shape[0]
    g_count = N_HEADS // NSA_KV_HEADS
    n_past_blocks = PAST_LEN // SEL_BLOCK
    per_page = PAGE_SIZE // SEL_BLOCK
    page_shift = per_page.bit_length() - 1
    assert per_page == 1 << page_shift

    def blk_spec(r):
        def index(b, h, t, idx, pt):
            blk = jnp.minimum(idx[((b * NSA_KV_HEADS + h) * DEC_SEQ + t) * SEL_TOPN + r], n_past_blocks - 1)
            page = pt[b, lax.shift_right_logical(blk, page_shift)]
            return (layer_half0 + page * per_page + jnp.bitwise_and(blk, per_page - 1), 0, 1, 0)
        return pl.BlockSpec((None, SEL_BLOCK, CMP_HEADS, HEAD_DIM), index)

    q_block = (None, None, None, g_count, HEAD_DIM)
    new_block = (None, None, None, DEC_PAD, HEAD_DIM)
    grid_spec = pltpu.PrefetchScalarGridSpec(
        num_scalar_prefetch=2,
        grid=(nb, NSA_KV_HEADS, DEC_SEQ),
        in_specs=[pl.BlockSpec(q_block, lambda b, h, t, idx, pt: (b, t, h, 0, 0))]
        + [blk_spec(r) for r in range(SEL_TOPN)]
        + [pl.BlockSpec(new_block, lambda b, h, t, idx, pt: (b, 2, h, 0, 0)),
           pl.BlockSpec(new_block, lambda b, h, t, idx, pt: (b, 3, h, 0, 0)),
           pl.BlockSpec((None, g_count, LANE), lambda b, h, t, idx, pt: (h, 0, 0))],
        out_specs=pl.BlockSpec(q_block, lambda b, h, t, idx, pt: (b, t, h, 0, 0)),
    )
    return pl.pallas_call(
        functools.partial(_sel_sample_kernel, n_past_blocks=n_past_blocks),
        grid_spec=grid_spec,
        out_shape=jax.ShapeDtypeStruct(q_s.shape, F32),
        compiler_params=_params(("arbitrary", "arbitrary", "arbitrary")),
        name="sel_attention_sample",
    )(sel_idx, page_table, q_s, *([pool_halves] * SEL_TOPN), new_kv8, new_kv8, slope_rows)


def _combine_kernel(oc_ref, os_ref, ow_ref, gl_ref, gb_ref, o_ref):
    gate = jax.nn.sigmoid(gl_ref[...] + gb_ref[...])
    for h in range(N_HEADS):
        sl = slice(h * HEAD_DIM, (h + 1) * HEAD_DIM)
        o = (gate[:, h:h + 1] * oc_ref[:, sl] + gate[:, N_HEADS + h:N_HEADS + h + 1] * os_ref[:, sl]
             + gate[:, 2 * N_HEADS + h:2 * N_HEADS + h + 1] * ow_ref[:, sl])
        o_ref[:, sl] = o.astype(o_ref.dtype)


def gate_combine(o_c, o_s, o_w, gate_logits, gate_bias, tm, row_block0=0):
    m, n = o_c.shape
    big = pl.BlockSpec((tm, n), lambda i: (i, 0))
    return pl.pallas_call(
        _combine_kernel,
        grid=(m // tm,),
        in_specs=[big, big, big,
                  pl.BlockSpec((tm, LANE), lambda i: (row_block0 + i, 0)),
                  pl.BlockSpec((1, LANE), lambda i: (0, 0))],
        out_specs=big,
        out_shape=jax.ShapeDtypeStruct((m, n), BF16),
        compiler_params=_params(("arbitrary",)),
        name="gate_combine",
    )(o_c, o_s, o_w, gate_logits, gate_bias)


TM_DENSE = 1664
TN_PROJ = 256
TF_FFN = 256
TM_FFN_OUT = 832
TN_FFN_OUT = 256
TM_NORM = 320
TM_PROMPT = 256
NQ = N_HEADS * HEAD_DIM


def _ffn_block(x, g, w_in, w_out_bf, layer, sub):
    xn = rms_norm(x, g, BF16, TM_NORM)
    h = ffn_in(xn, w_in, layer, sub, TM_DENSE, TF_FFN)
    return ffn_out(h, w_out_bf, layer, sub, x, TM_FFN_OUT, TN_FFN_OUT)


def _sample_rows8(proj):
    s = proj[M_PROMPT:].reshape(DEC_BATCH, DEC_SEQ, proj.shape[1])
    return jnp.pad(s, ((0, 0), (0, DEC_PAD - DEC_SEQ), (0, 0)))


def _swa_layer(x, g, w_in, w_out, sinks_all, cache_all, li, slopes):
    xn = rms_norm(x, g, BF16, TM_NORM)
    kv_cols = 2 * SWA_KV_HEADS * HEAD_DIM
    proj = matmul_wres(xn, w_in, (li,), NQ + kv_cols, TM_DENSE, TN_PROJ)
    sinks = sinks_all[li].astype(F32)
    o_p = band_attention(proj, slopes, sinks, batch=BATCH, seq=SEQ, kv_heads=SWA_KV_HEADS, k_col0=NQ,
                         v_col0=NQ + SWA_KV_HEADS * HEAD_DIM, window=SWA_WINDOW, use_sink=True, out_dtype=BF16)
    cache = cache_all[li]
    lc = cache.shape[1]
    o_s = decode_attention(_sample_rows8(proj), cache.reshape(DEC_BATCH, lc * 2 * SWA_KV_HEADS, HEAD_DIM), slopes, sinks,
                           kv_heads=SWA_KV_HEADS, k_col0=NQ, v_col0=NQ + SWA_KV_HEADS * HEAD_DIM,
                           window=SWA_WINDOW, use_sink=True, out_dtype=BF16)
    attn = jnp.concatenate([o_p, o_s[:, :DEC_SEQ].reshape(M_SAMPLE, NQ)], axis=0)
    x = matmul_wres(attn, w_out, (li,), D_MODEL, TM_DENSE, TN_PROJ, res=x)
    kv_p = proj[:M_PROMPT, NQ:].reshape(BATCH, SEQ, 2, SWA_KV_HEADS, HEAD_DIM)
    kv_s = proj[M_PROMPT:, NQ:].reshape(DEC_BATCH, DEC_SEQ, 2, SWA_KV_HEADS, HEAD_DIM)
    buf_p = kv_p[:, SEQ - min(SWA_WINDOW, SEQ):]
    buf_s = jnp.concatenate([cache, kv_s], axis=1)[:, DEC_SEQ:]
    return x, buf_p, buf_s


def _nsa_layer(x, g, w_in, gate_b, cmp_pe, cmp_w1, cmp_w2, w_out, win_cache_all, pool_all, page_table, li, slopes):
    xn = rms_norm(x, g, BF16, TM_NORM)
    kvw = NSA_KV_HEADS * HEAD_DIM
    main_cols = NQ + 6 * kvw
    proj = matmul_wres(xn, w_in, (li,), main_cols, TM_DENSE, TN_PROJ)
    n_gate = 3 * N_HEADS
    w_gate = jnp.pad(w_in[li][:, main_cols:], ((0, 0), (0, LANE - n_gate)))
    gate_logits = matmul_wres(xn, w_gate, (), LANE, TM_DENSE, LANE)
    gate_bias = jnp.pad(gate_b[li], (0, LANE - n_gate)).reshape(1, LANE)
    zeros = jnp.zeros((N_HEADS,), F32)

    pet = pe_term(cmp_pe, cmp_w1, li)
    w1 = cmp_w1[li].astype(BF16)
    top = jnp.concatenate([w1[:, 0:CMP_STRIDE:2], w1[:, CMP_STRIDE::2]], axis=-1)
    bot = jnp.concatenate([w1[:, 1:CMP_STRIDE:2], w1[:, CMP_STRIDE + 1::2]], axis=-1)
    w1_pairs = jnp.concatenate([top, bot], axis=-2)

    o_w_p = band_attention(proj, slopes, zeros, batch=BATCH, seq=SEQ, kv_heads=NSA_KV_HEADS, k_col0=NQ + 4 * kvw,
                           v_col0=NQ + 5 * kvw, window=NSA_WINDOW, use_sink=False, out_dtype=F32)
    cmp_p = compress_prompt(proj, cmp_w1, cmp_w2, pet, li, batch=BATCH, seq=SEQ, col0=NQ)
    o_c_p, sel_mask = cmp_attention_prompt(proj, cmp_p, slopes, batch=BATCH, seq=SEQ)
    attn_p = sel_attention_prompt(proj, sel_mask, slopes, o_c_p, o_w_p, gate_logits, gate_bias, batch=BATCH, seq=SEQ,
                                  k_col0=NQ + 2 * kvw, v_col0=NQ + 3 * kvw)

    qkv8 = _sample_rows8(proj)
    win_cache = win_cache_all[li]
    lc = win_cache.shape[1]
    o_w_s = decode_attention(qkv8, win_cache.reshape(DEC_BATCH, lc * 2 * NSA_KV_HEADS, HEAD_DIM), slopes, zeros,
                             kv_heads=NSA_KV_HEADS, k_col0=NQ + 4 * kvw, v_col0=NQ + 5 * kvw, window=NSA_WINDOW,
                             use_sink=False, out_dtype=F32)
    n_pool = pool_all.shape[1]
    pool = pool_all.reshape(pool_all.shape[0] * n_pool, PAGE_SIZE, 4 * kvw)
    pool4 = pool.reshape(pool.shape[0], PAGE_SIZE, 2 * CMP_HEADS, HEAD_DIM)
    part_a, part_b = compress_pages_partial(pool4, page_table, w1_pairs, li * n_pool)
    cmp_s = compress_pages_finish(part_a, part_b, cmp_w2, pet, li)
    o_c_s, sel_idx = cmp_attention_sample(qkv8, cmp_s, slopes, lk=PAST_LEN + DEC_SEQ)
    sel_flat = sel_idx[:, :, :DEC_SEQ, :SEL_TOPN].reshape(-1)
    rows_s = proj[M_PROMPT:, NQ:NQ + 4 * kvw].reshape(DEC_BATCH, DEC_SEQ, 4, NSA_KV_HEADS, HEAD_DIM)
    new_kv8 = jnp.pad(rows_s.transpose(0, 2, 3, 1, 4), ((0, 0), (0, 0), (0, 0), (0, DEC_PAD - DEC_SEQ), (0, 0)))
    g_count = N_HEADS // NSA_KV_HEADS
    q_s = proj[M_PROMPT:, :NQ].reshape(DEC_BATCH, DEC_SEQ, NSA_KV_HEADS, g_count, HEAD_DIM)
    slope_rows = jnp.broadcast_to(slopes.reshape(NSA_KV_HEADS, g_count, 1), (NSA_KV_HEADS, g_count, LANE))
    per_page = PAGE_SIZE // SEL_BLOCK
    pool_halves = pool.reshape(pool.shape[0] * per_page, SEL_BLOCK, 2 * CMP_HEADS, HEAD_DIM)
    o_s_s = sel_attention_sample(q_s, pool_halves, page_table, sel_flat, new_kv8, slope_rows, li * n_pool * per_page)
    attn_s = gate_combine(o_c_s[:, :DEC_SEQ].reshape(M_SAMPLE, NQ), o_s_s.reshape(M_SAMPLE, NQ),
                          o_w_s[:, :DEC_SEQ].reshape(M_SAMPLE, NQ), gate_logits, gate_bias, M_SAMPLE,
                          row_block0=M_PROMPT // M_SAMPLE)

    attn = jnp.concatenate([attn_p, attn_s], axis=0)
    x = matmul_wres(attn, w_out, (li,), D_MODEL, TM_DENSE, TN_PROJ, res=x)

    kv_p = proj[:M_PROMPT, NQ:main_cols].reshape(BATCH, SEQ, 6, NSA_KV_HEADS, HEAD_DIM)
    kv_s = proj[M_PROMPT:, NQ:main_cols].reshape(DEC_BATCH, DEC_SEQ, 6, NSA_KV_HEADS, HEAD_DIM)
    win_p = kv_p[:, SEQ - min(NSA_WINDOW, SEQ):, 4:]
    win_s = jnp.concatenate([win_cache, kv_s[:, :, 4:]], axis=1)[:, DEC_SEQ:]
    return x, kv_p[:, :, :4], kv_s[:, :, :4], win_p, win_s


def kernel(x_prompt, x_sample, cache_swa_kv, cache_nsa_win_kv, cache_nsa_kv, page_table, norm_g, final_norm_g,
           ffn_w_in, ffn_w_out, swa_w_in, swa_w_out, swa_sinks, nsa_w_in, nsa_gate_b, nsa_cmp_pe, nsa_cmp_w1,
           nsa_cmp_w2, nsa_w_out):
    x = jnp.concatenate([x_prompt.reshape(M_PROMPT, D_MODEL), x_sample.reshape(M_SAMPLE, D_MODEL)], axis=0)
    slopes = jnp.exp2(-8.0 * jnp.arange(1, N_HEADS + 1, dtype=F32) / N_HEADS)
    w_out_bf = ffn_w_out.astype(BF16)
    swa_p, swa_s, win_p, win_s, kv_p, kv_s = [], [], [], [], [], []
    for i in range(DEPTH):
        x = _ffn_block(x, norm_g[i, 0], ffn_w_in, w_out_bf, i, 0)
        li = i // N_MIXERS
        if i % N_MIXERS == 0:
            x, bp, bs = _swa_layer(x, norm_g[i, 1], swa_w_in, swa_w_out, swa_sinks, cache_swa_kv, li, slopes)
            swa_p.append(bp)
            swa_s.append(bs)
        else:
            x, rp, rs, wp, ws = _nsa_layer(x, norm_g[i, 1], nsa_w_in, nsa_gate_b, nsa_cmp_pe, nsa_cmp_w1,
                                           nsa_cmp_w2, nsa_w_out, cache_nsa_win_kv, cache_nsa_kv, page_table, li,
                                           slopes)
            kv_p.append(rp)
            kv_s.append(rs)
            win_p.append(wp)
            win_s.append(ws)
        x = _ffn_block(x, norm_g[i, 2], ffn_w_in, w_out_bf, i, 1)
    y_p = rms_norm(x, final_norm_g, F32, TM_PROMPT, rows=M_PROMPT).reshape(BATCH, SEQ, D_MODEL)
    y_s = rms_norm(x, final_norm_g, F32, M_SAMPLE, row_block0=M_PROMPT // M_SAMPLE, rows=M_SAMPLE)
    return (y_p, y_s.reshape(DEC_BATCH, DEC_SEQ, D_MODEL), jnp.stack(swa_p), jnp.stack(swa_s), jnp.stack(win_p),
            jnp.stack(win_s), jnp.stack(kv_p), jnp.stack(kv_s))
```
